```python
import math
import jax, jax.numpy as jnp
from jax import lax
import numpy as np

D_MODEL = 4096
BATCH = 4
SEQ = 2048
DEPTH = 2

GRID_W = 64
CTX_LEN = 256
N_ADA = 6
NORM_EPS = 1e-6
ROPE_BASE = 10000.0

RET_HEADS = 16
RET_HEAD_K = D_MODEL // RET_HEADS
RET_HEAD_V = 2 * RET_HEAD_K
RET_QK_DIM = RET_HEADS * RET_HEAD_K
RET_V_DIM = RET_HEADS * RET_HEAD_V
RET_IN_DIM = 2 * RET_QK_DIM + 2 * RET_V_DIM
RET_CHUNK = 128

DIFF_HEAD_DIM = 128
DIFF_HEADS = D_MODEL // (2 * DIFF_HEAD_DIM)
DIFF_QK_DIM = DIFF_HEADS * 2 * DIFF_HEAD_DIM
DIFF_V_DIM = DIFF_HEADS * 2 * DIFF_HEAD_DIM
DIFF_IN_DIM = 2 * DIFF_QK_DIM + DIFF_V_DIM
Q_BLOCK = 128

N_EXPERTS = 64
D_EXPERT = 256
MOE_TOPK = 8
MOE_GROUPS = 8
MOE_TOPK_GROUPS = 4
EXPERTS_PER_GROUP = N_EXPERTS // MOE_GROUPS
D_SHARED = MOE_TOPK * D_EXPERT
ROUTED_SCALE = 2.5

N_RET_LAYERS = (DEPTH + 1) // 2
N_DIFF_LAYERS = DEPTH // 2

kernel_name = "hybrid_retention_diffattn_moe_dit"


def rms_normalize(x):
    xf = x.astype(jnp.float32)
    return xf * lax.rsqrt(jnp.mean(xf * xf, axis=-1, keepdims=True) + NORM_EPS)


def rms_norm(x, w):
    return (rms_normalize(x) * w.astype(jnp.float32)).astype(x.dtype)


def modulate(h, shift, scale):
    return h * (1.0 + scale) + shift


def split_heads(t, n_heads, head_dim):
    return t.reshape(*t.shape[:-1], n_heads, head_dim)


def axial_rope_tables(rows, head_dim):
    row, col = jnp.meshgrid(jnp.arange(rows, dtype=jnp.float32), jnp.arange(GRID_W, dtype=jnp.float32), indexing="ij")
    n_freq = head_dim // 4
    inv_freq = ROPE_BASE ** (-jnp.arange(n_freq, dtype=jnp.float32) / n_freq)
    ang_r = row.reshape(-1, 1) * inv_freq
    ang_c = col.reshape(-1, 1) * inv_freq
    return (jnp.cos(ang_r), jnp.sin(ang_r), jnp.cos(ang_c), jnp.sin(ang_c))


def rope_2d(x, cos_r, sin_r, cos_c, sin_c):
    extra = x.ndim - 3

    def rot(xh, cos, sin):
        cos = cos.reshape(cos.shape[0], *([1] * extra), cos.shape[1])
        sin = sin.reshape(sin.shape[0], *([1] * extra), sin.shape[1])
        x1, x2 = jnp.split(xh, 2, axis=-1)
        return jnp.concatenate([x1 * cos - x2 * sin, x2 * cos + x1 * sin], axis=-1)

    x_row, x_col = jnp.split(x, 2, axis=-1)
    return jnp.concatenate([rot(x_row, cos_r, sin_r), rot(x_col, cos_c, sin_c)], axis=-1).astype(x.dtype)


def retention_chunkwise(q, k, v, log_gamma, state0):
    B, H, T, DK = q.shape
    DV = v.shape[-1]
    L = RET_CHUNK
    n = T // L
    to_chunks = lambda t: jnp.moveaxis(t.reshape(B, H, n, L, t.shape[-1]), 2, 0)
    idx = jnp.arange(L, dtype=jnp.float32)
    rel = idx[:, None] - idx[None, :]
    lg = log_gamma[:, None, None]
    inner_decay = jnp.where(rel >= 0, jnp.exp(lg * jnp.maximum(rel, 0.0)), 0.0)
    q_decay = jnp.exp(log_gamma[:, None] * (idx + 1.0))
    k_decay = jnp.exp(log_gamma[:, None] * (L - 1.0 - idx))
    chunk_decay = jnp.exp(log_gamma * L)

    def step(state, blk):
        qb, kb, vb = blk
        scores = jnp.einsum("bhid,bhjd->bhij", qb, kb) * inner_decay
        inner = jnp.einsum("bhij,bhjv->bhiv", scores, vb)
        cross = jnp.einsum("bhid,bhdv->bhiv", qb, state) * q_decay[..., None]
        new_state = state * chunk_decay[:, None, None] + jnp.einsum("bhjd,bhjv->bhdv", kb * k_decay[..., None], vb)
        return new_state, inner + cross

    _, out = lax.scan(step, state0, (to_chunks(q), to_chunks(k), to_chunks(v)))
    return jnp.moveaxis(out, 0, 2).reshape(B, H, T, DV)


def retention_final_state(k, v, log_gamma):
    T = k.shape[2]
    w = jnp.exp(log_gamma[:, None] * (T - 1.0 - jnp.arange(T, dtype=jnp.float32)))
    return jnp.einsum("bhtk,bhtv->bhkv", k * w[..., None], v).astype(jnp.float32)


def retention_output(o, g, w_out):
    B, H, T, DV = o.shape
    o = jnp.moveaxis(rms_normalize(o), 1, 2).reshape(B, T, H * DV).astype(g.dtype)
    return (jax.nn.silu(g) * o) @ w_out


def retention_mixer(h_lat, h_ctx, w_in, w_out, logit_fwd, logit_bwd, rope, with_ctx_out):
    to_bhtd = lambda t: jnp.moveaxis(t, 2, 1)
    flip = lambda t: jnp.flip(t, axis=2)
    k_scale = RET_HEAD_K ** -0.5
    cuts = [RET_QK_DIM, 2 * RET_QK_DIM, 2 * RET_QK_DIM + RET_V_DIM]
    q, k, v, g = jnp.split(h_lat @ w_in, cuts, axis=-1)
    q = to_bhtd(rope_2d(split_heads(q, RET_HEADS, RET_HEAD_K), *rope))
    k = to_bhtd(rope_2d(split_heads(k, RET_HEADS, RET_HEAD_K), *rope)) * k_scale
    v = to_bhtd(split_heads(v, RET_HEADS, RET_HEAD_V))
    if with_ctx_out:
        qc, kc, vc, gc = jnp.split(h_ctx @ w_in, cuts, axis=-1)
    else:
        kc, vc = jnp.split(h_ctx @ w_in[:, RET_QK_DIM:2 * RET_QK_DIM + RET_V_DIM], [RET_QK_DIM], axis=-1)
    kc = to_bhtd(split_heads(kc, RET_HEADS, RET_HEAD_K)) * k_scale
    vc = to_bhtd(split_heads(vc, RET_HEADS, RET_HEAD_V))
    lg_f = jax.nn.log_sigmoid(logit_fwd.astype(jnp.float32))
    lg_b = jax.nn.log_sigmoid(logit_bwd.astype(jnp.float32))
    state_f = retention_final_state(kc, vc, lg_f)
    state_b = retention_final_state(flip(kc), flip(vc), lg_b)
    o_lat = retention_chunkwise(q, k, v, lg_f, state_f) + flip(
        retention_chunkwise(flip(q), flip(k), flip(v), lg_b, state_b))
    y_lat = retention_output(o_lat, g, w_out)
    if not with_ctx_out:
        return y_lat, None
    qc = to_bhtd(split_heads(qc, RET_HEADS, RET_HEAD_K))
    zero = jnp.zeros(state_f.shape, jnp.float32)
    o_ctx = retention_chunkwise(qc, kc, vc, lg_f, zero) + flip(
        retention_chunkwise(flip(qc), flip(kc), flip(vc), lg_b, zero))
    return y_lat, retention_output(o_ctx, gc, w_out)


def diff_softmax_attend(q, k, v, lam):
    s = jnp.einsum("bqhcd,bkhcd->bhcqk", q, k).astype(jnp.float32) * (DIFF_HEAD_DIM ** -0.5)
    p = jax.nn.softmax(s, axis=-1)
    a = p[:, :, 0] - lam * p[:, :, 1]
    return jnp.einsum("bhqk,bkhv->bqhv", a.astype(v.dtype), v)


def diff_output(o, subln_w, lambda_init, w_out):
    B, T = o.shape[:2]
    o = rms_norm(o, subln_w) * (1.0 - lambda_init)
    return o.reshape(B, T, DIFF_V_DIM) @ w_out


def diff_attention_mixer(h_lat, h_ctx, w_in, w_out, lam_q1, lam_k1, lam_q2, lam_k2, subln_w, lambda_init, rope,
                         with_ctx_out):
    B, T, _ = h_lat.shape
    qk_heads = lambda t: t.reshape(t.shape[0], t.shape[1], DIFF_HEADS, 2, DIFF_HEAD_DIM)
    v_heads = lambda t: split_heads(t, DIFF_HEADS, 2 * DIFF_HEAD_DIM)
    cuts = [DIFF_QK_DIM, 2 * DIFF_QK_DIM]
    q, k, v = jnp.split(h_lat @ w_in, cuts, axis=-1)
    q = rope_2d(qk_heads(q), *rope)
    k = rope_2d(qk_heads(k), *rope)
    v = v_heads(v)
    if with_ctx_out:
        qc, kc, vc = jnp.split(h_ctx @ w_in, cuts, axis=-1)
    else:
        kc, vc = jnp.split(h_ctx @ w_in[:, DIFF_QK_DIM:], [DIFF_QK_DIM], axis=-1)
    kc, vc = qk_heads(kc), v_heads(vc)
    lam = (jnp.exp(jnp.sum(lam_q1.astype(jnp.float32) * lam_k1.astype(jnp.float32)))
           - jnp.exp(jnp.sum(lam_q2.astype(jnp.float32) * lam_k2.astype(jnp.float32))) + lambda_init)
    keys = jnp.concatenate([k, kc], axis=1)
    vals = jnp.concatenate([v, vc], axis=1)
    nb = T // Q_BLOCK
    q_blocks = jnp.moveaxis(q.reshape(B, nb, Q_BLOCK, DIFF_HEADS, 2, DIFF_HEAD_DIM), 1, 0)
    o = lax.map(lambda qb: diff_softmax_attend(qb, keys, vals, lam), q_blocks)
    o = jnp.moveaxis(o, 0, 1).reshape(B, T, DIFF_HEADS, 2 * DIFF_HEAD_DIM)
    y_lat = diff_output(o, subln_w, lambda_init, w_out)
    if not with_ctx_out:
        return y_lat, None
    o_ctx = diff_softmax_attend(qk_heads(qc), kc, vc, lam)
    return y_lat, diff_output(o_ctx, subln_w, lambda_init, w_out)


def diff_lambda_init(layer_idx):
    return 0.8 - 0.6 * math.exp(-0.3 * layer_idx)


def moe_ffn(h, router_w, router_b, w_gate, w_up, w_down, sh_gate, sh_up, sh_down):
    scores = jax.nn.sigmoid((h @ router_w).astype(jnp.float32))
    biased = scores + router_b.astype(jnp.float32)
    grouped = biased.reshape(*biased.shape[:-1], MOE_GROUPS, EXPERTS_PER_GROUP)
    group_score = lax.top_k(grouped, 2)[0].sum(-1)
    _, top_groups = lax.top_k(group_score, MOE_TOPK_GROUPS)
    group_keep = jax.nn.one_hot(top_groups, MOE_GROUPS, dtype=jnp.float32).sum(-2)
    expert_keep = jnp.repeat(group_keep, EXPERTS_PER_GROUP, axis=-1) > 0
    _, top_idx = lax.top_k(jnp.where(expert_keep, biased, -jnp.inf), MOE_TOPK)
    top_w = jnp.take_along_axis(scores, top_idx, axis=-1)
    top_w = top_w / jnp.sum(top_w, axis=-1, keepdims=True) * ROUTED_SCALE
    gates = jnp.einsum("bnk,bnke->bne", top_w, jax.nn.one_hot(top_idx, N_EXPERTS, dtype=jnp.float32))
    hid = jax.nn.silu(jnp.einsum("bnd,edf->bnef", h, w_gate)) * jnp.einsum("bnd,edf->bnef", h, w_up)
    routed = jnp.einsum("bnef,efd->bnd", hid * gates[..., None].astype(hid.dtype), w_down)
    shared = (jax.nn.silu(h @ sh_gate) * (h @ sh_up)) @ sh_down
    return routed + shared


def setup_inputs(seed: int = 0) -> dict:
    key = jax.random.key(seed)
    ks = iter(jax.random.split(key, 32))
    nrm = lambda shape, scale: jax.random.normal(next(ks), shape, jnp.float32) * scale
    gain = lambda shape: 1.0 + nrm(shape, 0.02)
    ret_logit = jnp.asarray(np.log(2.0 ** (5 + np.arange(RET_HEADS)) - 1.0), jnp.float32)
    return {
        "x": nrm((BATCH, SEQ, D_MODEL), 1.0),
        "c": nrm((BATCH, D_MODEL), 1.0),
        "ctx": nrm((BATCH, CTX_LEN, D_MODEL), 1.0),
        "c_ctx": nrm((D_MODEL,), 1.0),
        "ada_w": nrm((DEPTH, D_MODEL, N_ADA * D_MODEL), 0.5 * D_MODEL ** -0.5),
        "ada_b": nrm((DEPTH, N_ADA * D_MODEL), 0.02),
        "norm_pre_mix": gain((DEPTH, D_MODEL)),
        "norm_post_mix": gain((DEPTH, D_MODEL)),
        "norm_pre_ffn": gain((DEPTH, D_MODEL)),
        "norm_post_ffn": gain((DEPTH, D_MODEL)),
        "ret_w_in": nrm((N_RET_LAYERS, D_MODEL, RET_IN_DIM), D_MODEL ** -0.5),
        "ret_w_out": nrm((N_RET_LAYERS, RET_V_DIM, D_MODEL), RET_V_DIM ** -0.5),
        "ret_decay_fwd": ret_logit[None] + nrm((N_RET_LAYERS, RET_HEADS), 0.1),
        "ret_decay_bwd": ret_logit[None] + nrm((N_RET_LAYERS, RET_HEADS), 0.1),
        "diff_w_in": nrm((N_DIFF_LAYERS, D_MODEL, DIFF_IN_DIM), D_MODEL ** -0.5),
        "diff_w_out": nrm((N_DIFF_LAYERS, DIFF_V_DIM, D_MODEL), DIFF_V_DIM ** -0.5),
        "diff_lam_q1": nrm((N_DIFF_LAYERS, DIFF_HEAD_DIM), 0.1),
        "diff_lam_k1": nrm((N_DIFF_LAYERS, DIFF_HEAD_DIM), 0.1),
        "diff_lam_q2": nrm((N_DIFF_LAYERS, DIFF_HEAD_DIM), 0.1),
        "diff_lam_k2": nrm((N_DIFF_LAYERS, DIFF_HEAD_DIM), 0.1),
        "diff_subln_w": gain((N_DIFF_LAYERS, 2 * DIFF_HEAD_DIM)),
        "moe_router_w": nrm((DEPTH, D_MODEL, N_EXPERTS), D_MODEL ** -0.5),
        "moe_router_b": nrm((DEPTH, N_EXPERTS), 0.01),
        "moe_w_gate": nrm((DEPTH, N_EXPERTS, D_MODEL, D_EXPERT), D_MODEL ** -0.5),
        "moe_w_up": nrm((DEPTH, N_EXPERTS, D_MODEL, D_EXPERT), D_MODEL ** -0.5),
        "moe_w_down": nrm((DEPTH, N_EXPERTS, D_EXPERT, D_MODEL), D_EXPERT ** -0.5),
        "moe_shared_gate": nrm((DEPTH, D_MODEL, D_SHARED), D_MODEL ** -0.5),
        "moe_shared_up": nrm((DEPTH, D_MODEL, D_SHARED), D_MODEL ** -0.5),
        "moe_shared_down": nrm((DEPTH, D_SHARED, D_MODEL), D_SHARED ** -0.5),
    }


def reference(x, c, ctx, c_ctx, ada_w, ada_b, norm_pre_mix, norm_post_mix, norm_pre_ffn, norm_post_ffn,
              ret_w_in, ret_w_out, ret_decay_fwd, ret_decay_bwd, diff_w_in, diff_w_out, diff_lam_q1, diff_lam_k1,
              diff_lam_q2, diff_lam_k2, diff_subln_w, moe_router_w, moe_router_b, moe_w_gate, moe_w_up, moe_w_down,
              moe_shared_gate, moe_shared_up, moe_shared_down):
    T = x.shape[1]
    rows = T // GRID_W
    rope_ret = axial_rope_tables(rows, RET_HEAD_K)
    rope_diff = axial_rope_tables(rows, DIFF_HEAD_DIM)
    silu_c = jax.nn.silu(c)
    silu_cc = jax.nn.silu(c_ctx)
    xl, xc = x, ctx
    for i in range(DEPTH):
        last = i == DEPTH - 1
        mod_l = jnp.split((silu_c @ ada_w[i] + ada_b[i])[:, None, :], N_ADA, axis=-1)
        mod_c = jnp.split(silu_cc @ ada_w[i] + ada_b[i], N_ADA, axis=-1)
        hl = modulate(rms_norm(xl, norm_pre_mix[i]), mod_l[0], mod_l[1])
        hc = modulate(rms_norm(xc, norm_pre_mix[i]), mod_c[0], mod_c[1])
        j = i // 2
        if i % 2 == 0:
            yl, yc = retention_mixer(hl, hc, ret_w_in[j], ret_w_out[j], ret_decay_fwd[j], ret_decay_bwd[j],
                                     rope_ret, not last)
        else:
            yl, yc = diff_attention_mixer(hl, hc, diff_w_in[j], diff_w_out[j], diff_lam_q1[j], diff_lam_k1[j],
                                          diff_lam_q2[j], diff_lam_k2[j], diff_subln_w[j], diff_lambda_init(i),
                                          rope_diff, not last)
        xl = xl + mod_l[2] * rms_norm(yl, norm_post_mix[i])
        hl = modulate(rms_norm(xl, norm_pre_ffn[i]), mod_l[3], mod_l[4])
        moe_args = (moe_router_w[i], moe_router_b[i], moe_w_gate[i], moe_w_up[i], moe_w_down[i],
                    moe_shared_gate[i], moe_shared_up[i], moe_shared_down[i])
        if last:
            xl = xl + mod_l[5] * rms_norm(moe_ffn(hl, *moe_args), norm_post_ffn[i])
        else:
            xc = xc + mod_c[2] * rms_norm(yc, norm_post_mix[i])
            hc = modulate(rms_norm(xc, norm_pre_ffn[i]), mod_c[3], mod_c[4])
            f = moe_ffn(jnp.concatenate([hl, hc], axis=1), *moe_args)
            xl = xl + mod_l[5] * rms_norm(f[:, :T], norm_post_ffn[i])
            xc = xc + mod_c[5] * rms_norm(f[:, T:], norm_post_ffn[i])
    return xl
```

```python
import functools
import math

import jax
import jax.numpy as jnp
from jax import lax
from jax.experimental import pallas as pl
from jax.experimental.pallas import tpu as pltpu

GRID_W = 64
N_ADA = 6
NORM_EPS = 1e-6
ROPE_BASE = 10000.0
RET_CHUNK = 128
DIFF_HEAD_DIM = 128
MOE_TOPK = 8
MOE_GROUPS = 8
MOE_TOPK_GROUPS = 4
ROUTED_SCALE = 2.5

LANES = 128
ROW_TILE = 256
EXPERT_TILE = 256
COMBINE_TILE = 64
VMEM_LIMIT = 56 * 1024 * 1024

F32 = jnp.float32
BF16 = jnp.bfloat16


def _pick(dim, candidates):
    for c in candidates:
        if dim % c == 0:
            return c
    raise ValueError(f"no tile in {candidates} divides {dim}")


def _params(sem, vmem=VMEM_LIMIT):
    return pltpu.CompilerParams(dimension_semantics=sem, vmem_limit_bytes=vmem)


def _dot(a, b):
    return jnp.dot(a, b, preferred_element_type=F32)


def _dot_nt(a, b):
    return lax.dot_general(a, b, (((1,), (1,)), ((), ())), preferred_element_type=F32)


def _dot_tn(a, b):
    return lax.dot_general(a, b, (((0,), (0,)), ((), ())), preferred_element_type=F32)


def _silu(x):
    return x * jax.nn.sigmoid(x)


def _pack_halves(y):
    w = y.shape[-1] // 2
    lo = lax.bitcast_convert_type(y[:, :w].astype(BF16).astype(F32), jnp.uint32)
    hi = lax.bitcast_convert_type(y[:, w:].astype(BF16).astype(F32), jnp.uint32)
    return (hi & jnp.uint32(0xFFFF0000)) | (lo >> 16)


def _unpack_halves(p):
    lo = lax.bitcast_convert_type(p << 16, F32)
    hi = lax.bitcast_convert_type(p & jnp.uint32(0xFFFF0000), F32)
    return lo, hi


def _ada_kernel(c_ref, w_ref, b_ref, o_ref):
    a = _silu(c_ref[...]).astype(BF16)
    o_ref[...] = _dot(a, w_ref[...].astype(BF16)) + b_ref[...]


def ada_modulation(cc, ada_w, ada_b):
    depth, d, n = ada_w.shape
    tn = _pick(n, (512, 256, 128))
    out = pl.pallas_call(
        _ada_kernel,
        grid=(depth, n // tn),
        in_specs=[pl.BlockSpec((8, d), lambda l, j: (0, 0)),
                  pl.BlockSpec((None, d, tn), lambda l, j: (l, 0, j)),
                  pl.BlockSpec((None, 1, tn), lambda l, j: (l, 0, j))],
        out_specs=pl.BlockSpec((None, 8, tn), lambda l, j: (l, 0, j)),
        out_shape=jax.ShapeDtypeStruct((depth, 8, n), F32),
        compiler_params=_params(("parallel", "parallel")),
        name="ada_modulation",
    )(cc, ada_w, ada_b.reshape(depth, 1, n))
    return out.reshape(depth, 8, N_ADA, d)


def _rms(x):
    return x * lax.rsqrt(jnp.mean(x * x, axis=-1, keepdims=True) + NORM_EPS)


def _route(h, rw_ref, rb_ref, idx_ref, wgt_ref):
    n_exp = rw_ref.shape[0]
    tm = h.shape[0]
    per_group = n_exp // MOE_GROUPS
    w = rw_ref[...]
    w_hi = w.astype(BF16)
    w_lo = (w - w_hi.astype(F32)).astype(BF16)
    h_hi = h.astype(BF16)
    h_lo = (h - h_hi.astype(F32)).astype(BF16)
    logits = _dot_nt(w_hi, h_hi) + (_dot_nt(w_hi, h_lo) + _dot_nt(w_lo, h_hi))
    scores = jax.nn.sigmoid(logits)
    biased = scores + rb_ref[...]
    neg = jnp.float32(-jnp.inf)
    sub = lax.broadcasted_iota(jnp.int32, (per_group, tm), 0)
    giota = lax.broadcasted_iota(jnp.int32, (MOE_GROUPS, tm), 0)
    gs = jnp.zeros((MOE_GROUPS, tm), F32)
    for g in range(MOE_GROUPS):
        blk = biased[g * per_group:(g + 1) * per_group]
        m1 = jnp.max(blk, axis=0, keepdims=True)
        i1 = jnp.min(jnp.where(blk == m1, sub, per_group), axis=0, keepdims=True)
        m2 = jnp.max(jnp.where(sub == i1, neg, blk), axis=0, keepdims=True)
        gs = jnp.where(giota == g, m1 + m2, gs)
    rank = jnp.zeros((MOE_GROUPS, tm), jnp.int32)
    for j in range(MOE_GROUPS):
        gj = gs[j:j + 1]
        beats = (gj > gs) | ((gj == gs) & (giota > j))
        rank = rank + beats.astype(jnp.int32)
    keep = (rank < MOE_TOPK_GROUPS).astype(F32)
    keep_e = jnp.concatenate(
        [jnp.broadcast_to(keep[g:g + 1], (per_group, tm)) for g in range(MOE_GROUPS)], axis=0)
    masked = jnp.where(keep_e > 0.5, biased, neg)
    eiota = lax.broadcasted_iota(jnp.int32, (n_exp, tm), 0)
    sel_w = []
    for k in range(MOE_TOPK):
        m = jnp.max(masked, axis=0, keepdims=True)
        idx = jnp.min(jnp.where(masked == m, eiota, n_exp), axis=0, keepdims=True)
        hit = eiota == idx
        sel_w.append(jnp.sum(jnp.where(hit, scores, 0.0), axis=0, keepdims=True))
        masked = jnp.where(hit, neg, masked)
        idx_ref[k:k + 1, :] = idx
    total = sel_w[0]
    for k in range(1, MOE_TOPK):
        total = total + sel_w[k]
    for k in range(MOE_TOPK):
        wgt_ref[k:k + 1, :] = sel_w[k] / total * ROUTED_SCALE


def _fused_norm_kernel(*refs, has_resid, has_prenorm, has_router, gate_idx, shift_idx, scale_idx):
    refs = list(refs)
    x_ref = refs.pop(0)
    if has_resid:
        y_ref, mod_a_ref, wpost_ref = refs.pop(0), refs.pop(0), refs.pop(0)
    if has_prenorm:
        mod_b_ref, wpre_ref = refs.pop(0), refs.pop(0)
    if has_router:
        rw_ref, rb_ref = refs.pop(0), refs.pop(0)
    x = x_ref[...]
    if has_resid:
        xo_ref = refs.pop(0)
        y = y_ref[...].astype(F32)
        x = x + mod_a_ref[gate_idx:gate_idx + 1, :] * (_rms(y) * wpost_ref[...])
        xo_ref[...] = x
    if has_prenorm:
        h_ref = refs.pop(0)
        h = (_rms(x) * wpre_ref[...]) * (1.0 + mod_b_ref[scale_idx:scale_idx + 1, :]) \
            + mod_b_ref[shift_idx:shift_idx + 1, :]
        h_ref[...] = h.astype(BF16)
        if has_router:
            hp_ref, idx_ref, wgt_ref = refs.pop(0), refs.pop(0), refs.pop(0)
            hp_ref[...] = _pack_halves(h)
            _route(h, rw_ref, rb_ref, idx_ref, wgt_ref)


def fused_norm(x, x_tile_map, n_out_tiles, mod_row_map, *, y=None, mod_a=None, w_post=None, gate_idx=0,
               mod_b=None, w_pre=None, shift_idx=0, scale_idx=0, router_wt=None, router_b=None):
    d = x.shape[-1]
    tm = ROW_TILE
    has_resid, has_prenorm, has_router = y is not None, mod_b is not None, router_wt is not None
    row = lambda i: (i, 0)
    const = lambda i: (0, 0)
    mod_spec = pl.BlockSpec((None, N_ADA, d), lambda i: (mod_row_map(i), 0, 0))
    vec_spec = pl.BlockSpec((1, d), const)
    args, in_specs = [x], [pl.BlockSpec((tm, d), lambda i: (x_tile_map(i), 0))]
    out_shape, out_specs = [], []
    n_rows = n_out_tiles * tm
    if has_resid:
        args += [y, mod_a, w_post.reshape(1, d)]
        in_specs += [pl.BlockSpec((tm, d), row), mod_spec, vec_spec]
        out_shape.append(jax.ShapeDtypeStruct((n_rows, d), F32))
        out_specs.append(pl.BlockSpec((tm, d), row))
    if has_prenorm:
        args += [mod_b, w_pre.reshape(1, d)]
        in_specs += [mod_spec, vec_spec]
        out_shape.append(jax.ShapeDtypeStruct((n_rows, d), BF16))
        out_specs.append(pl.BlockSpec((tm, d), row))
    if has_router:
        n_exp = router_wt.shape[0]
        args += [router_wt, router_b.reshape(n_exp, 1)]
        in_specs += [pl.BlockSpec((n_exp, d), const), pl.BlockSpec((n_exp, 1), const)]
        out_shape += [jax.ShapeDtypeStruct((n_rows, d // 2), jnp.uint32),
                      jax.ShapeDtypeStruct((MOE_TOPK, n_rows), jnp.int32),
                      jax.ShapeDtypeStruct((MOE_TOPK, n_rows), F32)]
        out_specs += [pl.BlockSpec((tm, d // 2), row),
                      pl.BlockSpec((MOE_TOPK, tm), lambda i: (0, i)),
                      pl.BlockSpec((MOE_TOPK, tm), lambda i: (0, i))]
    kern = functools.partial(_fused_norm_kernel, has_resid=has_resid, has_prenorm=has_prenorm,
                             has_router=has_router, gate_idx=gate_idx, shift_idx=shift_idx,
                             scale_idx=scale_idx)
    return pl.pallas_call(
        kern, grid=(n_out_tiles,), in_specs=in_specs, out_specs=out_specs, out_shape=out_shape,
        compiler_params=_params(("parallel",)), name="fused_norm",
    )(*args)


def _mm_kernel(a_ref, w_ref, o_ref):
    o_ref[...] = _dot(a_ref[...], w_ref[...].astype(BF16)).astype(o_ref.dtype)


def _mm_acc_kernel(a_ref, w_ref, o_ref, acc_ref):
    k = pl.program_id(2)

    @pl.when(k == 0)
    def _():
        acc_ref[...] = jnp.zeros_like(acc_ref)

    acc_ref[...] += _dot(a_ref[...], w_ref[...].astype(BF16))

    @pl.when(k == pl.num_programs(2) - 1)
    def _():
        o_ref[...] = acc_ref[...].astype(o_ref.dtype)


def matmul(a, w, out_dtype=BF16):
    m, k = a.shape
    n = w.shape[1]
    tm = _pick(m, (1024, 768, 512, 256))
    tn = _pick(n, (512, 256, 128))
    tk = _pick(k, (4096, 2048, 1024, 512))
    if tk == k:
        return pl.pallas_call(
            _mm_kernel, grid=(m // tm, n // tn),
            in_specs=[pl.BlockSpec((tm, k), lambda i, j: (i, 0)),
                      pl.BlockSpec((k, tn), lambda i, j: (0, j))],
            out_specs=pl.BlockSpec((tm, tn), lambda i, j: (i, j)),
            out_shape=jax.ShapeDtypeStruct((m, n), out_dtype),
            compiler_params=_params(("parallel", "parallel")), name="matmul",
        )(a, w)
    return pl.pallas_call(
        _mm_acc_kernel, grid=(m // tm, n // tn, k // tk),
        in_specs=[pl.BlockSpec((tm, tk), lambda i, j, l: (i, l)),
                  pl.BlockSpec((tk, tn), lambda i, j, l: (l, j))],
        out_specs=pl.BlockSpec((tm, tn), lambda i, j, l: (i, j)),
        out_shape=jax.ShapeDtypeStruct((m, n), out_dtype),
        scratch_shapes=[pltpu.VMEM((tm, tn), F32)],
        compiler_params=_params(("parallel", "parallel", "arbitrary")), name="matmul_acc",
    )(a, w)


def _rope_tables(t_len, head_dim):
    rows = t_len // GRID_W
    n_freq = head_dim // 4
    row, col = jnp.meshgrid(jnp.arange(rows, dtype=F32), jnp.arange(GRID_W, dtype=F32), indexing="ij")
    inv_freq = ROPE_BASE ** (-jnp.arange(n_freq, dtype=F32) / n_freq)
    ang_r = row.reshape(-1, 1) * inv_freq
    ang_c = col.reshape(-1, 1) * inv_freq
    cr, sr, cc, sc = jnp.cos(ang_r), jnp.sin(ang_r), jnp.cos(ang_c), jnp.sin(ang_c)
    return (jnp.concatenate([cr, cr, cc, cc], axis=-1), jnp.concatenate([-sr, sr, -sc, sc], axis=-1))


def _swap_quarters(x, quarter):
    lane = lax.broadcasted_iota(jnp.int32, x.shape, 1)
    first = (lane % (2 * quarter)) < quarter
    return jnp.where(first, pltpu.roll(x, LANES - quarter, axis=1), pltpu.roll(x, quarter, axis=1))


def _retention_kernel(lgf_ref, lgb_ref, q_ref, k_ref, v_ref, g_ref, cos_ref, sin_ref, o_ref,
                      qr, kr, oacc, state, *, t_len, c_len):
    head = pl.program_id(1)
    L = RET_CHUNK
    s_len = t_len + c_len
    dk = q_ref.shape[-1]
    k_scale = dk ** -0.5
    rt = ROW_TILE

    def rope(x, rows):
        sw = jnp.concatenate([pltpu.roll(x[:, :LANES], LANES // 2, axis=1),
                              pltpu.roll(x[:, LANES:], LANES // 2, axis=1)], axis=1)
        return x * cos_ref[rows, :] + sw * sin_ref[rows, :]

    def rope_body(i, _):
        rows = pl.ds(pl.multiple_of(i * rt, rt), rt)
        qr[rows, :] = rope(q_ref[rows, :].astype(F32), rows).astype(BF16)
        kr[rows, :] = (rope(k_ref[rows, :].astype(F32), rows) * k_scale).astype(BF16)
        return 0

    lax.fori_loop(0, t_len // rt, rope_body, 0)
    ctx_rows = pl.ds(t_len, c_len)
    qr[ctx_rows, :] = q_ref[ctx_rows, :]
    kr[ctx_rows, :] = (k_ref[ctx_rows, :].astype(F32) * k_scale).astype(BF16)

    ii = lax.broadcasted_iota(jnp.int32, (L, L), 0)
    jj = lax.broadcasted_iota(jnp.int32, (L, L), 1)
    rel = (ii - jj).astype(F32)
    idx = lax.broadcasted_iota(jnp.int32, (L, 1), 0).astype(F32)

    def run(lg, reverse, accumulate):
        if reverse:
            dmat = jnp.where(rel <= 0, jnp.exp(lg * jnp.maximum(-rel, 0.0)), 0.0)
            q_decay = jnp.exp(lg * (L - idx))
            k_decay = jnp.exp(lg * idx)
        else:
            dmat = jnp.where(rel >= 0, jnp.exp(lg * jnp.maximum(rel, 0.0)), 0.0)
            q_decay = jnp.exp(lg * (idx + 1.0))
            k_decay = jnp.exp(lg * (L - 1.0 - idx))
        chunk_decay = jnp.exp(lg * L)

        def scan(r0, n_chunks):
            def body(ci, _):
                c = (n_chunks - 1 - ci) if reverse else ci
                rows = pl.ds(pl.multiple_of(r0 + c * L, L), L)
                qb, kb, vb = qr[rows, :], kr[rows, :], v_ref[rows, :]
                scores = _dot_nt(qb, kb) * dmat
                inner = _dot(scores.astype(BF16), vb)
                st = state[...]
                cross = _dot(qb, st.astype(BF16)) * q_decay
                if accumulate:
                    oacc[rows, :] += inner + cross
                else:
                    oacc[rows, :] = inner + cross
                kd = (kb.astype(F32) * k_decay).astype(BF16)
                state[...] = st * chunk_decay + _dot_tn(kd, vb)
                return 0
            lax.fori_loop(0, n_chunks, body, 0)

        state[...] = jnp.zeros_like(state)
        scan(t_len, c_len // L)
        scan(0, t_len // L)

    run(lgf_ref[head], False, False)
    run(lgb_ref[head], True, True)

    def out_body(i, _):
        rows = pl.ds(pl.multiple_of(i * rt, rt), rt)
        o = _rms(oacc[rows, :])
        o_ref[rows, :] = (_silu(g_ref[rows, :].astype(F32)) * o).astype(BF16)
        return 0

    lax.fori_loop(0, s_len // rt, out_body, 0)


def retention(qkvg, lg_f, lg_b, cos, sin, *, batch, t_len, c_len, heads):
    s_len = t_len + c_len
    dk = cos.shape[-1]
    dv = 2 * dk
    assert dk == 2 * LANES
    kern = functools.partial(_retention_kernel, t_len=t_len, c_len=c_len)
    grid_spec = pltpu.PrefetchScalarGridSpec(
        num_scalar_prefetch=2, grid=(batch, heads),
        in_specs=[pl.BlockSpec((s_len, dk), lambda b, h, *_: (b, h)),
                  pl.BlockSpec((s_len, dk), lambda b, h, *_: (b, heads + h)),
                  pl.BlockSpec((s_len, dv), lambda b, h, *_: (b, heads + h)),
                  pl.BlockSpec((s_len, dv), lambda b, h, *_: (b, 2 * heads + h)),
                  pl.BlockSpec((t_len, dk), lambda b, h, *_: (0, 0)),
                  pl.BlockSpec((t_len, dk), lambda b, h, *_: (0, 0))],
        out_specs=pl.BlockSpec((s_len, dv), lambda b, h, *_: (b, h)),
        scratch_shapes=[pltpu.VMEM((s_len, dk), BF16), pltpu.VMEM((s_len, dk), BF16),
                        pltpu.VMEM((s_len, dv), F32), pltpu.VMEM((dk, dv), F32)])
    return pl.pallas_call(
        kern, grid_spec=grid_spec,
        out_shape=jax.ShapeDtypeStruct((batch * s_len, heads * dv), BF16),
        compiler_params=_params(("parallel", "parallel")), name="retention",
    )(lg_f, lg_b, qkvg, qkvg, qkvg, qkvg, cos, sin)


def _diff_attn_kernel(lam_ref, q_ref, k_ref, v_ref, cosq_ref, sinq_ref, cosk_ref, sink_ref, subln_ref,
                      o_ref, kr, *, t_len, c_len, lambda_init):
    hd = DIFF_HEAD_DIM
    quarter = hd // 4
    rt = ROW_TILE

    def rope(x, cos, sin):
        return x * cos + _swap_quarters(x, quarter) * sin

    @pl.when(pl.program_id(2) == 0)
    def _():
        def body(i, _):
            rows = pl.ds(pl.multiple_of(i * rt, rt), rt)
            for c in range(2):
                cols = slice(c * hd, (c + 1) * hd)
                kr[rows, cols] = rope(k_ref[rows, cols].astype(F32), cosk_ref[rows, :],
                                      sink_ref[rows, :]).astype(BF16)
            return 0
        lax.fori_loop(0, t_len // rt, body, 0)
        ctx_rows = pl.ds(t_len, c_len)
        kr[ctx_rows, :] = k_ref[ctx_rows, :]

    lam_v = lam_ref[...]
    lam = (jnp.exp(jnp.sum(lam_v[0:1] * lam_v[1:2], axis=-1, keepdims=True))
           - jnp.exp(jnp.sum(lam_v[2:3] * lam_v[3:4], axis=-1, keepdims=True)) + lambda_init)
    v = v_ref[...]
    outs = []
    for c in range(2):
        cols = slice(c * hd, (c + 1) * hd)
        qc = rope(q_ref[:, cols].astype(F32), cosq_ref[...], sinq_ref[...]).astype(BF16)
        s = _dot_nt(qc, kr[:, cols]) * (hd ** -0.5)
        e = jnp.exp(s - jnp.max(s, axis=-1, keepdims=True))
        denom = jnp.sum(e, axis=-1, keepdims=True)
        outs.append(_dot(e.astype(BF16), v) / denom)
    o = outs[0] - lam * outs[1]
    o_ref[...] = ((_rms(o) * subln_ref[...]) * (1.0 - lambda_init)).astype(BF16)


def diff_attention(qkv, lam_vecs, subln_w, cos, sin, *, batch, t_len, c_len, heads, lambda_init):
    s_len = t_len + c_len
    hd = DIFF_HEAD_DIM
    tq = ROW_TILE
    q_tiles, s_tiles = t_len // tq, s_len // tq
    kern = functools.partial(_diff_attn_kernel, t_len=t_len, c_len=c_len, lambda_init=lambda_init)
    return pl.pallas_call(
        kern, grid=(batch, heads, q_tiles),
        in_specs=[pl.BlockSpec((4, hd), lambda b, h, i: (0, 0)),
                  pl.BlockSpec((tq, 2 * hd), lambda b, h, i: (b * s_tiles + i, h)),
                  pl.BlockSpec((s_len, 2 * hd), lambda b, h, i: (b, heads + h)),
                  pl.BlockSpec((s_len, 2 * hd), lambda b, h, i: (b, 2 * heads + h)),
                  pl.BlockSpec((tq, hd), lambda b, h, i: (i, 0)),
                  pl.BlockSpec((tq, hd), lambda b, h, i: (i, 0)),
                  pl.BlockSpec((t_len, hd), lambda b, h, i: (0, 0)),
                  pl.BlockSpec((t_len, hd), lambda b, h, i: (0, 0)),
                  pl.BlockSpec((1, 2 * hd), lambda b, h, i: (0, 0))],
        out_specs=pl.BlockSpec((tq, 2 * hd), lambda b, h, i: (b * q_tiles + i, h)),
        out_shape=jax.ShapeDtypeStruct((batch * t_len, heads * 2 * hd), BF16),
        scratch_shapes=[pltpu.VMEM((s_len, 2 * hd), BF16)],
        compiler_params=_params(("parallel", "parallel", "arbitrary")), name="diff_attention",
    )(lam_vecs, qkv, qkv, qkv, cos, sin, cos, sin, subln_w.reshape(1, 2 * hd))


def _shared_kernel(x_ref, wg_ref, wu_ref, wd_ref, o_ref, acc_ref):
    f = pl.program_id(1)

    @pl.when(f == 0)
    def _():
        acc_ref[...] = jnp.zeros_like(acc_ref)

    x = x_ref[...]
    hid = _silu(_dot(x, wg_ref[...].astype(BF16))) * _dot(x, wu_ref[...].astype(BF16))
    acc_ref[...] += _dot(hid.astype(BF16), wd_ref[...].astype(BF16))

    @pl.when(f == pl.num_programs(1) - 1)
    def _():
        o_ref[...] = acc_ref[...].astype(o_ref.dtype)


def shared_expert(h, w_gate, w_up, w_down):
    m, d = h.shape
    f_dim = w_gate.shape[1]
    tm = _pick(m, (512, 256))
    tf = _pick(f_dim, (128,))
    return pl.pallas_call(
        _shared_kernel, grid=(m // tm, f_dim // tf),
        in_specs=[pl.BlockSpec((tm, d), lambda i, f: (i, 0)),
                  pl.BlockSpec((d, tf), lambda i, f: (0, f)),
                  pl.BlockSpec((d, tf), lambda i, f: (0, f)),
                  pl.BlockSpec((tf, d), lambda i, f: (f, 0))],
        out_specs=pl.BlockSpec((tm, d), lambda i, f: (i, 0)),
        out_shape=jax.ShapeDtypeStruct((m, d), BF16),
        scratch_shapes=[pltpu.VMEM((tm, d), F32)],
        compiler_params=_params(("parallel", "arbitrary")), name="shared_expert",
    )(h, w_gate, w_up, w_down)


def _dispatch_plan(idx_t, wgt_t, n_exp, tm):
    k, n = idx_t.shape
    pairs = k * n
    flat_e = idx_t.reshape(pairs)
    flat_w = wgt_t.reshape(pairs)
    order = jnp.argsort(flat_e, stable=True).astype(jnp.int32)
    counts = jnp.sum((flat_e[:, None] == jnp.arange(n_exp, dtype=jnp.int32)[None, :]).astype(jnp.int32), axis=0)
    start = jnp.cumsum(counts) - counts
    padded = (counts + tm - 1) // tm * tm
    pend = jnp.cumsum(padded)
    pstart = pend - padded
    n_tiles = pairs // tm + n_exp
    tile_start = jnp.arange(n_tiles, dtype=jnp.int32) * tm
    tile_valid = tile_start < pend[-1]
    tile_e = jnp.minimum(jnp.searchsorted(pend, tile_start, side="right"), n_exp - 1).astype(jnp.int32)
    last_e = jnp.max(jnp.where(tile_valid, tile_e, 0))
    tile_e = jnp.where(tile_valid, tile_e, last_e)
    row = jnp.arange(n_tiles * tm, dtype=jnp.int32)
    row_e = jnp.repeat(tile_e, tm)
    local = row - pstart[row_e]
    row_valid = (local < counts[row_e]) & jnp.repeat(tile_valid, tm)
    src = order[jnp.clip(start[row_e] + local, 0, pairs - 1)]
    row_token = jnp.where(row_valid, src % n, 0).astype(jnp.int32)
    row_w = jnp.where(row_valid, flat_w[src], 0.0)
    inv = jnp.zeros((pairs,), jnp.int32).at[order].set(jnp.arange(pairs, dtype=jnp.int32))
    pos = (inv - start[flat_e] + pstart[flat_e]).astype(jnp.int32)
    return tile_e, tile_valid.astype(jnp.int32), row_token, row_w, pos


def _expert_kernel(te_ref, tv_ref, rt_ref, hp_hbm, roww_ref, wg_ref, wu_ref, wd_ref, o_ref,
                   xbuf, sems, wgb, wub, wdb):
    i = pl.program_id(0)
    n = pl.num_programs(0)
    tm = xbuf.shape[1]
    slot = i % 2

    def gather_rows(tile, dst_slot):
        def body(r, _):
            tok = rt_ref[tile * tm + r]
            pltpu.make_async_copy(hp_hbm.at[pl.ds(tok, 1)], xbuf.at[dst_slot, pl.ds(r, 1)],
                                  sems.at[dst_slot]).start()
            return 0
        lax.fori_loop(0, tm, body, 0)

    @pl.when((i == 0) & (tv_ref[0] == 1))
    def _():
        gather_rows(0, 0)

    nxt = jnp.minimum(i + 1, n - 1)

    @pl.when((i + 1 < n) & (tv_ref[nxt] == 1))
    def _():
        gather_rows(i + 1, 1 - slot)

    prev = jnp.maximum(i - 1, 0)

    @pl.when((i == 0) | (te_ref[i] != te_ref[prev]))
    def _():
        wgb[...] = wg_ref[...].astype(BF16)
        wub[...] = wu_ref[...].astype(BF16)
        wdb[...] = wd_ref[...].astype(BF16)

    @pl.when(tv_ref[i] == 1)
    def _():
        pltpu.make_async_copy(hp_hbm.at[pl.ds(0, tm)], xbuf.at[slot], sems.at[slot]).wait()
        lo, hi = _unpack_halves(xbuf[slot])
        x = jnp.concatenate([lo.astype(BF16), hi.astype(BF16)], axis=1)
        hid = _silu(_dot(x, wgb[...])) * _dot(x, wub[...]) * roww_ref[...]
        o_ref[...] = _pack_halves(_dot(hid.astype(BF16), wdb[...]))

    @pl.when(tv_ref[i] == 0)
    def _():
        o_ref[...] = jnp.zeros_like(o_ref)


def routed_experts(hp, plan, w_gate, w_up, w_down):
    tile_e, tile_valid, row_token, row_w, _ = plan
    tm = EXPERT_TILE
    n_tiles = tile_e.shape[0]
    n_exp, d, f = w_gate.shape
    grid_spec = pltpu.PrefetchScalarGridSpec(
        num_scalar_prefetch=3, grid=(n_tiles,),
        in_specs=[pl.BlockSpec(memory_space=pl.ANY),
                  pl.BlockSpec((tm, 1), lambda i, te, tv, rt: (i, 0)),
                  pl.BlockSpec((None, d, f), lambda i, te, tv, rt: (te[i], 0, 0)),
                  pl.BlockSpec((None, d, f), lambda i, te, tv, rt: (te[i], 0, 0)),
                  pl.BlockSpec((None, f, d), lambda i, te, tv, rt: (te[i], 0, 0))],
        out_specs=pl.BlockSpec((tm, d // 2), lambda i, te, tv, rt: (i, 0)),
        scratch_shapes=[pltpu.VMEM((2, tm, d // 2), jnp.uint32), pltpu.SemaphoreType.DMA((2,)),
                        pltpu.VMEM((d, f), BF16), pltpu.VMEM((d, f), BF16), pltpu.VMEM((f, d), BF16)])
    return pl.pallas_call(
        _expert_kernel, grid_spec=grid_spec,
        out_shape=jax.ShapeDtypeStruct((n_tiles * tm, d // 2), jnp.uint32),
        compiler_params=_params(("arbitrary",)), name="routed_experts",
    )(tile_e, tile_valid, row_token, hp, row_w.reshape(-1, 1), w_gate, w_up, w_down)


def _combine_kernel(pos_ref, ys_hbm, sh_ref, o_ref, buf, sems, *, n_tok):
    i = pl.program_id(0)
    n = pl.num_programs(0)
    tc = buf.shape[2]
    slot = i % 2

    def gather_rows(tile, dst_slot):
        def body(t, _):
            for k in range(MOE_TOPK):
                p = pos_ref[k * n_tok + tile * tc + t]
                pltpu.make_async_copy(ys_hbm.at[pl.ds(p, 1)], buf.at[dst_slot, k, pl.ds(t, 1)],
                                      sems.at[dst_slot]).start()
            return 0
        lax.fori_loop(0, tc, body, 0)

    @pl.when(i == 0)
    def _():
        gather_rows(0, 0)

    @pl.when(i + 1 < n)
    def _():
        gather_rows(i + 1, 1 - slot)

    for k in range(MOE_TOPK):
        pltpu.make_async_copy(ys_hbm.at[pl.ds(0, tc)], buf.at[slot, k], sems.at[slot]).wait()
    half = sh_ref.shape[-1] // 2
    sh = sh_ref[...].astype(F32)
    lo_acc, hi_acc = sh[:, :half], sh[:, half:]
    for k in range(MOE_TOPK):
        lo, hi = _unpack_halves(buf[slot, k])
        lo_acc = lo_acc + lo
        hi_acc = hi_acc + hi
    o_ref[...] = jnp.concatenate([lo_acc, hi_acc], axis=1).astype(o_ref.dtype)


def combine(ys, pos, shared, n_tok):
    d = shared.shape[-1]
    tc = COMBINE_TILE
    grid_spec = pltpu.PrefetchScalarGridSpec(
        num_scalar_prefetch=1, grid=(n_tok // tc,),
        in_specs=[pl.BlockSpec(memory_space=pl.ANY),
                  pl.BlockSpec((tc, d), lambda i, p: (i, 0))],
        out_specs=pl.BlockSpec((tc, d), lambda i, p: (i, 0)),
        scratch_shapes=[pltpu.VMEM((2, MOE_TOPK, tc, d // 2), jnp.uint32), pltpu.SemaphoreType.DMA((2,))])
    return pl.pallas_call(
        functools.partial(_combine_kernel, n_tok=n_tok), grid_spec=grid_spec,
        out_shape=jax.ShapeDtypeStruct((n_tok, d), BF16),
        compiler_params=_params(("arbitrary",)), name="moe_combine",
    )(pos, ys, shared)


def moe_ffn(h, hp, idx_t, wgt_t, w_gate, w_up, w_down, sh_gate, sh_up, sh_down):
    n_tok = h.shape[0]
    plan = _dispatch_plan(idx_t, wgt_t, w_gate.shape[0], EXPERT_TILE)
    ys = routed_experts(hp, plan, w_gate, w_up, w_down)
    shared = shared_expert(h, sh_gate, sh_up, sh_down)
    return combine(ys, plan[4], shared, n_tok)


def kernel(x, c, ctx, c_ctx, ada_w, ada_b, norm_pre_mix, norm_post_mix, norm_pre_ffn, norm_post_ffn, ret_w_in, ret_w_out, ret_decay_fwd, ret_decay_bwd, diff_w_in, diff_w_out, diff_lam_q1, diff_lam_k1, diff_lam_q2, diff_lam_k2, diff_subln_w, moe_router_w, moe_router_b, moe_w_gate, moe_w_up, moe_w_down, moe_shared_gate, moe_shared_up, moe_shared_down):
    batch, t_len, d = x.shape
    c_len = ctx.shape[1]
    s_len = t_len + c_len
    depth = ada_w.shape[0]
    assert depth == 2 and batch + 1 <= 8
    assert t_len % ROW_TILE == 0 and c_len % ROW_TILE == 0 and t_len % GRID_W == 0
    ret_heads = ret_decay_fwd.shape[-1]
    diff_heads = d // (2 * DIFF_HEAD_DIM)
    lat_tiles, all_tiles = t_len // ROW_TILE, s_len // ROW_TILE

    cc = jnp.concatenate([c, c_ctx[None], jnp.zeros((8 - batch - 1, d), F32)], axis=0)
    mods = ada_modulation(cc, ada_w, ada_b)
    xs = jnp.concatenate([x, ctx], axis=1).reshape(batch * s_len, d)
    rope_ret = _rope_tables(t_len, d // ret_heads)
    rope_diff = _rope_tables(t_len, DIFF_HEAD_DIM)

    ident = lambda i: i
    uni_mod = lambda i: jnp.where(i % all_tiles < lat_tiles, i // all_tiles, batch)
    lat_mod = lambda i: i // lat_tiles
    lat_of_uni = lambda i: (i // lat_tiles) * all_tiles + i % lat_tiles

    (h,) = fused_norm(xs, ident, batch * all_tiles, uni_mod, mod_b=mods[0], w_pre=norm_pre_mix[0],
                      shift_idx=0, scale_idx=1)
    qkvg = matmul(h, ret_w_in[0])
    lg_f = jax.nn.log_sigmoid(ret_decay_fwd[0].astype(F32))
    lg_b = jax.nn.log_sigmoid(ret_decay_bwd[0].astype(F32))
    r = retention(qkvg, lg_f, lg_b, *rope_ret, batch=batch, t_len=t_len, c_len=c_len, heads=ret_heads)
    y = matmul(r, ret_w_out[0])
    xs, h, hp, idx_t, wgt_t = fused_norm(
        xs, ident, batch * all_tiles, uni_mod, y=y, mod_a=mods[0], w_post=norm_post_mix[0], gate_idx=2,
        mod_b=mods[0], w_pre=norm_pre_ffn[0], shift_idx=3, scale_idx=4,
        router_wt=moe_router_w[0].T, router_b=moe_router_b[0])
    f = moe_ffn(h, hp, idx_t, wgt_t, moe_w_gate[0], moe_w_up[0], moe_w_down[0],
                moe_shared_gate[0], moe_shared_up[0], moe_shared_down[0])
    xs, h = fused_norm(xs, ident, batch * all_tiles, uni_mod, y=f, mod_a=mods[0], w_post=norm_post_ffn[0],
                       gate_idx=5, mod_b=mods[1], w_pre=norm_pre_mix[1], shift_idx=0, scale_idx=1)

    qkv = matmul(h, diff_w_in[0])
    lam_vecs = jnp.stack([diff_lam_q1[0], diff_lam_k1[0], diff_lam_q2[0], diff_lam_k2[0]]).astype(F32)
    lambda_init = 0.8 - 0.6 * math.exp(-0.3 * 1)
    a = diff_attention(qkv, lam_vecs, diff_subln_w[0], *rope_diff, batch=batch, t_len=t_len, c_len=c_len,
                       heads=diff_heads, lambda_init=lambda_init)
    y = matmul(a, diff_w_out[0])
    xl, h, hp, idx_t, wgt_t = fused_norm(
        xs, lat_of_uni, batch * lat_tiles, lat_mod, y=y, mod_a=mods[1], w_post=norm_post_mix[1], gate_idx=2,
        mod_b=mods[1], w_pre=norm_pre_ffn[1], shift_idx=3, scale_idx=4,
        router_wt=moe_router_w[1].T, router_b=moe_router_b[1])
    f = moe_ffn(h, hp, idx_t, wgt_t, moe_w_gate[1], moe_w_up[1], moe_w_down[1],
                moe_shared_gate[1], moe_shared_up[1], moe_shared_down[1])
    (out,) = fused_norm(xl, ident, batch * lat_tiles, lat_mod, y=f, mod_a=mods[1], w_post=norm_post_ffn[1],
                        gate_idx=5)
    return out.reshape(batch, t_len, d)
```

```python
import functools
import math

import jax
import jax.numpy as jnp
from jax import lax
from jax.experimental import pallas as pl
from jax.experimental.pallas import tpu as pltpu

GRID_W = 64
N_ADA = 6
NORM_EPS = 1e-6
ROPE_BASE = 10000.0
RET_BLOCK = 256
DIFF_HEAD_DIM = 128
MOE_TOPK = 8
MOE_GROUPS = 8
MOE_TOPK_GROUPS = 4
ROUTED_SCALE = 2.5

LANES = 128
ROW_TILE = 256
EXPERT_TILE = 256
COMBINE_TILE = 64
DISPATCH_TILE = 256
VMEM_LIMIT = 56 * 1024 * 1024

F32 = jnp.float32
BF16 = jnp.bfloat16


def _pick(dim, candidates):
    for c in candidates:
        if dim % c == 0:
            return c
    raise ValueError(f"no tile in {candidates} divides {dim}")


def _params(sem, vmem=VMEM_LIMIT):
    return pltpu.CompilerParams(dimension_semantics=sem, vmem_limit_bytes=vmem)


def _dot(a, b):
    return jnp.dot(a, b, preferred_element_type=F32)


def _dot_nt(a, b):
    return lax.dot_general(a, b, (((1,), (1,)), ((), ())), preferred_element_type=F32)


def _dot_tn(a, b):
    return lax.dot_general(a, b, (((0,), (0,)), ((), ())), preferred_element_type=F32)


def _silu(x):
    return x * jax.nn.sigmoid(x)


def _pack_halves(y):
    w = y.shape[-1] // 2
    lo = lax.bitcast_convert_type(y[:, :w].astype(BF16).astype(F32), jnp.uint32)
    hi = lax.bitcast_convert_type(y[:, w:].astype(BF16).astype(F32), jnp.uint32)
    return (hi & jnp.uint32(0xFFFF0000)) | (lo >> 16)


def _unpack_halves(p):
    lo = lax.bitcast_convert_type(p << 16, F32)
    hi = lax.bitcast_convert_type(p & jnp.uint32(0xFFFF0000), F32)
    return lo, hi


def _ada_kernel(c_ref, w_ref, b_ref, o_ref):
    a = _silu(c_ref[...]).astype(BF16)
    o_ref[...] = _dot(a, w_ref[...].astype(BF16)) + b_ref[...]


def ada_modulation(cc, ada_w, ada_b):
    depth, d, n = ada_w.shape
    tn = _pick(n, (512, 256, 128))
    out = pl.pallas_call(
        _ada_kernel,
        grid=(depth, n // tn),
        in_specs=[pl.BlockSpec((8, d), lambda l, j: (0, 0)),
                  pl.BlockSpec((None, d, tn), lambda l, j: (l, 0, j)),
                  pl.BlockSpec((None, 1, tn), lambda l, j: (l, 0, j))],
        out_specs=pl.BlockSpec((None, 8, tn), lambda l, j: (l, 0, j)),
        out_shape=jax.ShapeDtypeStruct((depth, 8, n), F32),
        compiler_params=_params(("parallel", "parallel")),
        name="ada_modulation",
    )(cc, ada_w, ada_b.reshape(depth, 1, n))
    return out.reshape(depth, 8, N_ADA, d)


def _rms(x):
    return x * lax.rsqrt(jnp.mean(x * x, axis=-1, keepdims=True) + NORM_EPS)


def _route(h, rw_ref, rb_ref, idx_ref, wgt_ref, rank_ref, cnt_ref):
    n_exp = rw_ref.shape[0]
    tm = h.shape[0]
    per_group = n_exp // MOE_GROUPS
    w = rw_ref[...]
    w_hi = w.astype(BF16)
    w_lo = (w - w_hi.astype(F32)).astype(BF16)
    h_hi = h.astype(BF16)
    h_lo = (h - h_hi.astype(F32)).astype(BF16)
    logits = _dot_nt(w_hi, h_hi) + (_dot_nt(w_hi, h_lo) + _dot_nt(w_lo, h_hi))
    scores = jax.nn.sigmoid(logits)
    biased = scores + rb_ref[...]
    neg = jnp.float32(-jnp.inf)
    sub = lax.broadcasted_iota(jnp.int32, (per_group, tm), 0)
    giota = lax.broadcasted_iota(jnp.int32, (MOE_GROUPS, tm), 0)
    gs = jnp.zeros((MOE_GROUPS, tm), F32)
    for g in range(MOE_GROUPS):
        blk = biased[g * per_group:(g + 1) * per_group]
        m1 = jnp.max(blk, axis=0, keepdims=True)
        i1 = jnp.min(jnp.where(blk == m1, sub, per_group), axis=0, keepdims=True)
        m2 = jnp.max(jnp.where(sub == i1, neg, blk), axis=0, keepdims=True)
        gs = jnp.where(giota == g, m1 + m2, gs)
    rank = jnp.zeros((MOE_GROUPS, tm), jnp.int32)
    for j in range(MOE_GROUPS):
        gj = gs[j:j + 1]
        beats = (gj > gs) | ((gj == gs) & (giota > j))
        rank = rank + beats.astype(jnp.int32)
    keep = (rank < MOE_TOPK_GROUPS).astype(F32)
    keep_e = jnp.concatenate(
        [jnp.broadcast_to(keep[g:g + 1], (per_group, tm)) for g in range(MOE_GROUPS)], axis=0)
    masked = jnp.where(keep_e > 0.5, biased, neg)
    eiota = lax.broadcasted_iota(jnp.int32, (n_exp, tm), 0)
    sel_w, hits = [], []
    for k in range(MOE_TOPK):
        m = jnp.max(masked, axis=0, keepdims=True)
        idx = jnp.min(jnp.where(masked == m, eiota, n_exp), axis=0, keepdims=True)
        hit = eiota == idx
        hits.append(hit)
        sel_w.append(jnp.sum(jnp.where(hit, scores, 0.0), axis=0, keepdims=True))
        masked = jnp.where(hit, neg, masked)
        idx_ref[k:k + 1, :] = idx
    total = sel_w[0]
    for k in range(1, MOE_TOPK):
        total = total + sel_w[k]
    for k in range(MOE_TOPK):
        wgt_ref[k:k + 1, :] = sel_w[k] / total * ROUTED_SCALE
    chosen = jnp.zeros((n_exp, tm), F32)
    for k in range(MOE_TOPK):
        chosen = jnp.where(hits[k], 1.0, chosen)
    before = (lax.broadcasted_iota(jnp.int32, (tm, tm), 0)
              < lax.broadcasted_iota(jnp.int32, (tm, tm), 1)).astype(BF16)
    prefix = _dot(chosen.astype(BF16), before)
    for k in range(MOE_TOPK):
        rank_ref[k:k + 1, :] = jnp.sum(jnp.where(hits[k], prefix, 0.0), axis=0,
                                       keepdims=True).astype(jnp.int32)
    cnt_ref[...] = jnp.broadcast_to(jnp.sum(chosen, axis=1, keepdims=True),
                                    cnt_ref.shape).astype(jnp.int32)


def _fused_norm_kernel(*refs, has_resid, has_prenorm, has_router, gate_idx, shift_idx, scale_idx):
    refs = list(refs)
    x_ref = refs.pop(0)
    if has_resid:
        y_ref, mod_a_ref, wpost_ref = refs.pop(0), refs.pop(0), refs.pop(0)
    if has_prenorm:
        mod_b_ref, wpre_ref = refs.pop(0), refs.pop(0)
    if has_router:
        rw_ref, rb_ref = refs.pop(0), refs.pop(0)
    x = x_ref[...]
    if has_resid:
        xo_ref = refs.pop(0)
        y = y_ref[...].astype(F32)
        x = x + mod_a_ref[gate_idx:gate_idx + 1, :] * (_rms(y) * wpost_ref[...])
        xo_ref[...] = x
    if has_prenorm:
        h_ref = refs.pop(0)
        h = (_rms(x) * wpre_ref[...]) * (1.0 + mod_b_ref[scale_idx:scale_idx + 1, :]) \
            + mod_b_ref[shift_idx:shift_idx + 1, :]
        h_ref[...] = h.astype(BF16)
        if has_router:
            hp_ref, idx_ref, wgt_ref, rank_ref, cnt_ref = (refs.pop(0) for _ in range(5))
            hp_ref[...] = _pack_halves(h)
            _route(h, rw_ref, rb_ref, idx_ref, wgt_ref, rank_ref, cnt_ref)


def fused_norm(x, x_tile_map, n_out_tiles, mod_row_map, *, y=None, mod_a=None, w_post=None, gate_idx=0,
               mod_b=None, w_pre=None, shift_idx=0, scale_idx=0, router_wt=None, router_b=None):
    d = x.shape[-1]
    tm = ROW_TILE
    has_resid, has_prenorm, has_router = y is not None, mod_b is not None, router_wt is not None
    row = lambda i: (i, 0)
    const = lambda i: (0, 0)
    mod_spec = pl.BlockSpec((None, N_ADA, d), lambda i: (mod_row_map(i), 0, 0))
    vec_spec = pl.BlockSpec((1, d), const)
    args, in_specs = [x], [pl.BlockSpec((tm, d), lambda i: (x_tile_map(i), 0))]
    out_shape, out_specs = [], []
    n_rows = n_out_tiles * tm
    if has_resid:
        args += [y, mod_a, w_post.reshape(1, d)]
        in_specs += [pl.BlockSpec((tm, d), row), mod_spec, vec_spec]
        out_shape.append(jax.ShapeDtypeStruct((n_rows, d), F32))
        out_specs.append(pl.BlockSpec((tm, d), row))
    if has_prenorm:
        args += [mod_b, w_pre.reshape(1, d)]
        in_specs += [mod_spec, vec_spec]
        out_shape.append(jax.ShapeDtypeStruct((n_rows, d), BF16))
        out_specs.append(pl.BlockSpec((tm, d), row))
    if has_router:
        n_exp = router_wt.shape[0]
        args += [router_wt, router_b.reshape(n_exp, 1)]
        in_specs += [pl.BlockSpec((n_exp, d), const), pl.BlockSpec((n_exp, 1), const)]
        out_shape += [jax.ShapeDtypeStruct((n_rows, d // 2), jnp.uint32),
                      jax.ShapeDtypeStruct((MOE_TOPK, n_rows), jnp.int32),
                      jax.ShapeDtypeStruct((MOE_TOPK, n_rows), F32),
                      jax.ShapeDtypeStruct((MOE_TOPK, n_rows), jnp.int32),
                      jax.ShapeDtypeStruct((n_out_tiles, n_exp, LANES), jnp.int32)]
        out_specs += [pl.BlockSpec((tm, d // 2), row),
                      pl.BlockSpec((MOE_TOPK, tm), lambda i: (0, i)),
                      pl.BlockSpec((MOE_TOPK, tm), lambda i: (0, i)),
                      pl.BlockSpec((MOE_TOPK, tm), lambda i: (0, i)),
                      pl.BlockSpec((None, n_exp, LANES), lambda i: (i, 0, 0))]
    kern = functools.partial(_fused_norm_kernel, has_resid=has_resid, has_prenorm=has_prenorm,
                             has_router=has_router, gate_idx=gate_idx, shift_idx=shift_idx,
                             scale_idx=scale_idx)
    return pl.pallas_call(
        kern, grid=(n_out_tiles,), in_specs=in_specs, out_specs=out_specs, out_shape=out_shape,
        compiler_params=_params(("parallel",)), name="fused_norm",
    )(*args)


def _mm_kernel(a_ref, w_ref, o_ref):
    o_ref[...] = _dot(a_ref[...], w_ref[...].astype(BF16)).astype(o_ref.dtype)


def _mm_acc_kernel(a_ref, w_ref, o_ref, acc_ref):
    k = pl.program_id(2)

    @pl.when(k == 0)
    def _():
        acc_ref[...] = jnp.zeros_like(acc_ref)

    acc_ref[...] += _dot(a_ref[...], w_ref[...].astype(BF16))

    @pl.when(k == pl.num_programs(2) - 1)
    def _():
        o_ref[...] = acc_ref[...].astype(o_ref.dtype)


def matmul(a, w, layer, out_dtype=BF16):
    m, k = a.shape
    n = w.shape[2]
    tm = _pick(m, (1024, 768, 512, 256))
    tn = _pick(n, (512, 256, 128))
    tk = _pick(k, (4096, 2048, 1024, 512))
    if tk == k:
        return pl.pallas_call(
            _mm_kernel, grid=(m // tm, n // tn),
            in_specs=[pl.BlockSpec((tm, k), lambda i, j: (i, 0)),
                      pl.BlockSpec((None, k, tn), lambda i, j: (layer, 0, j))],
            out_specs=pl.BlockSpec((tm, tn), lambda i, j: (i, j)),
            out_shape=jax.ShapeDtypeStruct((m, n), out_dtype),
            compiler_params=_params(("parallel", "parallel")), name="matmul",
        )(a, w)
    return pl.pallas_call(
        _mm_acc_kernel, grid=(m // tm, n // tn, k // tk),
        in_specs=[pl.BlockSpec((tm, tk), lambda i, j, l: (i, l)),
                  pl.BlockSpec((None, tk, tn), lambda i, j, l: (layer, l, j))],
        out_specs=pl.BlockSpec((tm, tn), lambda i, j, l: (i, j)),
        out_shape=jax.ShapeDtypeStruct((m, n), out_dtype),
        scratch_shapes=[pltpu.VMEM((tm, tn), F32)],
        compiler_params=_params(("parallel", "parallel", "arbitrary")), name="matmul_acc",
    )(a, w)


def _rope_tables(t_len, head_dim):
    rows = t_len // GRID_W
    n_freq = head_dim // 4
    row, col = jnp.meshgrid(jnp.arange(rows, dtype=F32), jnp.arange(GRID_W, dtype=F32), indexing="ij")
    inv_freq = ROPE_BASE ** (-jnp.arange(n_freq, dtype=F32) / n_freq)
    ang_r = row.reshape(-1, 1) * inv_freq
    ang_c = col.reshape(-1, 1) * inv_freq
    cr, sr, cc, sc = jnp.cos(ang_r), jnp.sin(ang_r), jnp.cos(ang_c), jnp.sin(ang_c)
    return (jnp.concatenate([cr, cr, cc, cc], axis=-1), jnp.concatenate([-sr, sr, -sc, sc], axis=-1))


def _swap_quarters(x, quarter):
    lane = lax.broadcasted_iota(jnp.int32, x.shape, 1)
    first = (lane % (2 * quarter)) < quarter
    return jnp.where(first, pltpu.roll(x, LANES - quarter, axis=1), pltpu.roll(x, quarter, axis=1))


def _retention_kernel(lgf_ref, lgb_ref, q_ref, k_ref, v_ref, g_ref, cos_ref, sin_ref, o_ref,
                      qr, kr, oacc_f, oacc_b, state_f, state_b, *, t_len, c_len):
    head = pl.program_id(1)
    L = RET_BLOCK
    s_len = t_len + c_len
    dk = q_ref.shape[-1]
    k_scale = dk ** -0.5
    rt = ROW_TILE

    def rope(x, rows):
        sw = jnp.concatenate([pltpu.roll(x[:, :LANES], LANES // 2, axis=1),
                              pltpu.roll(x[:, LANES:], LANES // 2, axis=1)], axis=1)
        return x * cos_ref[rows, :] + sw * sin_ref[rows, :]

    def rope_body(i, _):
        rows = pl.ds(pl.multiple_of(i * rt, rt), rt)
        qr[rows, :] = rope(q_ref[rows, :].astype(F32), rows).astype(BF16)
        kr[rows, :] = (rope(k_ref[rows, :].astype(F32), rows) * k_scale).astype(BF16)
        return 0

    lax.fori_loop(0, t_len // rt, rope_body, 0)
    ctx_rows = pl.ds(t_len, c_len)
    qr[ctx_rows, :] = q_ref[ctx_rows, :]
    kr[ctx_rows, :] = (k_ref[ctx_rows, :].astype(F32) * k_scale).astype(BF16)

    ii = lax.broadcasted_iota(jnp.int32, (L, L), 0)
    jj = lax.broadcasted_iota(jnp.int32, (L, L), 1)
    rel = (ii - jj).astype(F32)
    idx = lax.broadcasted_iota(jnp.int32, (L, 1), 0).astype(F32)

    lg_f, lg_b = lgf_ref[head], lgb_ref[head]
    fwd = (jnp.where(rel >= 0, jnp.exp(lg_f * jnp.maximum(rel, 0.0)), 0.0),
           jnp.exp(lg_f * (idx + 1.0)),
           jnp.exp(lg_f * (L - 1.0 - idx)),
           jnp.exp(lg_f * L), state_f, oacc_f)
    bwd = (jnp.where(rel <= 0, jnp.exp(lg_b * jnp.maximum(-rel, 0.0)), 0.0),
           jnp.exp(lg_b * (L - idx)),
           jnp.exp(lg_b * idx),
           jnp.exp(lg_b * L), state_b, oacc_b)

    def chunk(row0, direction):
        dmat, q_decay, k_decay, chunk_decay, state, oacc = direction
        rows = pl.ds(pl.multiple_of(row0, L), L)
        qb, kb, vb = qr[rows, :], kr[rows, :], v_ref[rows, :]
        scores = _dot_nt(qb, kb) * dmat
        inner = _dot(scores.astype(BF16), vb)
        st = state[...]
        cross = _dot(qb, st.astype(BF16)) * q_decay
        oacc[rows, :] = inner + cross
        kd = (kb.astype(F32) * k_decay).astype(BF16)
        state[...] = st * chunk_decay + _dot_tn(kd, vb)

    def scan(r0, n_chunks):
        def body(ci, _):
            chunk(r0 + ci * L, fwd)
            chunk(r0 + (n_chunks - 1 - ci) * L, bwd)
            return 0
        lax.fori_loop(0, n_chunks, body, 0)

    state_f[...] = jnp.zeros_like(state_f)
    state_b[...] = jnp.zeros_like(state_b)
    scan(t_len, c_len // L)
    scan(0, t_len // L)

    def out_body(i, _):
        rows = pl.ds(pl.multiple_of(i * rt, rt), rt)
        o = _rms(oacc_f[rows, :] + oacc_b[rows, :])
        o_ref[rows, :] = (_silu(g_ref[rows, :].astype(F32)) * o).astype(BF16)
        return 0

    lax.fori_loop(0, s_len // rt, out_body, 0)


def retention(qkvg, lg_f, lg_b, cos, sin, *, batch, t_len, c_len, heads):
    s_len = t_len + c_len
    dk = cos.shape[-1]
    dv = 2 * dk
    assert dk == 2 * LANES
    kern = functools.partial(_retention_kernel, t_len=t_len, c_len=c_len)
    grid_spec = pltpu.PrefetchScalarGridSpec(
        num_scalar_prefetch=2, grid=(batch, heads),
        in_specs=[pl.BlockSpec((s_len, dk), lambda b, h, *_: (b, h)),
                  pl.BlockSpec((s_len, dk), lambda b, h, *_: (b, heads + h)),
                  pl.BlockSpec((s_len, dv), lambda b, h, *_: (b, heads + h)),
                  pl.BlockSpec((s_len, dv), lambda b, h, *_: (b, 2 * heads + h)),
                  pl.BlockSpec((t_len, dk), lambda b, h, *_: (0, 0)),
                  pl.BlockSpec((t_len, dk), lambda b, h, *_: (0, 0))],
        out_specs=pl.BlockSpec((s_len, dv), lambda b, h, *_: (b, h)),
        scratch_shapes=[pltpu.VMEM((s_len, dk), BF16), pltpu.VMEM((s_len, dk), BF16),
                        pltpu.VMEM((s_len, dv), F32), pltpu.VMEM((s_len, dv), F32),
                        pltpu.VMEM((dk, dv), F32), pltpu.VMEM((dk, dv), F32)])
    return pl.pallas_call(
        kern, grid_spec=grid_spec,
        out_shape=jax.ShapeDtypeStruct((batch * s_len, heads * dv), BF16),
        compiler_params=_params(("parallel", "parallel")), name="retention",
    )(lg_f, lg_b, qkvg, qkvg, qkvg, qkvg, cos, sin)


def _diff_attn_kernel(lam_ref, q_ref, k_ref, v_ref, cosq_ref, sinq_ref, cosk_ref, sink_ref, subln_ref,
                      o_ref, kr, *, t_len, c_len, lambda_init):
    hd = DIFF_HEAD_DIM
    quarter = hd // 4
    rt = ROW_TILE

    def rope(x, cos, sin):
        return x * cos + _swap_quarters(x, quarter) * sin

    @pl.when(pl.program_id(2) == 0)
    def _():
        def body(i, _):
            rows = pl.ds(pl.multiple_of(i * rt, rt), rt)
            for c in range(2):
                cols = slice(c * hd, (c + 1) * hd)
                kr[rows, cols] = rope(k_ref[rows, cols].astype(F32), cosk_ref[rows, :],
                                      sink_ref[rows, :]).astype(BF16)
            return 0
        lax.fori_loop(0, t_len // rt, body, 0)
        ctx_rows = pl.ds(t_len, c_len)
        kr[ctx_rows, :] = k_ref[ctx_rows, :]

    lam_v = lam_ref[...]
    lam = (jnp.exp(jnp.sum(lam_v[0:1] * lam_v[1:2], axis=-1, keepdims=True))
           - jnp.exp(jnp.sum(lam_v[2:3] * lam_v[3:4], axis=-1, keepdims=True)) + lambda_init)
    v = v_ref[...]
    outs = []
    q_scale = (hd ** -0.5) * math.log2(math.e)
    for c in range(2):
        cols = slice(c * hd, (c + 1) * hd)
        qc = (rope(q_ref[:, cols].astype(F32), cosq_ref[...], sinq_ref[...]) * q_scale).astype(BF16)
        s = _dot_nt(qc, kr[:, cols])
        e = jnp.exp2(s - jnp.max(s, axis=-1, keepdims=True))
        denom = jnp.sum(e, axis=-1, keepdims=True)
        outs.append(_dot(e.astype(BF16), v) / denom)
    o = outs[0] - lam * outs[1]
    o_ref[...] = ((_rms(o) * subln_ref[...]) * (1.0 - lambda_init)).astype(BF16)


def diff_attention(qkv, lam_vecs, subln_w, cos, sin, *, batch, t_len, c_len, heads, lambda_init):
    s_len = t_len + c_len
    hd = DIFF_HEAD_DIM
    tq = ROW_TILE
    q_tiles, s_tiles = t_len // tq, s_len // tq
    kern = functools.partial(_diff_attn_kernel, t_len=t_len, c_len=c_len, lambda_init=lambda_init)
    return pl.pallas_call(
        kern, grid=(batch, heads, q_tiles),
        in_specs=[pl.BlockSpec((4, hd), lambda b, h, i: (0, 0)),
                  pl.BlockSpec((tq, 2 * hd), lambda b, h, i: (b * s_tiles + i, h)),
                  pl.BlockSpec((s_len, 2 * hd), lambda b, h, i: (b, heads + h)),
                  pl.BlockSpec((s_len, 2 * hd), lambda b, h, i: (b, 2 * heads + h)),
                  pl.BlockSpec((tq, hd), lambda b, h, i: (i, 0)),
                  pl.BlockSpec((tq, hd), lambda b, h, i: (i, 0)),
                  pl.BlockSpec((t_len, hd), lambda b, h, i: (0, 0)),
                  pl.BlockSpec((t_len, hd), lambda b, h, i: (0, 0)),
                  pl.BlockSpec((1, 2 * hd), lambda b, h, i: (0, 0))],
        out_specs=pl.BlockSpec((tq, 2 * hd), lambda b, h, i: (b * q_tiles + i, h)),
        out_shape=jax.ShapeDtypeStruct((batch * t_len, heads * 2 * hd), BF16),
        scratch_shapes=[pltpu.VMEM((s_len, 2 * hd), BF16)],
        compiler_params=_params(("parallel", "parallel", "arbitrary")), name="diff_attention",
    )(lam_vecs, qkv, qkv, qkv, cos, sin, cos, sin, subln_w.reshape(1, 2 * hd))


def _shared_kernel(x_ref, wg_ref, wu_ref, wd_ref, o_ref, acc_ref):
    f = pl.program_id(1)

    @pl.when(f == 0)
    def _():
        acc_ref[...] = jnp.zeros_like(acc_ref)

    x = x_ref[...]
    hid = _silu(_dot(x, wg_ref[...])) * _dot(x, wu_ref[...])
    acc_ref[...] += _dot(hid.astype(BF16), wd_ref[...])

    @pl.when(f == pl.num_programs(1) - 1)
    def _():
        o_ref[...] = acc_ref[...].astype(o_ref.dtype)


def shared_expert(h, w_gate, w_up, w_down):
    m, d = h.shape
    f_dim = w_gate.shape[1]
    tm = _pick(m, (512, 256))
    tf = _pick(f_dim, (256, 128))
    return pl.pallas_call(
        _shared_kernel, grid=(m // tm, f_dim // tf),
        in_specs=[pl.BlockSpec((tm, d), lambda i, f: (i, 0)),
                  pl.BlockSpec((d, tf), lambda i, f: (0, f)),
                  pl.BlockSpec((d, tf), lambda i, f: (0, f)),
                  pl.BlockSpec((tf, d), lambda i, f: (f, 0))],
        out_specs=pl.BlockSpec((tm, d), lambda i, f: (i, 0)),
        out_shape=jax.ShapeDtypeStruct((m, d), BF16),
        scratch_shapes=[pltpu.VMEM((tm, d), F32)],
        compiler_params=_params(("parallel", "arbitrary")), name="shared_expert",
    )(h, w_gate, w_up, w_down)


def _dispatch_plan(cnt, pairs, tm):
    n_exp = cnt.shape[1]
    total = jnp.sum(cnt, axis=0)
    padded = (total + tm - 1) // tm * tm
    pend = jnp.cumsum(padded)
    base = (pend - padded)[None, :] + jnp.cumsum(cnt, axis=0) - cnt
    n_tiles = pairs // tm + n_exp
    tile_start = jnp.arange(n_tiles, dtype=jnp.int32) * tm
    valid = tile_start < pend[-1]
    tile_e = jnp.minimum(jnp.sum((tile_start[:, None] >= pend[None, :]).astype(jnp.int32), axis=1), n_exp - 1)
    tile_e = jnp.where(valid, tile_e, jnp.max(jnp.where(valid, tile_e, 0)))
    next_e = jnp.concatenate([tile_e[1:], jnp.full((1,), -1, jnp.int32)])
    next_valid = jnp.concatenate([valid[1:], jnp.zeros((1,), bool)])
    zero_fill = (~valid) | (tile_e != next_e) | (~next_valid)
    return (tile_e.astype(jnp.int32), valid.astype(jnp.int32), zero_fill.astype(jnp.int32),
            base.astype(jnp.int32))


def _positions_kernel(idx_ref, rank_ref, base_ref, pos_ref):
    n_exp = base_ref.shape[0]
    tm = idx_ref.shape[1]
    eiota = lax.broadcasted_iota(jnp.int32, (n_exp, tm), 0)
    base = jnp.broadcast_to(base_ref[...].astype(F32), (n_exp, tm))
    for k in range(MOE_TOPK):
        hit = eiota == idx_ref[k:k + 1, :]
        first = jnp.sum(jnp.where(hit, base, 0.0), axis=0, keepdims=True)
        pos_ref[k:k + 1, :] = first.astype(jnp.int32) + rank_ref[k:k + 1, :]


def pair_positions(idx_t, rank_t, base):
    k, n = idx_t.shape
    tiles, n_exp = base.shape
    tm = n // tiles
    spec = pl.BlockSpec((k, tm), lambda i: (0, i))
    return pl.pallas_call(
        _positions_kernel, grid=(tiles,),
        in_specs=[spec, spec, pl.BlockSpec((None, n_exp, 1), lambda i: (i, 0, 0))],
        out_specs=spec, out_shape=jax.ShapeDtypeStruct((k, n), jnp.int32),
        compiler_params=_params(("parallel",)), name="pair_positions",
    )(idx_t, rank_t, base.reshape(tiles, n_exp, 1))


def _dispatch_kernel(pos_ref, zf_ref, hp_hbm, xs_hbm, zeros, sem_z, sem_s, *, n_tok, n_tiles):
    i = pl.program_id(0)
    td = DISPATCH_TILE
    tm = zeros.shape[0]

    def zero_copy(j):
        return pltpu.make_async_copy(zeros, xs_hbm.at[pl.ds(pl.multiple_of(j * tm, tm), tm)], sem_z)

    @pl.when(i == 0)
    def _():
        zeros[...] = jnp.zeros_like(zeros)

        def start(j, _):
            @pl.when(zf_ref[j] == 1)
            def _():
                zero_copy(j).start()
            return 0

        def wait(j, _):
            @pl.when(zf_ref[j] == 1)
            def _():
                zero_copy(j).wait()
            return 0

        lax.fori_loop(0, n_tiles, start, 0)
        lax.fori_loop(0, n_tiles, wait, 0)

    def body(t, _):
        tok = i * td + t
        for k in range(MOE_TOPK):
            row = pos_ref[k * n_tok + tok]
            pltpu.make_async_copy(hp_hbm.at[pl.ds(tok, 1)], xs_hbm.at[pl.ds(row, 1)], sem_s).start()
        return 0

    lax.fori_loop(0, td, body, 0, unroll=2)
    rows = MOE_TOPK * td
    pltpu.make_async_copy(hp_hbm.at[pl.ds(0, rows)], xs_hbm.at[pl.ds(0, rows)], sem_s).wait()


def dispatch(hp, pos, zero_fill, n_tiles):
    n_tok, half = hp.shape
    tm = EXPERT_TILE
    grid_spec = pltpu.PrefetchScalarGridSpec(
        num_scalar_prefetch=2, grid=(n_tok // DISPATCH_TILE,),
        in_specs=[pl.BlockSpec(memory_space=pl.ANY)],
        out_specs=pl.BlockSpec(memory_space=pl.ANY),
        scratch_shapes=[pltpu.VMEM((tm, half), jnp.uint32), pltpu.SemaphoreType.DMA(()),
                        pltpu.SemaphoreType.DMA(())])
    return pl.pallas_call(
        functools.partial(_dispatch_kernel, n_tok=n_tok, n_tiles=n_tiles), grid_spec=grid_spec,
        out_shape=jax.ShapeDtypeStruct((n_tiles * tm, half), jnp.uint32),
        compiler_params=_params(("arbitrary",)), name="moe_dispatch",
    )(pos, zero_fill, hp)


def _expert_kernel(te_ref, tv_ref, x_ref, wg_ref, wu_ref, wd_ref, o_ref, wgb, wub, wdb):
    i = pl.program_id(0)
    prev = jnp.maximum(i - 1, 0)

    @pl.when((i == 0) | (te_ref[i] != te_ref[prev]))
    def _():
        wgb[...] = wg_ref[...].astype(BF16)
        wub[...] = wu_ref[...].astype(BF16)
        wdb[...] = wd_ref[...].astype(BF16)

    @pl.when(tv_ref[i] == 1)
    def _():
        lo, hi = _unpack_halves(x_ref[...])
        x = jnp.concatenate([lo.astype(BF16), hi.astype(BF16)], axis=1)
        hid = _silu(_dot(x, wgb[...])) * _dot(x, wub[...])
        o_ref[...] = _pack_halves(_dot(hid.astype(BF16), wdb[...]))

    @pl.when(tv_ref[i] == 0)
    def _():
        o_ref[...] = jnp.zeros_like(o_ref)


def routed_experts(xs, tile_e, tile_valid, w_gate, w_up, w_down, layer):
    tm = EXPERT_TILE
    n_tiles = tile_e.shape[0]
    _, n_exp, d, f = w_gate.shape
    wmap = lambda i, te, tv: (layer, te[i], 0, 0)
    grid_spec = pltpu.PrefetchScalarGridSpec(
        num_scalar_prefetch=2, grid=(n_tiles,),
        in_specs=[pl.BlockSpec((tm, d // 2), lambda i, te, tv: (i, 0)),
                  pl.BlockSpec((None, None, d, f), wmap),
                  pl.BlockSpec((None, None, d, f), wmap),
                  pl.BlockSpec((None, None, f, d), wmap)],
        out_specs=pl.BlockSpec((tm, d // 2), lambda i, te, tv: (i, 0)),
        scratch_shapes=[pltpu.VMEM((d, f), BF16), pltpu.VMEM((d, f), BF16), pltpu.VMEM((f, d), BF16)])
    return pl.pallas_call(
        _expert_kernel, grid_spec=grid_spec,
        out_shape=jax.ShapeDtypeStruct((n_tiles * tm, d // 2), jnp.uint32),
        compiler_params=_params(("arbitrary",)), name="routed_experts",
    )(tile_e, tile_valid, xs, w_gate, w_up, w_down)


def _combine_kernel(pos_ref, ys_hbm, sh_ref, w_ref, o_ref, buf, sems, *, n_tok):
    i = pl.program_id(0)
    n = pl.num_programs(0)
    tc = buf.shape[2]
    slot = i % 2

    def gather_rows(tile, dst_slot):
        def body(t, _):
            for k in range(MOE_TOPK):
                p = pos_ref[k * n_tok + tile * tc + t]
                pltpu.make_async_copy(ys_hbm.at[pl.ds(p, 1)], buf.at[dst_slot, k, pl.ds(t, 1)],
                                      sems.at[dst_slot]).start()
            return 0
        lax.fori_loop(0, tc, body, 0, unroll=2)

    @pl.when(i == 0)
    def _():
        gather_rows(0, 0)

    @pl.when(i + 1 < n)
    def _():
        gather_rows(i + 1, 1 - slot)

    for k in range(MOE_TOPK):
        pltpu.make_async_copy(ys_hbm.at[pl.ds(0, tc)], buf.at[slot, k], sems.at[slot]).wait()
    half = sh_ref.shape[-1] // 2
    sh = sh_ref[...].astype(F32)
    w = w_ref[...]
    lo_acc, hi_acc = sh[:, :half], sh[:, half:]
    for k in range(MOE_TOPK):
        lo, hi = _unpack_halves(buf[slot, k])
        lo_acc = lo_acc + w[:, k:k + 1] * lo
        hi_acc = hi_acc + w[:, k:k + 1] * hi
    o_ref[...] = jnp.concatenate([lo_acc, hi_acc], axis=1).astype(o_ref.dtype)


def combine(ys, pos, shared, wgt):
    n_tok, d = shared.shape
    tc = COMBINE_TILE
    grid_spec = pltpu.PrefetchScalarGridSpec(
        num_scalar_prefetch=1, grid=(n_tok // tc,),
        in_specs=[pl.BlockSpec(memory_space=pl.ANY),
                  pl.BlockSpec((tc, d), lambda i, p: (i, 0)),
                  pl.BlockSpec((tc, MOE_TOPK), lambda i, p: (i, 0))],
        out_specs=pl.BlockSpec((tc, d), lambda i, p: (i, 0)),
        scratch_shapes=[pltpu.VMEM((2, MOE_TOPK, tc, d // 2), jnp.uint32), pltpu.SemaphoreType.DMA((2,))])
    return pl.pallas_call(
        functools.partial(_combine_kernel, n_tok=n_tok), grid_spec=grid_spec,
        out_shape=jax.ShapeDtypeStruct((n_tok, d), BF16),
        compiler_params=_params(("arbitrary",)), name="moe_combine",
    )(pos, ys, shared, wgt)


def moe_ffn(h, hp, route, w_gate, w_up, w_down, sh_gate, sh_up, sh_down, layer):
    idx_t, wgt_t, rank_t, cnt = route
    pairs = idx_t.shape[0] * idx_t.shape[1]
    tile_e, tile_valid, zero_fill, base = _dispatch_plan(cnt[:, :, 0], pairs, EXPERT_TILE)
    pos = pair_positions(idx_t, rank_t, base).reshape(pairs)
    xs = dispatch(hp, pos, zero_fill, tile_e.shape[0])
    ys = routed_experts(xs, tile_e, tile_valid, w_gate, w_up, w_down, layer)
    shared = shared_expert(h, sh_gate[layer].astype(BF16), sh_up[layer].astype(BF16),
                           sh_down[layer].astype(BF16))
    return combine(ys, pos, shared, wgt_t.T)


def kernel(x, c, ctx, c_ctx, ada_w, ada_b, norm_pre_mix, norm_post_mix, norm_pre_ffn, norm_post_ffn, ret_w_in, ret_w_out, ret_decay_fwd, ret_decay_bwd, diff_w_in, diff_w_out, diff_lam_q1, diff_lam_k1, diff_lam_q2, diff_lam_k2, diff_subln_w, moe_router_w, moe_router_b, moe_w_gate, moe_w_up, moe_w_down, moe_shared_gate, moe_shared_up, moe_shared_down):
    batch, t_len, d = x.shape
    c_len = ctx.shape[1]
    s_len = t_len + c_len
    depth = ada_w.shape[0]
    assert depth == 2 and batch + 1 <= 8
    assert t_len % ROW_TILE == 0 and c_len % ROW_TILE == 0 and t_len % GRID_W == 0
    ret_heads = ret_decay_fwd.shape[-1]
    diff_heads = d // (2 * DIFF_HEAD_DIM)
    lat_tiles, all_tiles = t_len // ROW_TILE, s_len // ROW_TILE

    cc = jnp.concatenate([c, c_ctx[None], jnp.zeros((8 - batch - 1, d), F32)], axis=0)
    mods = ada_modulation(cc, ada_w, ada_b)
    xs = jnp.concatenate([x, ctx], axis=1).reshape(batch * s_len, d)
    rope_ret = _rope_tables(t_len, d // ret_heads)
    rope_diff = _rope_tables(t_len, DIFF_HEAD_DIM)

    ident = lambda i: i
    uni_mod = lambda i: jnp.where(i % all_tiles < lat_tiles, i // all_tiles, batch)
    lat_mod = lambda i: i // lat_tiles
    lat_of_uni = lambda i: (i // lat_tiles) * all_tiles + i % lat_tiles

    (h,) = fused_norm(xs, ident, batch * all_tiles, uni_mod, mod_b=mods[0], w_pre=norm_pre_mix[0],
                      shift_idx=0, scale_idx=1)
    qkvg = matmul(h, ret_w_in, 0)
    lg_f = jax.nn.log_sigmoid(ret_decay_fwd[0].astype(F32))
    lg_b = jax.nn.log_sigmoid(ret_decay_bwd[0].astype(F32))
    r = retention(qkvg, lg_f, lg_b, *rope_ret, batch=batch, t_len=t_len, c_len=c_len, heads=ret_heads)
    y = matmul(r, ret_w_out, 0)
    xs, h, hp, *route = fused_norm(
        xs, ident, batch * all_tiles, uni_mod, y=y, mod_a=mods[0], w_post=norm_post_mix[0], gate_idx=2,
        mod_b=mods[0], w_pre=norm_pre_ffn[0], shift_idx=3, scale_idx=4,
        router_wt=moe_router_w[0].T, router_b=moe_router_b[0])
    f = moe_ffn(h, hp, route, moe_w_gate, moe_w_up, moe_w_down,
                moe_shared_gate, moe_shared_up, moe_shared_down, 0)
    xs, h = fused_norm(xs, ident, batch * all_tiles, uni_mod, y=f, mod_a=mods[0], w_post=norm_post_ffn[0],
                       gate_idx=5, mod_b=mods[1], w_pre=norm_pre_mix[1], shift_idx=0, scale_idx=1)

    qkv = matmul(h, diff_w_in, 0)
    lam_vecs = jnp.stack([diff_lam_q1[0], diff_lam_k1[0], diff_lam_q2[0], diff_lam_k2[0]]).astype(F32)
    lambda_init = 0.8 - 0.6 * math.exp(-0.3 * 1)
    a = diff_attention(qkv, lam_vecs, diff_subln_w[0], *rope_diff, batch=batch, t_len=t_len, c_len=c_len,
                       heads=diff_heads, lambda_init=lambda_init)
    y = matmul(a, diff_w_out, 0)
    xl, h, hp, *route = fused_norm(
        xs, lat_of_uni, batch * lat_tiles, lat_mod, y=y, mod_a=mods[1], w_post=norm_post_mix[1], gate_idx=2,
        mod_b=mods[1], w_pre=norm_pre_ffn[1], shift_idx=3, scale_idx=4,
        router_wt=moe_router_w[1].T, router_b=moe_router_b[1])
    f = moe_ffn(h, hp, route, moe_w_gate, moe_w_up, moe_w_down,
                moe_shared_gate, moe_shared_up, moe_shared_down, 1)
    (out,) = fused_norm(xl, ident, batch * lat_tiles, lat_mod, y=f, mod_a=mods[1], w_post=norm_post_ffn[1],
                        gate_idx=5)
    return out.reshape(batch, t_len, d)
```

```python
import functools
import math

import jax
import jax.numpy as jnp
from jax import lax
from jax.experimental import pallas as pl
from jax.experimental.pallas import tpu as pltpu

GRID_W = 64
N_ADA = 6
NORM_EPS = 1e-6
ROPE_BASE = 10000.0
RET_BLOCK = 256
DIFF_HEAD_DIM = 128
MOE_TOPK = 8
MOE_GROUPS = 8
MOE_TOPK_GROUPS = 4
ROUTED_SCALE = 2.5

LANES = 128
ROW_TILE = 256
EXPERT_TILE = 256
COMBINE_TILE = 64
DISPATCH_TILE = 256
ATTN_KEY_BLOCK = 256
VMEM_LIMIT = 56 * 1024 * 1024

F32 = jnp.float32
BF16 = jnp.bfloat16


def _pick(dim, candidates):
    for c in candidates:
        if dim % c == 0:
            return c
    raise ValueError(f"no tile in {candidates} divides {dim}")


def _params(sem, vmem=VMEM_LIMIT):
    return pltpu.CompilerParams(dimension_semantics=sem, vmem_limit_bytes=vmem)


def _dot(a, b):
    return jnp.dot(a, b, preferred_element_type=F32)


def _dot_nt(a, b):
    return lax.dot_general(a, b, (((1,), (1,)), ((), ())), preferred_element_type=F32)


def _dot_tn(a, b):
    return lax.dot_general(a, b, (((0,), (0,)), ((), ())), preferred_element_type=F32)


def _silu(x):
    return x * jax.nn.sigmoid(x)


def _pack_halves(y):
    w = y.shape[-1] // 2
    lo = lax.bitcast_convert_type(y[:, :w].astype(BF16).astype(F32), jnp.uint32)
    hi = lax.bitcast_convert_type(y[:, w:].astype(BF16).astype(F32), jnp.uint32)
    return (hi & jnp.uint32(0xFFFF0000)) | (lo >> 16)


def _unpack_halves(p):
    lo = lax.bitcast_convert_type(p << 16, F32)
    hi = lax.bitcast_convert_type(p & jnp.uint32(0xFFFF0000), F32)
    return lo, hi


def _ada_kernel(c_ref, w_ref, b_ref, o_ref):
    a = _silu(c_ref[...]).astype(BF16)
    o_ref[...] = _dot(a, w_ref[...].astype(BF16)) + b_ref[...]


def ada_modulation(cc, ada_w, ada_b):
    depth, d, n = ada_w.shape
    tn = _pick(n, (512, 256, 128))
    out = pl.pallas_call(
        _ada_kernel,
        grid=(depth, n // tn),
        in_specs=[pl.BlockSpec((8, d), lambda l, j: (0, 0)),
                  pl.BlockSpec((None, d, tn), lambda l, j: (l, 0, j)),
                  pl.BlockSpec((None, 1, tn), lambda l, j: (l, 0, j))],
        out_specs=pl.BlockSpec((None, 8, tn), lambda l, j: (l, 0, j)),
        out_shape=jax.ShapeDtypeStruct((depth, 8, n), F32),
        compiler_params=_params(("parallel", "parallel")),
        name="ada_modulation",
    )(cc, ada_w, ada_b.reshape(depth, 1, n))
    return out.reshape(depth, 8, N_ADA, d)


def _rms(x):
    return x * lax.rsqrt(jnp.mean(x * x, axis=-1, keepdims=True) + NORM_EPS)


def _route(h, rw_ref, rb_ref, idx_ref, wgt_ref, rank_ref, cnt_ref):
    n_exp = rw_ref.shape[0]
    tm = h.shape[0]
    per_group = n_exp // MOE_GROUPS
    w = rw_ref[...]
    w_hi = w.astype(BF16)
    w_lo = (w - w_hi.astype(F32)).astype(BF16)
    h_hi = h.astype(BF16)
    h_lo = (h - h_hi.astype(F32)).astype(BF16)
    logits = _dot_nt(w_hi, h_hi) + (_dot_nt(w_hi, h_lo) + _dot_nt(w_lo, h_hi))
    scores = jax.nn.sigmoid(logits)
    biased = scores + rb_ref[...]
    neg = jnp.float32(-jnp.inf)
    sub = lax.broadcasted_iota(jnp.int32, (per_group, tm), 0)
    giota = lax.broadcasted_iota(jnp.int32, (MOE_GROUPS, tm), 0)
    gs = jnp.zeros((MOE_GROUPS, tm), F32)
    for g in range(MOE_GROUPS):
        blk = biased[g * per_group:(g + 1) * per_group]
        m1 = jnp.max(blk, axis=0, keepdims=True)
        i1 = jnp.min(jnp.where(blk == m1, sub, per_group), axis=0, keepdims=True)
        m2 = jnp.max(jnp.where(sub == i1, neg, blk), axis=0, keepdims=True)
        gs = jnp.where(giota == g, m1 + m2, gs)
    rank = jnp.zeros((MOE_GROUPS, tm), jnp.int32)
    for j in range(MOE_GROUPS):
        gj = gs[j:j + 1]
        beats = (gj > gs) | ((gj == gs) & (giota > j))
        rank = rank + beats.astype(jnp.int32)
    keep = (rank < MOE_TOPK_GROUPS).astype(F32)
    keep_e = jnp.concatenate(
        [jnp.broadcast_to(keep[g:g + 1], (per_group, tm)) for g in range(MOE_GROUPS)], axis=0)
    masked = jnp.where(keep_e > 0.5, biased, neg)
    eiota = lax.broadcasted_iota(jnp.int32, (n_exp, tm), 0)
    sel_w, hits = [], []
    for k in range(MOE_TOPK):
        m = jnp.max(masked, axis=0, keepdims=True)
        idx = jnp.min(jnp.where(masked == m, eiota, n_exp), axis=0, keepdims=True)
        hit = eiota == idx
        hits.append(hit)
        sel_w.append(jnp.sum(jnp.where(hit, scores, 0.0), axis=0, keepdims=True))
        masked = jnp.where(hit, neg, masked)
        idx_ref[k:k + 1, :] = idx
    total = sel_w[0]
    for k in range(1, MOE_TOPK):
        total = total + sel_w[k]
    for k in range(MOE_TOPK):
        wgt_ref[k:k + 1, :] = sel_w[k] / total * ROUTED_SCALE
    chosen = jnp.zeros((n_exp, tm), F32)
    for k in range(MOE_TOPK):
        chosen = jnp.where(hits[k], 1.0, chosen)
    before = (lax.broadcasted_iota(jnp.int32, (tm, tm), 0)
              < lax.broadcasted_iota(jnp.int32, (tm, tm), 1)).astype(BF16)
    prefix = _dot(chosen.astype(BF16), before)
    for k in range(MOE_TOPK):
        rank_ref[k:k + 1, :] = jnp.sum(jnp.where(hits[k], prefix, 0.0), axis=0,
                                       keepdims=True).astype(jnp.int32)
    cnt_ref[...] = jnp.broadcast_to(jnp.sum(chosen, axis=1, keepdims=True),
                                    cnt_ref.shape).astype(jnp.int32)


def _fused_norm_kernel(*refs, has_resid, has_prenorm, has_router, gate_idx, shift_idx, scale_idx):
    refs = list(refs)
    x_ref = refs.pop(0)
    if has_resid:
        y_ref, mod_a_ref, wpost_ref = refs.pop(0), refs.pop(0), refs.pop(0)
    if has_prenorm:
        mod_b_ref, wpre_ref = refs.pop(0), refs.pop(0)
    if has_router:
        rw_ref, rb_ref = refs.pop(0), refs.pop(0)
    x = x_ref[...]
    if has_resid:
        xo_ref = refs.pop(0)
        y = y_ref[...].astype(F32)
        x = x + mod_a_ref[gate_idx:gate_idx + 1, :] * (_rms(y) * wpost_ref[...])
        xo_ref[...] = x
    if has_prenorm:
        h_ref = refs.pop(0)
        h = (_rms(x) * wpre_ref[...]) * (1.0 + mod_b_ref[scale_idx:scale_idx + 1, :]) \
            + mod_b_ref[shift_idx:shift_idx + 1, :]
        h_ref[...] = h.astype(BF16)
        if has_router:
            hp_ref, idx_ref, wgt_ref, rank_ref, cnt_ref = (refs.pop(0) for _ in range(5))
            hp_ref[...] = _pack_halves(h)
            _route(h, rw_ref, rb_ref, idx_ref, wgt_ref, rank_ref, cnt_ref)


def fused_norm(x, x_tile_map, n_out_tiles, mod_row_map, *, y=None, mod_a=None, w_post=None, gate_idx=0,
               mod_b=None, w_pre=None, shift_idx=0, scale_idx=0, router_wt=None, router_b=None):
    d = x.shape[-1]
    tm = ROW_TILE
    has_resid, has_prenorm, has_router = y is not None, mod_b is not None, router_wt is not None
    row = lambda i: (i, 0)
    const = lambda i: (0, 0)
    mod_spec = pl.BlockSpec((None, N_ADA, d), lambda i: (mod_row_map(i), 0, 0))
    vec_spec = pl.BlockSpec((1, d), const)
    args, in_specs = [x], [pl.BlockSpec((tm, d), lambda i: (x_tile_map(i), 0))]
    out_shape, out_specs = [], []
    n_rows = n_out_tiles * tm
    if has_resid:
        args += [y, mod_a, w_post.reshape(1, d)]
        in_specs += [pl.BlockSpec((tm, d), row), mod_spec, vec_spec]
        out_shape.append(jax.ShapeDtypeStruct((n_rows, d), F32))
        out_specs.append(pl.BlockSpec((tm, d), row))
    if has_prenorm:
        args += [mod_b, w_pre.reshape(1, d)]
        in_specs += [mod_spec, vec_spec]
        out_shape.append(jax.ShapeDtypeStruct((n_rows, d), BF16))
        out_specs.append(pl.BlockSpec((tm, d), row))
    if has_router:
        n_exp = router_wt.shape[0]
        args += [router_wt, router_b.reshape(n_exp, 1)]
        in_specs += [pl.BlockSpec((n_exp, d), const), pl.BlockSpec((n_exp, 1), const)]
        out_shape += [jax.ShapeDtypeStruct((n_rows, d // 2), jnp.uint32),
                      jax.ShapeDtypeStruct((MOE_TOPK, n_rows), jnp.int32),
                      jax.ShapeDtypeStruct((MOE_TOPK, n_rows), F32),
                      jax.ShapeDtypeStruct((MOE_TOPK, n_rows), jnp.int32),
                      jax.ShapeDtypeStruct((n_out_tiles, n_exp, LANES), jnp.int32)]
        out_specs += [pl.BlockSpec((tm, d // 2), row),
                      pl.BlockSpec((MOE_TOPK, tm), lambda i: (0, i)),
                      pl.BlockSpec((MOE_TOPK, tm), lambda i: (0, i)),
                      pl.BlockSpec((MOE_TOPK, tm), lambda i: (0, i)),
                      pl.BlockSpec((None, n_exp, LANES), lambda i: (i, 0, 0))]
    kern = functools.partial(_fused_norm_kernel, has_resid=has_resid, has_prenorm=has_prenorm,
                             has_router=has_router, gate_idx=gate_idx, shift_idx=shift_idx,
                             scale_idx=scale_idx)
    return pl.pallas_call(
        kern, grid=(n_out_tiles,), in_specs=in_specs, out_specs=out_specs, out_shape=out_shape,
        compiler_params=_params(("parallel",)), name="fused_norm",
    )(*args)


def _mm_kernel(a_ref, w_ref, o_ref):
    o_ref[...] = _dot(a_ref[...], w_ref[...].astype(BF16)).astype(o_ref.dtype)


def _mm_acc_kernel(a_ref, w_ref, o_ref, acc_ref):
    k = pl.program_id(2)

    @pl.when(k == 0)
    def _():
        acc_ref[...] = jnp.zeros_like(acc_ref)

    acc_ref[...] += _dot(a_ref[...], w_ref[...].astype(BF16))

    @pl.when(k == pl.num_programs(2) - 1)
    def _():
        o_ref[...] = acc_ref[...].astype(o_ref.dtype)


def matmul(a, w, layer, out_dtype=BF16):
    m, k = a.shape
    n = w.shape[2]
    tm = _pick(m, (1024, 768, 512, 256))
    tn = _pick(n, (512, 256, 128))
    tk = _pick(k, (4096, 2048, 1024, 512))
    if tk == k:
        return pl.pallas_call(
            _mm_kernel, grid=(m // tm, n // tn),
            in_specs=[pl.BlockSpec((tm, k), lambda i, j: (i, 0)),
                      pl.BlockSpec((None, k, tn), lambda i, j: (layer, 0, j))],
            out_specs=pl.BlockSpec((tm, tn), lambda i, j: (i, j)),
            out_shape=jax.ShapeDtypeStruct((m, n), out_dtype),
            compiler_params=_params(("parallel", "parallel")), name="matmul",
        )(a, w)
    return pl.pallas_call(
        _mm_acc_kernel, grid=(m // tm, n // tn, k // tk),
        in_specs=[pl.BlockSpec((tm, tk), lambda i, j, l: (i, l)),
                  pl.BlockSpec((None, tk, tn), lambda i, j, l: (layer, l, j))],
        out_specs=pl.BlockSpec((tm, tn), lambda i, j, l: (i, j)),
        out_shape=jax.ShapeDtypeStruct((m, n), out_dtype),
        scratch_shapes=[pltpu.VMEM((tm, tn), F32)],
        compiler_params=_params(("parallel", "parallel", "arbitrary")), name="matmul_acc",
    )(a, w)


def _rope_tables(t_len, head_dim):
    rows = t_len // GRID_W
    n_freq = head_dim // 4
    row, col = jnp.meshgrid(jnp.arange(rows, dtype=F32), jnp.arange(GRID_W, dtype=F32), indexing="ij")
    inv_freq = ROPE_BASE ** (-jnp.arange(n_freq, dtype=F32) / n_freq)
    ang_r = row.reshape(-1, 1) * inv_freq
    ang_c = col.reshape(-1, 1) * inv_freq
    cr, sr, cc, sc = jnp.cos(ang_r), jnp.sin(ang_r), jnp.cos(ang_c), jnp.sin(ang_c)
    return (jnp.concatenate([cr, cr, cc, cc], axis=-1), jnp.concatenate([-sr, sr, -sc, sc], axis=-1))


def _swap_quarters(x, quarter):
    lane = lax.broadcasted_iota(jnp.int32, x.shape, 1)
    first = (lane % (2 * quarter)) < quarter
    return jnp.where(first, pltpu.roll(x, LANES - quarter, axis=1), pltpu.roll(x, quarter, axis=1))


def _retention_kernel(lgf_ref, lgb_ref, q_ref, k_ref, v_ref, g_ref, cos_ref, sin_ref, o_ref,
                      qr, kr, oacc_f, oacc_b, state_f, state_b, *, t_len, c_len):
    head = pl.program_id(1)
    L = RET_BLOCK
    s_len = t_len + c_len
    dk = q_ref.shape[-1]
    k_scale = dk ** -0.5
    rt = ROW_TILE

    def rope(x, rows):
        sw = jnp.concatenate([pltpu.roll(x[:, :LANES], LANES // 2, axis=1),
                              pltpu.roll(x[:, LANES:], LANES // 2, axis=1)], axis=1)
        return x * cos_ref[rows, :] + sw * sin_ref[rows, :]

    def rope_body(i, _):
        rows = pl.ds(pl.multiple_of(i * rt, rt), rt)
        qr[rows, :] = rope(q_ref[rows, :].astype(F32), rows).astype(BF16)
        kr[rows, :] = (rope(k_ref[rows, :].astype(F32), rows) * k_scale).astype(BF16)
        return 0

    lax.fori_loop(0, t_len // rt, rope_body, 0)
    ctx_rows = pl.ds(t_len, c_len)
    qr[ctx_rows, :] = q_ref[ctx_rows, :]
    kr[ctx_rows, :] = (k_ref[ctx_rows, :].astype(F32) * k_scale).astype(BF16)

    ii = lax.broadcasted_iota(jnp.int32, (L, L), 0)
    jj = lax.broadcasted_iota(jnp.int32, (L, L), 1)
    rel = (ii - jj).astype(F32)
    idx = lax.broadcasted_iota(jnp.int32, (L, 1), 0).astype(F32)

    lg_f, lg_b = lgf_ref[head], lgb_ref[head]
    fwd = (jnp.where(rel >= 0, jnp.exp(lg_f * jnp.maximum(rel, 0.0)), 0.0),
           jnp.exp(lg_f * (idx + 1.0)),
           jnp.exp(lg_f * (L - 1.0 - idx)),
           jnp.exp(lg_f * L), state_f, oacc_f)
    bwd = (jnp.where(rel <= 0, jnp.exp(lg_b * jnp.maximum(-rel, 0.0)), 0.0),
           jnp.exp(lg_b * (L - idx)),
           jnp.exp(lg_b * idx),
           jnp.exp(lg_b * L), state_b, oacc_b)

    def chunk(row0, direction):
        dmat, q_decay, k_decay, chunk_decay, state, oacc = direction
        rows = pl.ds(pl.multiple_of(row0, L), L)
        qb, kb, vb = qr[rows, :], kr[rows, :], v_ref[rows, :]
        scores = _dot_nt(qb, kb) * dmat
        inner = _dot(scores.astype(BF16), vb)
        st = state[...]
        cross = _dot(qb, st.astype(BF16)) * q_decay
        oacc[rows, :] = inner + cross
        kd = (kb.astype(F32) * k_decay).astype(BF16)
        state[...] = st * chunk_decay + _dot_tn(kd, vb)

    def scan(r0, n_chunks):
        def body(ci, _):
            chunk(r0 + ci * L, fwd)
            chunk(r0 + (n_chunks - 1 - ci) * L, bwd)
            return 0
        lax.fori_loop(0, n_chunks, body, 0)

    state_f[...] = jnp.zeros_like(state_f)
    state_b[...] = jnp.zeros_like(state_b)
    scan(t_len, c_len // L)
    scan(0, t_len // L)

    def out_body(i, _):
        rows = pl.ds(pl.multiple_of(i * rt, rt), rt)
        o = _rms(oacc_f[rows, :] + oacc_b[rows, :])
        o_ref[rows, :] = (_silu(g_ref[rows, :].astype(F32)) * o).astype(BF16)
        return 0

    lax.fori_loop(0, s_len // rt, out_body, 0)


def retention(qkvg, lg_f, lg_b, cos, sin, *, batch, t_len, c_len, heads):
    s_len = t_len + c_len
    dk = cos.shape[-1]
    dv = 2 * dk
    assert dk == 2 * LANES
    kern = functools.partial(_retention_kernel, t_len=t_len, c_len=c_len)
    grid_spec = pltpu.PrefetchScalarGridSpec(
        num_scalar_prefetch=2, grid=(batch, heads),
        in_specs=[pl.BlockSpec((s_len, dk), lambda b, h, *_: (b, h)),
                  pl.BlockSpec((s_len, dk), lambda b, h, *_: (b, heads + h)),
                  pl.BlockSpec((s_len, dv), lambda b, h, *_: (b, heads + h)),
                  pl.BlockSpec((s_len, dv), lambda b, h, *_: (b, 2 * heads + h)),
                  pl.BlockSpec((t_len, dk), lambda b, h, *_: (0, 0)),
                  pl.BlockSpec((t_len, dk), lambda b, h, *_: (0, 0))],
        out_specs=pl.BlockSpec((s_len, dv), lambda b, h, *_: (b, h)),
        scratch_shapes=[pltpu.VMEM((s_len, dk), BF16), pltpu.VMEM((s_len, dk), BF16),
                        pltpu.VMEM((s_len, dv), F32), pltpu.VMEM((s_len, dv), F32),
                        pltpu.VMEM((dk, dv), F32), pltpu.VMEM((dk, dv), F32)])
    return pl.pallas_call(
        kern, grid_spec=grid_spec,
        out_shape=jax.ShapeDtypeStruct((batch * s_len, heads * dv), BF16),
        compiler_params=_params(("parallel", "parallel")), name="retention",
    )(lg_f, lg_b, qkvg, qkvg, qkvg, qkvg, cos, sin)


def _diff_attn_kernel(lam_ref, q_ref, k_ref, v_ref, cosq_ref, sinq_ref, cosk_ref, sink_ref, subln_ref,
                      o_ref, kr, *, t_len, c_len, lambda_init):
    hd = DIFF_HEAD_DIM
    quarter = hd // 4
    rt = ROW_TILE

    def rope(x, cos, sin):
        return x * cos + _swap_quarters(x, quarter) * sin

    @pl.when(pl.program_id(2) == 0)
    def _():
        def body(i, _):
            rows = pl.ds(pl.multiple_of(i * rt, rt), rt)
            for c in range(2):
                cols = slice(c * hd, (c + 1) * hd)
                kr[rows, cols] = rope(k_ref[rows, cols].astype(F32), cosk_ref[rows, :],
                                      sink_ref[rows, :]).astype(BF16)
            return 0
        lax.fori_loop(0, t_len // rt, body, 0)
        ctx_rows = pl.ds(t_len, c_len)
        kr[ctx_rows, :] = k_ref[ctx_rows, :]

    lam_v = lam_ref[...]
    lam = (jnp.exp(jnp.sum(lam_v[0:1] * lam_v[1:2], axis=-1, keepdims=True))
           - jnp.exp(jnp.sum(lam_v[2:3] * lam_v[3:4], axis=-1, keepdims=True)) + lambda_init)
    outs = []
    q_scale = (hd ** -0.5) * math.log2(math.e)
    kb = ATTN_KEY_BLOCK
    n_kb = (t_len + c_len) // kb
    for c in range(2):
        cols = slice(c * hd, (c + 1) * hd)
        qc = (rope(q_ref[:, cols].astype(F32), cosq_ref[...], sinq_ref[...]) * q_scale).astype(BF16)
        scores = [_dot_nt(qc, kr[j * kb:(j + 1) * kb, cols]) for j in range(n_kb)]
        wide_max = scores[0]
        for j in range(1, n_kb):
            wide_max = jnp.maximum(wide_max, scores[j])
        row_max = jnp.max(wide_max, axis=-1, keepdims=True)
        acc = wide_sum = None
        for j in range(n_kb):
            e = jnp.exp2(scores[j] - row_max)
            pv = _dot(e.astype(BF16), v_ref[j * kb:(j + 1) * kb, :])
            acc = pv if acc is None else acc + pv
            wide_sum = e if wide_sum is None else wide_sum + e
        outs.append(acc / jnp.sum(wide_sum, axis=-1, keepdims=True))
    o = outs[0] - lam * outs[1]
    o_ref[...] = ((_rms(o) * subln_ref[...]) * (1.0 - lambda_init)).astype(BF16)


def diff_attention(qkv, lam_vecs, subln_w, cos, sin, *, batch, t_len, c_len, heads, lambda_init):
    s_len = t_len + c_len
    hd = DIFF_HEAD_DIM
    tq = ROW_TILE
    q_tiles, s_tiles = t_len // tq, s_len // tq
    kern = functools.partial(_diff_attn_kernel, t_len=t_len, c_len=c_len, lambda_init=lambda_init)
    return pl.pallas_call(
        kern, grid=(batch, heads, q_tiles),
        in_specs=[pl.BlockSpec((4, hd), lambda b, h, i: (0, 0)),
                  pl.BlockSpec((tq, 2 * hd), lambda b, h, i: (b * s_tiles + i, h)),
                  pl.BlockSpec((s_len, 2 * hd), lambda b, h, i: (b, heads + h)),
                  pl.BlockSpec((s_len, 2 * hd), lambda b, h, i: (b, 2 * heads + h)),
                  pl.BlockSpec((tq, hd), lambda b, h, i: (i, 0)),
                  pl.BlockSpec((tq, hd), lambda b, h, i: (i, 0)),
                  pl.BlockSpec((t_len, hd), lambda b, h, i: (0, 0)),
                  pl.BlockSpec((t_len, hd), lambda b, h, i: (0, 0)),
                  pl.BlockSpec((1, 2 * hd), lambda b, h, i: (0, 0))],
        out_specs=pl.BlockSpec((tq, 2 * hd), lambda b, h, i: (b * q_tiles + i, h)),
        out_shape=jax.ShapeDtypeStruct((batch * t_len, heads * 2 * hd), BF16),
        scratch_shapes=[pltpu.VMEM((s_len, 2 * hd), BF16)],
        compiler_params=_params(("parallel", "parallel", "arbitrary")), name="diff_attention",
    )(lam_vecs, qkv, qkv, qkv, cos, sin, cos, sin, subln_w.reshape(1, 2 * hd))


def _shared_kernel(x_ref, wg_ref, wu_ref, wd_ref, o_ref, acc_ref):
    f = pl.program_id(1)

    @pl.when(f == 0)
    def _():
        acc_ref[...] = jnp.zeros_like(acc_ref)

    x = x_ref[...]
    hid = _silu(_dot(x, wg_ref[...])) * _dot(x, wu_ref[...])
    acc_ref[...] += _dot(hid.astype(BF16), wd_ref[...])

    @pl.when(f == pl.num_programs(1) - 1)
    def _():
        o_ref[...] = acc_ref[...].astype(o_ref.dtype)


def shared_expert(h, w_gate, w_up, w_down):
    m, d = h.shape
    f_dim = w_gate.shape[1]
    tm = _pick(m, (512, 256))
    tf = _pick(f_dim, (256, 128))
    return pl.pallas_call(
        _shared_kernel, grid=(m // tm, f_dim // tf),
        in_specs=[pl.BlockSpec((tm, d), lambda i, f: (i, 0)),
                  pl.BlockSpec((d, tf), lambda i, f: (0, f)),
                  pl.BlockSpec((d, tf), lambda i, f: (0, f)),
                  pl.BlockSpec((tf, d), lambda i, f: (f, 0))],
        out_specs=pl.BlockSpec((tm, d), lambda i, f: (i, 0)),
        out_shape=jax.ShapeDtypeStruct((m, d), BF16),
        scratch_shapes=[pltpu.VMEM((tm, d), F32)],
        compiler_params=_params(("parallel", "arbitrary")), name="shared_expert",
    )(h, w_gate, w_up, w_down)


def _dispatch_plan(cnt, pairs, tm):
    n_exp = cnt.shape[1]
    total = jnp.sum(cnt, axis=0)
    padded = (total + tm - 1) // tm * tm
    pend = jnp.cumsum(padded)
    base = (pend - padded)[None, :] + jnp.cumsum(cnt, axis=0) - cnt
    n_tiles = pairs // tm + n_exp
    tile_start = jnp.arange(n_tiles, dtype=jnp.int32) * tm
    valid = tile_start < pend[-1]
    tile_e = jnp.minimum(jnp.sum((tile_start[:, None] >= pend[None, :]).astype(jnp.int32), axis=1), n_exp - 1)
    tile_e = jnp.where(valid, tile_e, jnp.max(jnp.where(valid, tile_e, 0)))
    next_e = jnp.concatenate([tile_e[1:], jnp.full((1,), -1, jnp.int32)])
    next_valid = jnp.concatenate([valid[1:], jnp.zeros((1,), bool)])
    zero_fill = (~valid) | (tile_e != next_e) | (~next_valid)
    return (tile_e.astype(jnp.int32), valid.astype(jnp.int32), zero_fill.astype(jnp.int32),
            base.astype(jnp.int32))


def _positions_kernel(idx_ref, rank_ref, base_ref, pos_ref):
    n_exp = base_ref.shape[0]
    tm = idx_ref.shape[1]
    eiota = lax.broadcasted_iota(jnp.int32, (n_exp, tm), 0)
    base = jnp.broadcast_to(base_ref[...].astype(F32), (n_exp, tm))
    for k in range(MOE_TOPK):
        hit = eiota == idx_ref[k:k + 1, :]
        first = jnp.sum(jnp.where(hit, base, 0.0), axis=0, keepdims=True)
        pos_ref[k:k + 1, :] = first.astype(jnp.int32) + rank_ref[k:k + 1, :]


def pair_positions(idx_t, rank_t, base):
    k, n = idx_t.shape
    tiles, n_exp = base.shape
    tm = n // tiles
    spec = pl.BlockSpec((k, tm), lambda i: (0, i))
    return pl.pallas_call(
        _positions_kernel, grid=(tiles,),
        in_specs=[spec, spec, pl.BlockSpec((None, n_exp, 1), lambda i: (i, 0, 0))],
        out_specs=spec, out_shape=jax.ShapeDtypeStruct((k, n), jnp.int32),
        compiler_params=_params(("parallel",)), name="pair_positions",
    )(idx_t, rank_t, base.reshape(tiles, n_exp, 1))


def _dispatch_kernel(pos_ref, zf_ref, hp_ref, xs_hbm, zeros, sem_z, sem_s, *, n_tok, n_tiles):
    i = pl.program_id(0)
    td = hp_ref.shape[0]
    tm = zeros.shape[0]

    def zero_copy(j):
        return pltpu.make_async_copy(zeros, xs_hbm.at[pl.ds(pl.multiple_of(j * tm, tm), tm)], sem_z)

    @pl.when(i == 0)
    def _():
        zeros[...] = jnp.zeros_like(zeros)

        def start(j, _):
            @pl.when(zf_ref[j] == 1)
            def _():
                zero_copy(j).start()
            return 0

        def wait(j, _):
            @pl.when(zf_ref[j] == 1)
            def _():
                zero_copy(j).wait()
            return 0

        lax.fori_loop(0, n_tiles, start, 0)
        lax.fori_loop(0, n_tiles, wait, 0)

    def body(t, _):
        for k in range(MOE_TOPK):
            row = pos_ref[k * n_tok + i * td + t]
            pltpu.make_async_copy(hp_ref.at[pl.ds(t, 1)], xs_hbm.at[pl.ds(row, 1)], sem_s).start()
        return 0

    lax.fori_loop(0, td, body, 0, unroll=2)
    for k in range(MOE_TOPK):
        pltpu.make_async_copy(hp_ref, xs_hbm.at[pl.ds(0, td)], sem_s).wait()


def dispatch(hp, pos, zero_fill, n_tiles):
    n_tok, half = hp.shape
    tm = EXPERT_TILE
    grid_spec = pltpu.PrefetchScalarGridSpec(
        num_scalar_prefetch=2, grid=(n_tok // DISPATCH_TILE,),
        in_specs=[pl.BlockSpec((DISPATCH_TILE, half), lambda i, p, z: (i, 0))],
        out_specs=pl.BlockSpec(memory_space=pl.ANY),
        scratch_shapes=[pltpu.VMEM((tm, half), jnp.uint32), pltpu.SemaphoreType.DMA(()),
                        pltpu.SemaphoreType.DMA(())])
    return pl.pallas_call(
        functools.partial(_dispatch_kernel, n_tok=n_tok, n_tiles=n_tiles), grid_spec=grid_spec,
        out_shape=jax.ShapeDtypeStruct((n_tiles * tm, half), jnp.uint32),
        compiler_params=_params(("arbitrary",)), name="moe_dispatch",
    )(pos, zero_fill, hp)


def _expert_kernel(te_ref, tv_ref, x_ref, wg_ref, wu_ref, wd_ref, o_ref, wgb, wub, wdb):
    i = pl.program_id(0)
    prev = jnp.maximum(i - 1, 0)

    @pl.when((i == 0) | (te_ref[i] != te_ref[prev]))
    def _():
        wgb[...] = wg_ref[...].astype(BF16)
        wub[...] = wu_ref[...].astype(BF16)
        wdb[...] = wd_ref[...].astype(BF16)

    @pl.when(tv_ref[i] == 1)
    def _():
        lo, hi = _unpack_halves(x_ref[...])
        x = jnp.concatenate([lo.astype(BF16), hi.astype(BF16)], axis=1)
        hid = _silu(_dot(x, wgb[...])) * _dot(x, wub[...])
        o_ref[...] = _pack_halves(_dot(hid.astype(BF16), wdb[...]))

    @pl.when(tv_ref[i] == 0)
    def _():
        o_ref[...] = jnp.zeros_like(o_ref)


def routed_experts(xs, tile_e, tile_valid, w_gate, w_up, w_down, layer):
    tm = EXPERT_TILE
    n_tiles = tile_e.shape[0]
    _, n_exp, d, f = w_gate.shape
    wmap = lambda i, te, tv: (layer, te[i], 0, 0)
    grid_spec = pltpu.PrefetchScalarGridSpec(
        num_scalar_prefetch=2, grid=(n_tiles,),
        in_specs=[pl.BlockSpec((tm, d // 2), lambda i, te, tv: (i, 0)),
                  pl.BlockSpec((None, None, d, f), wmap),
                  pl.BlockSpec((None, None, d, f), wmap),
                  pl.BlockSpec((None, None, f, d), wmap)],
        out_specs=pl.BlockSpec((tm, d // 2), lambda i, te, tv: (i, 0)),
        scratch_shapes=[pltpu.VMEM((d, f), BF16), pltpu.VMEM((d, f), BF16), pltpu.VMEM((f, d), BF16)])
    return pl.pallas_call(
        _expert_kernel, grid_spec=grid_spec,
        out_shape=jax.ShapeDtypeStruct((n_tiles * tm, d // 2), jnp.uint32),
        compiler_params=_params(("arbitrary",)), name="routed_experts",
    )(tile_e, tile_valid, xs, w_gate, w_up, w_down)


def _combine_kernel(pos_ref, ys_hbm, sh_ref, w_ref, o_ref, buf, sems, *, n_tok):
    i = pl.program_id(0)
    n = pl.num_programs(0)
    tc = buf.shape[2]
    slot = i % 2

    def gather_rows(tile, dst_slot):
        def body(t, _):
            for k in range(MOE_TOPK):
                p = pos_ref[k * n_tok + tile * tc + t]
                pltpu.make_async_copy(ys_hbm.at[pl.ds(p, 1)], buf.at[dst_slot, k, pl.ds(t, 1)],
                                      sems.at[dst_slot]).start()
            return 0
        lax.fori_loop(0, tc, body, 0, unroll=2)

    @pl.when(i == 0)
    def _():
        gather_rows(0, 0)

    @pl.when(i + 1 < n)
    def _():
        gather_rows(i + 1, 1 - slot)

    for k in range(MOE_TOPK):
        pltpu.make_async_copy(ys_hbm.at[pl.ds(0, tc)], buf.at[slot, k], sems.at[slot]).wait()
    half = sh_ref.shape[-1] // 2
    sh = sh_ref[...].astype(F32)
    w = w_ref[...]
    lo_acc, hi_acc = sh[:, :half], sh[:, half:]
    for k in range(MOE_TOPK):
        lo, hi = _unpack_halves(buf[slot, k])
        lo_acc = lo_acc + w[:, k:k + 1] * lo
        hi_acc = hi_acc + w[:, k:k + 1] * hi
    o_ref[...] = jnp.concatenate([lo_acc, hi_acc], axis=1).astype(o_ref.dtype)


def combine(ys, pos, shared, wgt):
    n_tok, d = shared.shape
    tc = COMBINE_TILE
    grid_spec = pltpu.PrefetchScalarGridSpec(
        num_scalar_prefetch=1, grid=(n_tok // tc,),
        in_specs=[pl.BlockSpec(memory_space=pl.ANY),
                  pl.BlockSpec((tc, d), lambda i, p: (i, 0)),
                  pl.BlockSpec((tc, MOE_TOPK), lambda i, p: (i, 0))],
        out_specs=pl.BlockSpec((tc, d), lambda i, p: (i, 0)),
        scratch_shapes=[pltpu.VMEM((2, MOE_TOPK, tc, d // 2), jnp.uint32), pltpu.SemaphoreType.DMA((2,))])
    return pl.pallas_call(
        functools.partial(_combine_kernel, n_tok=n_tok), grid_spec=grid_spec,
        out_shape=jax.ShapeDtypeStruct((n_tok, d), BF16),
        compiler_params=_params(("arbitrary",)), name="moe_combine",
    )(pos, ys, shared, wgt)


def moe_ffn(h, hp, route, w_gate, w_up, w_down, sh_gate, sh_up, sh_down, layer):
    idx_t, wgt_t, rank_t, cnt = route
    pairs = idx_t.shape[0] * idx_t.shape[1]
    tile_e, tile_valid, zero_fill, base = _dispatch_plan(cnt[:, :, 0], pairs, EXPERT_TILE)
    pos = pair_positions(idx_t, rank_t, base).reshape(pairs)
    xs = dispatch(hp, pos, zero_fill, tile_e.shape[0])
    ys = routed_experts(xs, tile_e, tile_valid, w_gate, w_up, w_down, layer)
    shared = shared_expert(h, sh_gate[layer].astype(BF16), sh_up[layer].astype(BF16),
                           sh_down[layer].astype(BF16))
    return combine(ys, pos, shared, wgt_t.T)


def kernel(x, c, ctx, c_ctx, ada_w, ada_b, norm_pre_mix, norm_post_mix, norm_pre_ffn, norm_post_ffn, ret_w_in, ret_w_out, ret_decay_fwd, ret_decay_bwd, diff_w_in, diff_w_out, diff_lam_q1, diff_lam_k1, diff_lam_q2, diff_lam_k2, diff_subln_w, moe_router_w, moe_router_b, moe_w_gate, moe_w_up, moe_w_down, moe_shared_gate, moe_shared_up, moe_shared_down):
    batch, t_len, d = x.shape
    c_len = ctx.shape[1]
    s_len = t_len + c_len
    depth = ada_w.shape[0]
    assert depth == 2 and batch + 1 <= 8
    assert t_len % ROW_TILE == 0 and c_len % ROW_TILE == 0 and t_len % GRID_W == 0
    ret_heads = ret_decay_fwd.shape[-1]
    diff_heads = d // (2 * DIFF_HEAD_DIM)
    lat_tiles, all_tiles = t_len // ROW_TILE, s_len // ROW_TILE

    cc = jnp.concatenate([c, c_ctx[None], jnp.zeros((8 - batch - 1, d), F32)], axis=0)
    mods = ada_modulation(cc, ada_w, ada_b)
    xs = jnp.concatenate([x, ctx], axis=1).reshape(batch * s_len, d)
    rope_ret = _rope_tables(t_len, d // ret_heads)
    rope_diff = _rope_tables(t_len, DIFF_HEAD_DIM)

    ident = lambda i: i
    uni_mod = lambda i: jnp.where(i % all_tiles < lat_tiles, i // all_tiles, batch)
    lat_mod = lambda i: i // lat_tiles
    lat_of_uni = lambda i: (i // lat_tiles) * all_tiles + i % lat_tiles

    (h,) = fused_norm(xs, ident, batch * all_tiles, uni_mod, mod_b=mods[0], w_pre=norm_pre_mix[0],
                      shift_idx=0, scale_idx=1)
    qkvg = matmul(h, ret_w_in, 0)
    lg_f = jax.nn.log_sigmoid(ret_decay_fwd[0].astype(F32))
    lg_b = jax.nn.log_sigmoid(ret_decay_bwd[0].astype(F32))
    r = retention(qkvg, lg_f, lg_b, *rope_ret, batch=batch, t_len=t_len, c_len=c_len, heads=ret_heads)
    y = matmul(r, ret_w_out, 0)
    xs, h, hp, *route = fused_norm(
        xs, ident, batch * all_tiles, uni_mod, y=y, mod_a=mods[0], w_post=norm_post_mix[0], gate_idx=2,
        mod_b=mods[0], w_pre=norm_pre_ffn[0], shift_idx=3, scale_idx=4,
        router_wt=moe_router_w[0].T, router_b=moe_router_b[0])
    f = moe_ffn(h, hp, route, moe_w_gate, moe_w_up, moe_w_down,
                moe_shared_gate, moe_shared_up, moe_shared_down, 0)
    xs, h = fused_norm(xs, ident, batch * all_tiles, uni_mod, y=f, mod_a=mods[0], w_post=norm_post_ffn[0],
                       gate_idx=5, mod_b=mods[1], w_pre=norm_pre_mix[1], shift_idx=0, scale_idx=1)

    qkv = matmul(h, diff_w_in, 0)
    lam_vecs = jnp.stack([diff_lam_q1[0], diff_lam_k1[0], diff_lam_q2[0], diff_lam_k2[0]]).astype(F32)
    lambda_init = 0.8 - 0.6 * math.exp(-0.3 * 1)
    a = diff_attention(qkv, lam_vecs, diff_subln_w[0], *rope_diff, batch=batch, t_len=t_len, c_len=c_len,
                       heads=diff_heads, lambda_init=lambda_init)
    y = matmul(a, diff_w_out, 0)
    xl, h, hp, *route = fused_norm(
        xs, lat_of_uni, batch * lat_tiles, lat_mod, y=y, mod_a=mods[1], w_post=norm_post_mix[1], gate_idx=2,
        mod_b=mods[1], w_pre=norm_pre_ffn[1], shift_idx=3, scale_idx=4,
        router_wt=moe_router_w[1].T, router_b=moe_router_b[1])
    f = moe_ffn(h, hp, route, moe_w_gate, moe_w_up, moe_w_down,
                moe_shared_gate, moe_shared_up, moe_shared_down, 1)
    (out,) = fused_norm(xl, ident, batch * lat_tiles, lat_mod, y=f, mod_a=mods[1], w_post=norm_post_ffn[1],
                        gate_idx=5)
    return out.reshape(batch, t_len, d)
```

```python
import functools
import math

import jax
import jax.numpy as jnp
from jax import lax
from jax.experimental import pallas as pl
from jax.experimental.pallas import tpu as pltpu

GRID_W = 64
N_ADA = 6
NORM_EPS = 1e-6
ROPE_BASE = 10000.0
RET_BLOCK = 256
DIFF_HEAD_DIM = 128
MOE_TOPK = 8
MOE_GROUPS = 8
MOE_TOPK_GROUPS = 4
ROUTED_SCALE = 2.5

LANES = 128
ROW_TILE = 256
EXPERT_TILE = 256
COMBINE_TILE = 64
DISPATCH_TILE = 256
ATTN_Q_TILES = 4
VMEM_LIMIT = 56 * 1024 * 1024

F32 = jnp.float32
BF16 = jnp.bfloat16


def _pick(dim, candidates):
    for c in candidates:
        if dim % c == 0:
            return c
    raise ValueError(f"no tile in {candidates} divides {dim}")


def _params(sem, vmem=VMEM_LIMIT):
    return pltpu.CompilerParams(dimension_semantics=sem, vmem_limit_bytes=vmem)


def _dot(a, b):
    return jnp.dot(a, b, preferred_element_type=F32)


def _dot_nt(a, b):
    return lax.dot_general(a, b, (((1,), (1,)), ((), ())), preferred_element_type=F32)


def _dot_tn(a, b):
    return lax.dot_general(a, b, (((0,), (0,)), ((), ())), preferred_element_type=F32)


def _silu(x):
    return x * jax.nn.sigmoid(x)


def _pack_halves(y):
    w = y.shape[-1] // 2
    lo = lax.bitcast_convert_type(y[:, :w].astype(BF16).astype(F32), jnp.uint32)
    hi = lax.bitcast_convert_type(y[:, w:].astype(BF16).astype(F32), jnp.uint32)
    return (hi & jnp.uint32(0xFFFF0000)) | (lo >> 16)


def _unpack_halves(p):
    lo = lax.bitcast_convert_type(p << 16, F32)
    hi = lax.bitcast_convert_type(p & jnp.uint32(0xFFFF0000), F32)
    return lo, hi


def _ada_kernel(c_ref, w_ref, b_ref, o_ref):
    a = _silu(c_ref[...]).astype(BF16)
    o_ref[...] = _dot(a, w_ref[...].astype(BF16)) + b_ref[...]


def ada_modulation(cc, ada_w, ada_b):
    depth, d, n = ada_w.shape
    tn = _pick(n, (512, 256, 128))
    out = pl.pallas_call(
        _ada_kernel,
        grid=(depth, n // tn),
        in_specs=[pl.BlockSpec((8, d), lambda l, j: (0, 0)),
                  pl.BlockSpec((None, d, tn), lambda l, j: (l, 0, j)),
                  pl.BlockSpec((None, 1, tn), lambda l, j: (l, 0, j))],
        out_specs=pl.BlockSpec((None, 8, tn), lambda l, j: (l, 0, j)),
        out_shape=jax.ShapeDtypeStruct((depth, 8, n), F32),
        compiler_params=_params(("parallel", "parallel")),
        name="ada_modulation",
    )(cc, ada_w, ada_b.reshape(depth, 1, n))
    return out.reshape(depth, 8, N_ADA, d)


def _rms(x):
    return x * lax.rsqrt(jnp.mean(x * x, axis=-1, keepdims=True) + NORM_EPS)


def _route(h, rw_ref, rb_ref, idx_ref, wgt_ref, rank_ref, cnt_ref):
    n_exp = rw_ref.shape[0]
    tm = h.shape[0]
    per_group = n_exp // MOE_GROUPS
    w = rw_ref[...]
    w_hi = w.astype(BF16)
    w_lo = (w - w_hi.astype(F32)).astype(BF16)
    h_hi = h.astype(BF16)
    h_lo = (h - h_hi.astype(F32)).astype(BF16)
    logits = _dot_nt(w_hi, h_hi) + (_dot_nt(w_hi, h_lo) + _dot_nt(w_lo, h_hi))
    scores = jax.nn.sigmoid(logits)
    biased = scores + rb_ref[...]
    neg = jnp.float32(-jnp.inf)
    sub = lax.broadcasted_iota(jnp.int32, (per_group, tm), 0)
    giota = lax.broadcasted_iota(jnp.int32, (MOE_GROUPS, tm), 0)
    gs = jnp.zeros((MOE_GROUPS, tm), F32)
    for g in range(MOE_GROUPS):
        blk = biased[g * per_group:(g + 1) * per_group]
        m1 = jnp.max(blk, axis=0, keepdims=True)
        i1 = jnp.min(jnp.where(blk == m1, sub, per_group), axis=0, keepdims=True)
        m2 = jnp.max(jnp.where(sub == i1, neg, blk), axis=0, keepdims=True)
        gs = jnp.where(giota == g, m1 + m2, gs)
    rank = jnp.zeros((MOE_GROUPS, tm), jnp.int32)
    for j in range(MOE_GROUPS):
        gj = gs[j:j + 1]
        beats = (gj > gs) | ((gj == gs) & (giota > j))
        rank = rank + beats.astype(jnp.int32)
    keep = (rank < MOE_TOPK_GROUPS).astype(F32)
    keep_e = jnp.concatenate(
        [jnp.broadcast_to(keep[g:g + 1], (per_group, tm)) for g in range(MOE_GROUPS)], axis=0)
    masked = jnp.where(keep_e > 0.5, biased, neg)
    eiota = lax.broadcasted_iota(jnp.int32, (n_exp, tm), 0)
    sel_w, hits = [], []
    for k in range(MOE_TOPK):
        m = jnp.max(masked, axis=0, keepdims=True)
        idx = jnp.min(jnp.where(masked == m, eiota, n_exp), axis=0, keepdims=True)
        hit = eiota == idx
        hits.append(hit)
        sel_w.append(jnp.sum(jnp.where(hit, scores, 0.0), axis=0, keepdims=True))
        masked = jnp.where(hit, neg, masked)
        idx_ref[k:k + 1, :] = idx
    total = sel_w[0]
    for k in range(1, MOE_TOPK):
        total = total + sel_w[k]
    for k in range(MOE_TOPK):
        wgt_ref[k:k + 1, :] = sel_w[k] / total * ROUTED_SCALE
    chosen = jnp.zeros((n_exp, tm), F32)
    for k in range(MOE_TOPK):
        chosen = jnp.where(hits[k], 1.0, chosen)
    before = (lax.broadcasted_iota(jnp.int32, (tm, tm), 0)
              < lax.broadcasted_iota(jnp.int32, (tm, tm), 1)).astype(BF16)
    prefix = _dot(chosen.astype(BF16), before)
    for k in range(MOE_TOPK):
        rank_ref[k:k + 1, :] = jnp.sum(jnp.where(hits[k], prefix, 0.0), axis=0,
                                       keepdims=True).astype(jnp.int32)
    cnt_ref[...] = jnp.broadcast_to(jnp.sum(chosen, axis=1, keepdims=True),
                                    cnt_ref.shape).astype(jnp.int32)


def _fused_norm_kernel(*refs, has_resid, has_prenorm, has_router, gate_idx, shift_idx, scale_idx):
    refs = list(refs)
    x_ref = refs.pop(0)
    if has_resid:
        y_ref, mod_a_ref, wpost_ref = refs.pop(0), refs.pop(0), refs.pop(0)
    if has_prenorm:
        mod_b_ref, wpre_ref = refs.pop(0), refs.pop(0)
    if has_router:
        rw_ref, rb_ref = refs.pop(0), refs.pop(0)
    x = x_ref[...]
    if has_resid:
        xo_ref = refs.pop(0)
        y = y_ref[...].astype(F32)
        x = x + mod_a_ref[gate_idx:gate_idx + 1, :] * (_rms(y) * wpost_ref[...])
        xo_ref[...] = x
    if has_prenorm:
        h_ref = refs.pop(0)
        h = (_rms(x) * wpre_ref[...]) * (1.0 + mod_b_ref[scale_idx:scale_idx + 1, :]) \
            + mod_b_ref[shift_idx:shift_idx + 1, :]
        h_ref[...] = h.astype(BF16)
        if has_router:
            hp_ref, idx_ref, wgt_ref, rank_ref, cnt_ref = (refs.pop(0) for _ in range(5))
            hp_ref[...] = _pack_halves(h)
            _route(h, rw_ref, rb_ref, idx_ref, wgt_ref, rank_ref, cnt_ref)


def fused_norm(x, x_tile_map, n_out_tiles, mod_row_map, *, y=None, mod_a=None, w_post=None, gate_idx=0,
               mod_b=None, w_pre=None, shift_idx=0, scale_idx=0, router_wt=None, router_b=None):
    d = x.shape[-1]
    tm = ROW_TILE
    has_resid, has_prenorm, has_router = y is not None, mod_b is not None, router_wt is not None
    row = lambda i: (i, 0)
    const = lambda i: (0, 0)
    mod_spec = pl.BlockSpec((None, N_ADA, d), lambda i: (mod_row_map(i), 0, 0))
    vec_spec = pl.BlockSpec((1, d), const)
    args, in_specs = [x], [pl.BlockSpec((tm, d), lambda i: (x_tile_map(i), 0))]
    out_shape, out_specs = [], []
    n_rows = n_out_tiles * tm
    if has_resid:
        args += [y, mod_a, w_post.reshape(1, d)]
        in_specs += [pl.BlockSpec((tm, d), row), mod_spec, vec_spec]
        out_shape.append(jax.ShapeDtypeStruct((n_rows, d), F32))
        out_specs.append(pl.BlockSpec((tm, d), row))
    if has_prenorm:
        args += [mod_b, w_pre.reshape(1, d)]
        in_specs += [mod_spec, vec_spec]
        out_shape.append(jax.ShapeDtypeStruct((n_rows, d), BF16))
        out_specs.append(pl.BlockSpec((tm, d), row))
    if has_router:
        n_exp = router_wt.shape[0]
        args += [router_wt, router_b.reshape(n_exp, 1)]
        in_specs += [pl.BlockSpec((n_exp, d), const), pl.BlockSpec((n_exp, 1), const)]
        out_shape += [jax.ShapeDtypeStruct((n_rows, d // 2), jnp.uint32),
                      jax.ShapeDtypeStruct((MOE_TOPK, n_rows), jnp.int32),
                      jax.ShapeDtypeStruct((MOE_TOPK, n_rows), F32),
                      jax.ShapeDtypeStruct((MOE_TOPK, n_rows), jnp.int32),
                      jax.ShapeDtypeStruct((n_out_tiles, n_exp, LANES), jnp.int32)]
        out_specs += [pl.BlockSpec((tm, d // 2), row),
                      pl.BlockSpec((MOE_TOPK, tm), lambda i: (0, i)),
                      pl.BlockSpec((MOE_TOPK, tm), lambda i: (0, i)),
                      pl.BlockSpec((MOE_TOPK, tm), lambda i: (0, i)),
                      pl.BlockSpec((None, n_exp, LANES), lambda i: (i, 0, 0))]
    kern = functools.partial(_fused_norm_kernel, has_resid=has_resid, has_prenorm=has_prenorm,
                             has_router=has_router, gate_idx=gate_idx, shift_idx=shift_idx,
                             scale_idx=scale_idx)
    return pl.pallas_call(
        kern, grid=(n_out_tiles,), in_specs=in_specs, out_specs=out_specs, out_shape=out_shape,
        compiler_params=_params(("parallel",)), name="fused_norm",
    )(*args)


def _mm_kernel(a_ref, w_ref, o_ref):
    o_ref[...] = _dot(a_ref[...], w_ref[...].astype(BF16)).astype(o_ref.dtype)


def _mm_acc_kernel(a_ref, w_ref, o_ref, acc_ref):
    k = pl.program_id(2)

    @pl.when(k == 0)
    def _():
        acc_ref[...] = jnp.zeros_like(acc_ref)

    acc_ref[...] += _dot(a_ref[...], w_ref[...].astype(BF16))

    @pl.when(k == pl.num_programs(2) - 1)
    def _():
        o_ref[...] = acc_ref[...].astype(o_ref.dtype)


def matmul(a, w, layer, out_dtype=BF16):
    m, k = a.shape
    n = w.shape[2]
    tm = _pick(m, (1024, 768, 512, 256))
    tn = _pick(n, (512, 256, 128))
    tk = _pick(k, (4096, 2048, 1024, 512))
    if tk == k:
        return pl.pallas_call(
            _mm_kernel, grid=(m // tm, n // tn),
            in_specs=[pl.BlockSpec((tm, k), lambda i, j: (i, 0)),
                      pl.BlockSpec((None, k, tn), lambda i, j: (layer, 0, j))],
            out_specs=pl.BlockSpec((tm, tn), lambda i, j: (i, j)),
            out_shape=jax.ShapeDtypeStruct((m, n), out_dtype),
            compiler_params=_params(("parallel", "parallel")), name="matmul",
        )(a, w)
    return pl.pallas_call(
        _mm_acc_kernel, grid=(m // tm, n // tn, k // tk),
        in_specs=[pl.BlockSpec((tm, tk), lambda i, j, l: (i, l)),
                  pl.BlockSpec((None, tk, tn), lambda i, j, l: (layer, l, j))],
        out_specs=pl.BlockSpec((tm, tn), lambda i, j, l: (i, j)),
        out_shape=jax.ShapeDtypeStruct((m, n), out_dtype),
        scratch_shapes=[pltpu.VMEM((tm, tn), F32)],
        compiler_params=_params(("parallel", "parallel", "arbitrary")), name="matmul_acc",
    )(a, w)


def _rope_tables(t_len, head_dim):
    rows = t_len // GRID_W
    n_freq = head_dim // 4
    row, col = jnp.meshgrid(jnp.arange(rows, dtype=F32), jnp.arange(GRID_W, dtype=F32), indexing="ij")
    inv_freq = ROPE_BASE ** (-jnp.arange(n_freq, dtype=F32) / n_freq)
    ang_r = row.reshape(-1, 1) * inv_freq
    ang_c = col.reshape(-1, 1) * inv_freq
    cr, sr, cc, sc = jnp.cos(ang_r), jnp.sin(ang_r), jnp.cos(ang_c), jnp.sin(ang_c)
    return (jnp.concatenate([cr, cr, cc, cc], axis=-1), jnp.concatenate([-sr, sr, -sc, sc], axis=-1))


def _swap_quarters(x, quarter):
    lane = lax.broadcasted_iota(jnp.int32, x.shape, 1)
    first = (lane % (2 * quarter)) < quarter
    return jnp.where(first, pltpu.roll(x, LANES - quarter, axis=1), pltpu.roll(x, quarter, axis=1))


def _retention_kernel(lgf_ref, lgb_ref, q_ref, k_ref, v_ref, g_ref, cos_ref, sin_ref, o_ref,
                      qr, kr, oacc_f, oacc_b, state_f, state_b, *, t_len, c_len):
    head = pl.program_id(1)
    L = RET_BLOCK
    s_len = t_len + c_len
    dk = q_ref.shape[-1]
    k_scale = dk ** -0.5
    rt = ROW_TILE

    def rope(x, rows):
        sw = jnp.concatenate([pltpu.roll(x[:, :LANES], LANES // 2, axis=1),
                              pltpu.roll(x[:, LANES:], LANES // 2, axis=1)], axis=1)
        return x * cos_ref[rows, :] + sw * sin_ref[rows, :]

    def rope_body(i, _):
        rows = pl.ds(pl.multiple_of(i * rt, rt), rt)
        qr[rows, :] = rope(q_ref[rows, :].astype(F32), rows).astype(BF16)
        kr[rows, :] = (rope(k_ref[rows, :].astype(F32), rows) * k_scale).astype(BF16)
        return 0

    lax.fori_loop(0, t_len // rt, rope_body, 0)
    ctx_rows = pl.ds(t_len, c_len)
    qr[ctx_rows, :] = q_ref[ctx_rows, :]
    kr[ctx_rows, :] = (k_ref[ctx_rows, :].astype(F32) * k_scale).astype(BF16)

    ii = lax.broadcasted_iota(jnp.int32, (L, L), 0)
    jj = lax.broadcasted_iota(jnp.int32, (L, L), 1)
    rel = (ii - jj).astype(F32)
    idx = lax.broadcasted_iota(jnp.int32, (L, 1), 0).astype(F32)

    lg_f, lg_b = lgf_ref[head], lgb_ref[head]
    fwd = (jnp.where(rel >= 0, jnp.exp(lg_f * jnp.maximum(rel, 0.0)), 0.0),
           jnp.exp(lg_f * (idx + 1.0)),
           jnp.exp(lg_f * (L - 1.0 - idx)),
           jnp.exp(lg_f * L), state_f, oacc_f)
    bwd = (jnp.where(rel <= 0, jnp.exp(lg_b * jnp.maximum(-rel, 0.0)), 0.0),
           jnp.exp(lg_b * (L - idx)),
           jnp.exp(lg_b * idx),
           jnp.exp(lg_b * L), state_b, oacc_b)

    def chunk(row0, direction):
        dmat, q_decay, k_decay, chunk_decay, state, oacc = direction
        rows = pl.ds(pl.multiple_of(row0, L), L)
        qb, kb, vb = qr[rows, :], kr[rows, :], v_ref[rows, :]
        scores = _dot_nt(qb, kb) * dmat
        inner = _dot(scores.astype(BF16), vb)
        st = state[...]
        cross = _dot(qb, st.astype(BF16)) * q_decay
        oacc[rows, :] = inner + cross
        kd = (kb.astype(F32) * k_decay).astype(BF16)
        state[...] = st * chunk_decay + _dot_tn(kd, vb)

    def scan(r0, n_chunks):
        def body(ci, _):
            chunk(r0 + ci * L, fwd)
            chunk(r0 + (n_chunks - 1 - ci) * L, bwd)
            return 0
        lax.fori_loop(0, n_chunks, body, 0)

    state_f[...] = jnp.zeros_like(state_f)
    state_b[...] = jnp.zeros_like(state_b)
    scan(t_len, c_len // L)
    scan(0, t_len // L)

    def out_body(i, _):
        rows = pl.ds(pl.multiple_of(i * rt, rt), rt)
        o = _rms(oacc_f[rows, :] + oacc_b[rows, :])
        o_ref[rows, :] = (_silu(g_ref[rows, :].astype(F32)) * o).astype(BF16)
        return 0

    lax.fori_loop(0, s_len // rt, out_body, 0)


def retention(qkvg, lg_f, lg_b, cos, sin, *, batch, t_len, c_len, heads):
    s_len = t_len + c_len
    dk = cos.shape[-1]
    dv = 2 * dk
    assert dk == 2 * LANES
    kern = functools.partial(_retention_kernel, t_len=t_len, c_len=c_len)
    grid_spec = pltpu.PrefetchScalarGridSpec(
        num_scalar_prefetch=2, grid=(batch, heads),
        in_specs=[pl.BlockSpec((s_len, dk), lambda b, h, *_: (b, h)),
                  pl.BlockSpec((s_len, dk), lambda b, h, *_: (b, heads + h)),
                  pl.BlockSpec((s_len, dv), lambda b, h, *_: (b, heads + h)),
                  pl.BlockSpec((s_len, dv), lambda b, h, *_: (b, 2 * heads + h)),
                  pl.BlockSpec((t_len, dk), lambda b, h, *_: (0, 0)),
                  pl.BlockSpec((t_len, dk), lambda b, h, *_: (0, 0))],
        out_specs=pl.BlockSpec((s_len, dv), lambda b, h, *_: (b, h)),
        scratch_shapes=[pltpu.VMEM((s_len, dk), BF16), pltpu.VMEM((s_len, dk), BF16),
                        pltpu.VMEM((s_len, dv), F32), pltpu.VMEM((s_len, dv), F32),
                        pltpu.VMEM((dk, dv), F32), pltpu.VMEM((dk, dv), F32)])
    return pl.pallas_call(
        kern, grid_spec=grid_spec,
        out_shape=jax.ShapeDtypeStruct((batch * s_len, heads * dv), BF16),
        compiler_params=_params(("parallel", "parallel")), name="retention",
    )(lg_f, lg_b, qkvg, qkvg, qkvg, qkvg, cos, sin)


def _diff_attn_kernel(lam_ref, *refs, n_q, t_len, c_len, lambda_init):
    q_refs = refs[:n_q]
    k_ref, v_ref, cosq_ref, sinq_ref, cosk_ref, sink_ref, subln_ref, o_ref, kr = refs[n_q:]
    hd = DIFF_HEAD_DIM
    quarter = hd // 4
    rt = ROW_TILE

    def rope(x, cos, sin):
        return x * cos + _swap_quarters(x, quarter) * sin

    @pl.when(pl.program_id(2) == 0)
    def _():
        def body(i, _):
            rows = pl.ds(pl.multiple_of(i * rt, rt), rt)
            for c in range(2):
                cols = slice(c * hd, (c + 1) * hd)
                kr[rows, cols] = rope(k_ref[rows, cols].astype(F32), cosk_ref[rows, :],
                                      sink_ref[rows, :]).astype(BF16)
            return 0
        lax.fori_loop(0, t_len // rt, body, 0)
        ctx_rows = pl.ds(t_len, c_len)
        kr[ctx_rows, :] = k_ref[ctx_rows, :]

    lam_v = lam_ref[...]
    lam = (jnp.exp(jnp.sum(lam_v[0:1] * lam_v[1:2], axis=-1, keepdims=True))
           - jnp.exp(jnp.sum(lam_v[2:3] * lam_v[3:4], axis=-1, keepdims=True)) + lambda_init)
    q_scale = (hd ** -0.5) * math.log2(math.e)
    tq = q_refs[0].shape[0]
    for part, q_ref in enumerate(q_refs):
        rows = slice(part * tq, (part + 1) * tq)
        outs = []
        for c in range(2):
            cols = slice(c * hd, (c + 1) * hd)
            qc = (rope(q_ref[:, cols].astype(F32), cosq_ref[rows, :], sinq_ref[rows, :])
                  * q_scale).astype(BF16)
            s = _dot_nt(qc, kr[:, cols])
            e = jnp.exp2(s - jnp.max(s, axis=-1, keepdims=True))
            denom = jnp.sum(e, axis=-1, keepdims=True)
            outs.append(_dot(e.astype(BF16), v_ref[...]) / denom)
        o = outs[0] - lam * outs[1]
        o_ref[rows, :] = ((_rms(o) * subln_ref[...]) * (1.0 - lambda_init)).astype(BF16)


def diff_attention(qkv, lam_vecs, subln_w, cos, sin, *, batch, t_len, c_len, heads, lambda_init):
    s_len = t_len + c_len
    hd = DIFF_HEAD_DIM
    tq = ROW_TILE
    n_q = _pick(t_len // tq, (ATTN_Q_TILES, 2, 1))
    steps, s_tiles = t_len // (n_q * tq), s_len // tq
    kern = functools.partial(_diff_attn_kernel, n_q=n_q, t_len=t_len, c_len=c_len, lambda_init=lambda_init)
    q_specs = [pl.BlockSpec((tq, 2 * hd), lambda b, h, i, j=j: (b * s_tiles + n_q * i + j, h))
               for j in range(n_q)]
    return pl.pallas_call(
        kern, grid=(batch, heads, steps),
        in_specs=[pl.BlockSpec((4, hd), lambda b, h, i: (0, 0)), *q_specs,
                  pl.BlockSpec((s_len, 2 * hd), lambda b, h, i: (b, heads + h)),
                  pl.BlockSpec((s_len, 2 * hd), lambda b, h, i: (b, 2 * heads + h)),
                  pl.BlockSpec((n_q * tq, hd), lambda b, h, i: (i, 0)),
                  pl.BlockSpec((n_q * tq, hd), lambda b, h, i: (i, 0)),
                  pl.BlockSpec((t_len, hd), lambda b, h, i: (0, 0)),
                  pl.BlockSpec((t_len, hd), lambda b, h, i: (0, 0)),
                  pl.BlockSpec((1, 2 * hd), lambda b, h, i: (0, 0))],
        out_specs=pl.BlockSpec((n_q * tq, 2 * hd), lambda b, h, i: (b * steps + i, h)),
        out_shape=jax.ShapeDtypeStruct((batch * t_len, heads * 2 * hd), BF16),
        scratch_shapes=[pltpu.VMEM((s_len, 2 * hd), BF16)],
        compiler_params=_params(("parallel", "parallel", "arbitrary")), name="diff_attention",
    )(lam_vecs, *([qkv] * (n_q + 2)), cos, sin, cos, sin, subln_w.reshape(1, 2 * hd))


def _shared_kernel(x_ref, wg_ref, wu_ref, wd_ref, o_ref, acc_ref):
    f = pl.program_id(1)

    @pl.when(f == 0)
    def _():
        acc_ref[...] = jnp.zeros_like(acc_ref)

    x = x_ref[...]
    hid = _silu(_dot(x, wg_ref[...])) * _dot(x, wu_ref[...])
    acc_ref[...] += _dot(hid.astype(BF16), wd_ref[...])

    @pl.when(f == pl.num_programs(1) - 1)
    def _():
        o_ref[...] = acc_ref[...].astype(o_ref.dtype)


def shared_expert(h, w_gate, w_up, w_down):
    m, d = h.shape
    f_dim = w_gate.shape[1]
    tm = _pick(m, (512, 256))
    tf = _pick(f_dim, (256, 128))
    return pl.pallas_call(
        _shared_kernel, grid=(m // tm, f_dim // tf),
        in_specs=[pl.BlockSpec((tm, d), lambda i, f: (i, 0)),
                  pl.BlockSpec((d, tf), lambda i, f: (0, f)),
                  pl.BlockSpec((d, tf), lambda i, f: (0, f)),
                  pl.BlockSpec((tf, d), lambda i, f: (f, 0))],
        out_specs=pl.BlockSpec((tm, d), lambda i, f: (i, 0)),
        out_shape=jax.ShapeDtypeStruct((m, d), BF16),
        scratch_shapes=[pltpu.VMEM((tm, d), F32)],
        compiler_params=_params(("parallel", "arbitrary")), name="shared_expert",
    )(h, w_gate, w_up, w_down)


def _dispatch_plan(cnt, pairs, tm):
    n_exp = cnt.shape[1]
    total = jnp.sum(cnt, axis=0)
    padded = (total + tm - 1) // tm * tm
    pend = jnp.cumsum(padded)
    base = (pend - padded)[None, :] + jnp.cumsum(cnt, axis=0) - cnt
    n_tiles = pairs // tm + n_exp
    tile_start = jnp.arange(n_tiles, dtype=jnp.int32) * tm
    valid = tile_start < pend[-1]
    tile_e = jnp.minimum(jnp.sum((tile_start[:, None] >= pend[None, :]).astype(jnp.int32), axis=1), n_exp - 1)
    tile_e = jnp.where(valid, tile_e, jnp.max(jnp.where(valid, tile_e, 0)))
    next_e = jnp.concatenate([tile_e[1:], jnp.full((1,), -1, jnp.int32)])
    next_valid = jnp.concatenate([valid[1:], jnp.zeros((1,), bool)])
    zero_fill = (~valid) | (tile_e != next_e) | (~next_valid)
    return (tile_e.astype(jnp.int32), valid.astype(jnp.int32), zero_fill.astype(jnp.int32),
            base.astype(jnp.int32))


def _positions_kernel(idx_ref, rank_ref, base_ref, pos_ref):
    n_exp = base_ref.shape[0]
    tm = idx_ref.shape[1]
    eiota = lax.broadcasted_iota(jnp.int32, (n_exp, tm), 0)
    base = jnp.broadcast_to(base_ref[...].astype(F32), (n_exp, tm))
    for k in range(MOE_TOPK):
        hit = eiota == idx_ref[k:k + 1, :]
        first = jnp.sum(jnp.where(hit, base, 0.0), axis=0, keepdims=True)
        pos_ref[k:k + 1, :] = first.astype(jnp.int32) + rank_ref[k:k + 1, :]


def pair_positions(idx_t, rank_t, base):
    k, n = idx_t.shape
    tiles, n_exp = base.shape
    tm = n // tiles
    spec = pl.BlockSpec((k, tm), lambda i: (0, i))
    return pl.pallas_call(
        _positions_kernel, grid=(tiles,),
        in_specs=[spec, spec, pl.BlockSpec((None, n_exp, 1), lambda i: (i, 0, 0))],
        out_specs=spec, out_shape=jax.ShapeDtypeStruct((k, n), jnp.int32),
        compiler_params=_params(("parallel",)), name="pair_positions",
    )(idx_t, rank_t, base.reshape(tiles, n_exp, 1))


def _dispatch_kernel(pos_ref, zf_ref, hp_ref, xs_hbm, zeros, sem_z, sem_s, *, n_tok, n_tiles):
    i = pl.program_id(0)
    td = hp_ref.shape[0]
    tm = zeros.shape[0]

    def zero_copy(j):
        return pltpu.make_async_copy(zeros, xs_hbm.at[pl.ds(pl.multiple_of(j * tm, tm), tm)], sem_z)

    @pl.when(i == 0)
    def _():
        zeros[...] = jnp.zeros_like(zeros)

        def start(j, _):
            @pl.when(zf_ref[j] == 1)
            def _():
                zero_copy(j).start()
            return 0

        def wait(j, _):
            @pl.when(zf_ref[j] == 1)
            def _():
                zero_copy(j).wait()
            return 0

        lax.fori_loop(0, n_tiles, start, 0)
        lax.fori_loop(0, n_tiles, wait, 0)

    def body(t, _):
        for k in range(MOE_TOPK):
            row = pos_ref[k * n_tok + i * td + t]
            pltpu.make_async_copy(hp_ref.at[pl.ds(t, 1)], xs_hbm.at[pl.ds(row, 1)], sem_s).start()
        return 0

    lax.fori_loop(0, td, body, 0, unroll=2)
    for k in range(MOE_TOPK):
        pltpu.make_async_copy(hp_ref, xs_hbm.at[pl.ds(0, td)], sem_s).wait()


def dispatch(hp, pos, zero_fill, n_tiles):
    n_tok, half = hp.shape
    tm = EXPERT_TILE
    grid_spec = pltpu.PrefetchScalarGridSpec(
        num_scalar_prefetch=2, grid=(n_tok // DISPATCH_TILE,),
        in_specs=[pl.BlockSpec((DISPATCH_TILE, half), lambda i, p, z: (i, 0))],
        out_specs=pl.BlockSpec(memory_space=pl.ANY),
        scratch_shapes=[pltpu.VMEM((tm, half), jnp.uint32), pltpu.SemaphoreType.DMA(()),
                        pltpu.SemaphoreType.DMA(())])
    return pl.pallas_call(
        functools.partial(_dispatch_kernel, n_tok=n_tok, n_tiles=n_tiles), grid_spec=grid_spec,
        out_shape=jax.ShapeDtypeStruct((n_tiles * tm, half), jnp.uint32),
        compiler_params=_params(("arbitrary",)), name="moe_dispatch",
    )(pos, zero_fill, hp)


def _expert_kernel(te_ref, tv_ref, x_ref, wg_ref, wu_ref, wd_ref, o_ref, wgb, wub, wdb):
    i = pl.program_id(0)
    prev = jnp.maximum(i - 1, 0)

    @pl.when((i == 0) | (te_ref[i] != te_ref[prev]))
    def _():
        wgb[...] = wg_ref[...].astype(BF16)
        wub[...] = wu_ref[...].astype(BF16)
        wdb[...] = wd_ref[...].astype(BF16)

    @pl.when(tv_ref[i] == 1)
    def _():
        lo, hi = _unpack_halves(x_ref[...])
        x = jnp.concatenate([lo.astype(BF16), hi.astype(BF16)], axis=1)
        hid = _silu(_dot(x, wgb[...])) * _dot(x, wub[...])
        o_ref[...] = _pack_halves(_dot(hid.astype(BF16), wdb[...]))

    @pl.when(tv_ref[i] == 0)
    def _():
        o_ref[...] = jnp.zeros_like(o_ref)


def routed_experts(xs, tile_e, tile_valid, w_gate, w_up, w_down, layer):
    tm = EXPERT_TILE
    n_tiles = tile_e.shape[0]
    _, n_exp, d, f = w_gate.shape
    wmap = lambda i, te, tv: (layer, te[i], 0, 0)
    grid_spec = pltpu.PrefetchScalarGridSpec(
        num_scalar_prefetch=2, grid=(n_tiles,),
        in_specs=[pl.BlockSpec((tm, d // 2), lambda i, te, tv: (i, 0)),
                  pl.BlockSpec((None, None, d, f), wmap),
                  pl.BlockSpec((None, None, d, f), wmap),
                  pl.BlockSpec((None, None, f, d), wmap)],
        out_specs=pl.BlockSpec((tm, d // 2), lambda i, te, tv: (i, 0)),
        scratch_shapes=[pltpu.VMEM((d, f), BF16), pltpu.VMEM((d, f), BF16), pltpu.VMEM((f, d), BF16)])
    return pl.pallas_call(
        _expert_kernel, grid_spec=grid_spec,
        out_shape=jax.ShapeDtypeStruct((n_tiles * tm, d // 2), jnp.uint32),
        compiler_params=_params(("arbitrary",)), name="routed_experts",
    )(tile_e, tile_valid, xs, w_gate, w_up, w_down)


def _combine_kernel(pos_ref, ys_hbm, sh_ref, w_ref, o_ref, buf, sems, *, n_tok):
    i = pl.program_id(0)
    n = pl.num_programs(0)
    tc = buf.shape[2]
    slot = i % 2

    def gather_rows(tile, dst_slot):
        def body(t, _):
            for k in range(MOE_TOPK):
                p = pos_ref[k * n_tok + tile * tc + t]
                pltpu.make_async_copy(ys_hbm.at[pl.ds(p, 1)], buf.at[dst_slot, k, pl.ds(t, 1)],
                                      sems.at[dst_slot]).start()
            return 0
        lax.fori_loop(0, tc, body, 0, unroll=2)

    @pl.when(i == 0)
    def _():
        gather_rows(0, 0)

    @pl.when(i + 1 < n)
    def _():
        gather_rows(i + 1, 1 - slot)

    for k in range(MOE_TOPK):
        pltpu.make_async_copy(ys_hbm.at[pl.ds(0, tc)], buf.at[slot, k], sems.at[slot]).wait()
    half = sh_ref.shape[-1] // 2
    sh = sh_ref[...].astype(F32)
    w = w_ref[...]
    lo_acc, hi_acc = sh[:, :half], sh[:, half:]
    for k in range(MOE_TOPK):
        lo, hi = _unpack_halves(buf[slot, k])
        lo_acc = lo_acc + w[:, k:k + 1] * lo
        hi_acc = hi_acc + w[:, k:k + 1] * hi
    o_ref[...] = jnp.concatenate([lo_acc, hi_acc], axis=1).astype(o_ref.dtype)


def combine(ys, pos, shared, wgt):
    n_tok, d = shared.shape
    tc = COMBINE_TILE
    grid_spec = pltpu.PrefetchScalarGridSpec(
        num_scalar_prefetch=1, grid=(n_tok // tc,),
        in_specs=[pl.BlockSpec(memory_space=pl.ANY),
                  pl.BlockSpec((tc, d), lambda i, p: (i, 0)),
                  pl.BlockSpec((tc, MOE_TOPK), lambda i, p: (i, 0))],
        out_specs=pl.BlockSpec((tc, d), lambda i, p: (i, 0)),
        scratch_shapes=[pltpu.VMEM((2, MOE_TOPK, tc, d // 2), jnp.uint32), pltpu.SemaphoreType.DMA((2,))])
    return pl.pallas_call(
        functools.partial(_combine_kernel, n_tok=n_tok), grid_spec=grid_spec,
        out_shape=jax.ShapeDtypeStruct((n_tok, d), BF16),
        compiler_params=_params(("arbitrary",)), name="moe_combine",
    )(pos, ys, shared, wgt)


def moe_ffn(h, hp, route, w_gate, w_up, w_down, sh_gate, sh_up, sh_down, layer):
    idx_t, wgt_t, rank_t, cnt = route
    pairs = idx_t.shape[0] * idx_t.shape[1]
    tile_e, tile_valid, zero_fill, base = _dispatch_plan(cnt[:, :, 0], pairs, EXPERT_TILE)
    pos = pair_positions(idx_t, rank_t, base).reshape(pairs)
    xs = dispatch(hp, pos, zero_fill, tile_e.shape[0])
    ys = routed_experts(xs, tile_e, tile_valid, w_gate, w_up, w_down, layer)
    shared = shared_expert(h, sh_gate[layer].astype(BF16), sh_up[layer].astype(BF16),
                           sh_down[layer].astype(BF16))
    return combine(ys, pos, shared, wgt_t.T)


def kernel(x, c, ctx, c_ctx, ada_w, ada_b, norm_pre_mix, norm_post_mix, norm_pre_ffn, norm_post_ffn, ret_w_in, ret_w_out, ret_decay_fwd, ret_decay_bwd, diff_w_in, diff_w_out, diff_lam_q1, diff_lam_k1, diff_lam_q2, diff_lam_k2, diff_subln_w, moe_router_w, moe_router_b, moe_w_gate, moe_w_up, moe_w_down, moe_shared_gate, moe_shared_up, moe_shared_down):
    batch, t_len, d = x.shape
    c_len = ctx.shape[1]
    s_len = t_len + c_len
    depth = ada_w.shape[0]
    assert depth == 2 and batch + 1 <= 8
    assert t_len % ROW_TILE == 0 and c_len % ROW_TILE == 0 and t_len % GRID_W == 0
    ret_heads = ret_decay_fwd.shape[-1]
    diff_heads = d // (2 * DIFF_HEAD_DIM)
    lat_tiles, all_tiles = t_len // ROW_TILE, s_len // ROW_TILE

    cc = jnp.concatenate([c, c_ctx[None], jnp.zeros((8 - batch - 1, d), F32)], axis=0)
    mods = ada_modulation(cc, ada_w, ada_b)
    xs = jnp.concatenate([x, ctx], axis=1).reshape(batch * s_len, d)
    rope_ret = _rope_tables(t_len, d // ret_heads)
    rope_diff = _rope_tables(t_len, DIFF_HEAD_DIM)

    ident = lambda i: i
    uni_mod = lambda i: jnp.where(i % all_tiles < lat_tiles, i // all_tiles, batch)
    lat_mod = lambda i: i // lat_tiles
    lat_of_uni = lambda i: (i // lat_tiles) * all_tiles + i % lat_tiles

    (h,) = fused_norm(xs, ident, batch * all_tiles, uni_mod, mod_b=mods[0], w_pre=norm_pre_mix[0],
                      shift_idx=0, scale_idx=1)
    qkvg = matmul(h, ret_w_in, 0)
    lg_f = jax.nn.log_sigmoid(ret_decay_fwd[0].astype(F32))
    lg_b = jax.nn.log_sigmoid(ret_decay_bwd[0].astype(F32))
    r = retention(qkvg, lg_f, lg_b, *rope_ret, batch=batch, t_len=t_len, c_len=c_len, heads=ret_heads)
    y = matmul(r, ret_w_out, 0)
    xs, h, hp, *route = fused_norm(
        xs, ident, batch * all_tiles, uni_mod, y=y, mod_a=mods[0], w_post=norm_post_mix[0], gate_idx=2,
        mod_b=mods[0], w_pre=norm_pre_ffn[0], shift_idx=3, scale_idx=4,
        router_wt=moe_router_w[0].T, router_b=moe_router_b[0])
    f = moe_ffn(h, hp, route, moe_w_gate, moe_w_up, moe_w_down,
                moe_shared_gate, moe_shared_up, moe_shared_down, 0)
    xs, h = fused_norm(xs, ident, batch * all_tiles, uni_mod, y=f, mod_a=mods[0], w_post=norm_post_ffn[0],
                       gate_idx=5, mod_b=mods[1], w_pre=norm_pre_mix[1], shift_idx=0, scale_idx=1)

    qkv = matmul(h, diff_w_in, 0)
    lam_vecs = jnp.stack([diff_lam_q1[0], diff_lam_k1[0], diff_lam_q2[0], diff_lam_k2[0]]).astype(F32)
    lambda_init = 0.8 - 0.6 * math.exp(-0.3 * 1)
    a = diff_attention(qkv, lam_vecs, diff_subln_w[0], *rope_diff, batch=batch, t_len=t_len, c_len=c_len,
                       heads=diff_heads, lambda_init=lambda_init)
    y = matmul(a, diff_w_out, 0)
    xl, h, hp, *route = fused_norm(
        xs, lat_of_uni, batch * lat_tiles, lat_mod, y=y, mod_a=mods[1], w_post=norm_post_mix[1], gate_idx=2,
        mod_b=mods[1], w_pre=norm_pre_ffn[1], shift_idx=3, scale_idx=4,
        router_wt=moe_router_w[1].T, router_b=moe_router_b[1])
    f = moe_ffn(h, hp, route, moe_w_gate, moe_w_up, moe_w_down,
                moe_shared_gate, moe_shared_up, moe_shared_down, 1)
    (out,) = fused_norm(xl, ident, batch * lat_tiles, lat_mod, y=f, mod_a=mods[1], w_post=norm_post_ffn[1],
                        gate_idx=5)
    return out.reshape(batch, t_len, d)
```

```python
import functools
import math

import jax
import jax.numpy as jnp
from jax import lax
from jax.experimental import pallas as pl
from jax.experimental.pallas import tpu as pltpu

GRID_W = 64
N_ADA = 6
NORM_EPS = 1e-6
ROPE_BASE = 10000.0
RET_BLOCK = 256
DIFF_HEAD_DIM = 128
MOE_TOPK = 8
MOE_GROUPS = 8
MOE_TOPK_GROUPS = 4
ROUTED_SCALE = 2.5

LANES = 128
ROW_TILE = 256
EXPERT_TILE = 256
COMBINE_TILE = 64
DISPATCH_TILE = 256
ATTN_Q_TILES = 4
VMEM_LIMIT = 56 * 1024 * 1024

F32 = jnp.float32
BF16 = jnp.bfloat16


def _pick(dim, candidates):
    for c in candidates:
        if dim % c == 0:
            return c
    raise ValueError(f"no tile in {candidates} divides {dim}")


def _params(sem, vmem=VMEM_LIMIT):
    return pltpu.CompilerParams(dimension_semantics=sem, vmem_limit_bytes=vmem)


def _dot(a, b):
    return jnp.dot(a, b, preferred_element_type=F32)


def _dot_nt(a, b):
    return lax.dot_general(a, b, (((1,), (1,)), ((), ())), preferred_element_type=F32)


def _dot_tn(a, b):
    return lax.dot_general(a, b, (((0,), (0,)), ((), ())), preferred_element_type=F32)


def _silu(x):
    return x * jax.nn.sigmoid(x)


def _pack_halves(y):
    w = y.shape[-1] // 2
    lo = lax.bitcast_convert_type(y[:, :w].astype(BF16).astype(F32), jnp.uint32)
    hi = lax.bitcast_convert_type(y[:, w:].astype(BF16).astype(F32), jnp.uint32)
    return (hi & jnp.uint32(0xFFFF0000)) | (lo >> 16)


def _unpack_halves(p):
    lo = lax.bitcast_convert_type(p << 16, F32)
    hi = lax.bitcast_convert_type(p & jnp.uint32(0xFFFF0000), F32)
    return lo, hi


def _ada_kernel(c_ref, w_ref, b_ref, o_ref):
    a = _silu(c_ref[...]).astype(BF16)
    o_ref[...] = _dot(a, w_ref[...].astype(BF16)) + b_ref[...]


def ada_modulation(cc, ada_w, ada_b):
    depth, d, n = ada_w.shape
    tn = _pick(n, (512, 256, 128))
    out = pl.pallas_call(
        _ada_kernel,
        grid=(depth, n // tn),
        in_specs=[pl.BlockSpec((8, d), lambda l, j: (0, 0)),
                  pl.BlockSpec((None, d, tn), lambda l, j: (l, 0, j)),
                  pl.BlockSpec((None, 1, tn), lambda l, j: (l, 0, j))],
        out_specs=pl.BlockSpec((None, 8, tn), lambda l, j: (l, 0, j)),
        out_shape=jax.ShapeDtypeStruct((depth, 8, n), F32),
        compiler_params=_params(("parallel", "parallel")),
        name="ada_modulation",
    )(cc, ada_w, ada_b.reshape(depth, 1, n))
    return out.reshape(depth, 8, N_ADA, d)


def _rms(x):
    return x * lax.rsqrt(jnp.mean(x * x, axis=-1, keepdims=True) + NORM_EPS)


def _route(h, rw_ref, rb_ref, idx_ref, wgt_ref, rank_ref, cnt_ref):
    n_exp = rw_ref.shape[0]
    tm = h.shape[0]
    per_group = n_exp // MOE_GROUPS
    w = rw_ref[...]
    w_hi = w.astype(BF16)
    w_lo = (w - w_hi.astype(F32)).astype(BF16)
    h_hi = h.astype(BF16)
    h_lo = (h - h_hi.astype(F32)).astype(BF16)
    logits = _dot_nt(w_hi, h_hi) + (_dot_nt(w_hi, h_lo) + _dot_nt(w_lo, h_hi))
    scores = jax.nn.sigmoid(logits)
    biased = scores + rb_ref[...]
    neg = jnp.float32(-jnp.inf)
    sub = lax.broadcasted_iota(jnp.int32, (per_group, tm), 0)
    giota = lax.broadcasted_iota(jnp.int32, (MOE_GROUPS, tm), 0)
    gs = jnp.zeros((MOE_GROUPS, tm), F32)
    for g in range(MOE_GROUPS):
        blk = biased[g * per_group:(g + 1) * per_group]
        m1 = jnp.max(blk, axis=0, keepdims=True)
        i1 = jnp.min(jnp.where(blk == m1, sub, per_group), axis=0, keepdims=True)
        m2 = jnp.max(jnp.where(sub == i1, neg, blk), axis=0, keepdims=True)
        gs = jnp.where(giota == g, m1 + m2, gs)
    rank = jnp.zeros((MOE_GROUPS, tm), jnp.int32)
    for j in range(MOE_GROUPS):
        gj = gs[j:j + 1]
        beats = (gj > gs) | ((gj == gs) & (giota > j))
        rank = rank + beats.astype(jnp.int32)
    keep = (rank < MOE_TOPK_GROUPS).astype(F32)
    keep_e = jnp.concatenate(
        [jnp.broadcast_to(keep[g:g + 1], (per_group, tm)) for g in range(MOE_GROUPS)], axis=0)
    masked = jnp.where(keep_e > 0.5, biased, neg)
    eiota = lax.broadcasted_iota(jnp.int32, (n_exp, tm), 0)
    sel_w, hits = [], []
    for k in range(MOE_TOPK):
        m = jnp.max(masked, axis=0, keepdims=True)
        idx = jnp.min(jnp.where(masked == m, eiota, n_exp), axis=0, keepdims=True)
        hit = eiota == idx
        hits.append(hit)
        sel_w.append(jnp.sum(jnp.where(hit, scores, 0.0), axis=0, keepdims=True))
        masked = jnp.where(hit, neg, masked)
        idx_ref[k:k + 1, :] = idx
    total = sel_w[0]
    for k in range(1, MOE_TOPK):
        total = total + sel_w[k]
    for k in range(MOE_TOPK):
        wgt_ref[k:k + 1, :] = sel_w[k] / total * ROUTED_SCALE
    chosen = jnp.zeros((n_exp, tm), F32)
    for k in range(MOE_TOPK):
        chosen = jnp.where(hits[k], 1.0, chosen)
    before = (lax.broadcasted_iota(jnp.int32, (tm, tm), 0)
              < lax.broadcasted_iota(jnp.int32, (tm, tm), 1)).astype(BF16)
    prefix = _dot(chosen.astype(BF16), before)
    for k in range(MOE_TOPK):
        rank_ref[k:k + 1, :] = jnp.sum(jnp.where(hits[k], prefix, 0.0), axis=0,
                                       keepdims=True).astype(jnp.int32)
    cnt_ref[...] = jnp.broadcast_to(jnp.sum(chosen, axis=1, keepdims=True),
                                    cnt_ref.shape).astype(jnp.int32)


def _fused_norm_kernel(*refs, has_resid, has_prenorm, has_router, gate_idx, shift_idx, scale_idx):
    refs = list(refs)
    x_ref = refs.pop(0)
    if has_resid:
        y_ref, mod_a_ref, wpost_ref = refs.pop(0), refs.pop(0), refs.pop(0)
    if has_prenorm:
        mod_b_ref, wpre_ref = refs.pop(0), refs.pop(0)
    if has_router:
        rw_ref, rb_ref = refs.pop(0), refs.pop(0)
    x = x_ref[...]
    if has_resid:
        xo_ref = refs.pop(0)
        y = y_ref[...].astype(F32)
        x = x + mod_a_ref[gate_idx:gate_idx + 1, :] * (_rms(y) * wpost_ref[...])
        xo_ref[...] = x
    if has_prenorm:
        h_ref = refs.pop(0)
        h = (_rms(x) * wpre_ref[...]) * (1.0 + mod_b_ref[scale_idx:scale_idx + 1, :]) \
            + mod_b_ref[shift_idx:shift_idx + 1, :]
        h_ref[...] = h.astype(BF16)
        if has_router:
            hp_ref, idx_ref, wgt_ref, rank_ref, cnt_ref = (refs.pop(0) for _ in range(5))
            hp_ref[...] = _pack_halves(h)
            _route(h, rw_ref, rb_ref, idx_ref, wgt_ref, rank_ref, cnt_ref)


def fused_norm(x, x_tile_map, n_out_tiles, mod_row_map, *, y=None, mod_a=None, w_post=None, gate_idx=0,
               mod_b=None, w_pre=None, shift_idx=0, scale_idx=0, router_wt=None, router_b=None):
    d = x.shape[-1]
    tm = ROW_TILE
    has_resid, has_prenorm, has_router = y is not None, mod_b is not None, router_wt is not None
    row = lambda i: (i, 0)
    const = lambda i: (0, 0)
    mod_spec = pl.BlockSpec((None, N_ADA, d), lambda i: (mod_row_map(i), 0, 0))
    vec_spec = pl.BlockSpec((1, d), const)
    args, in_specs = [x], [pl.BlockSpec((tm, d), lambda i: (x_tile_map(i), 0))]
    out_shape, out_specs = [], []
    n_rows = n_out_tiles * tm
    if has_resid:
        args += [y, mod_a, w_post.reshape(1, d)]
        in_specs += [pl.BlockSpec((tm, d), row), mod_spec, vec_spec]
        out_shape.append(jax.ShapeDtypeStruct((n_rows, d), F32))
        out_specs.append(pl.BlockSpec((tm, d), row))
    if has_prenorm:
        args += [mod_b, w_pre.reshape(1, d)]
        in_specs += [mod_spec, vec_spec]
        out_shape.append(jax.ShapeDtypeStruct((n_rows, d), BF16))
        out_specs.append(pl.BlockSpec((tm, d), row))
    if has_router:
        n_exp = router_wt.shape[0]
        args += [router_wt, router_b.reshape(n_exp, 1)]
        in_specs += [pl.BlockSpec((n_exp, d), const), pl.BlockSpec((n_exp, 1), const)]
        out_shape += [jax.ShapeDtypeStruct((n_rows, d // 2), jnp.uint32),
                      jax.ShapeDtypeStruct((MOE_TOPK, n_rows), jnp.int32),
                      jax.ShapeDtypeStruct((MOE_TOPK, n_rows), F32),
                      jax.ShapeDtypeStruct((MOE_TOPK, n_rows), jnp.int32),
                      jax.ShapeDtypeStruct((n_out_tiles, n_exp, LANES), jnp.int32)]
        out_specs += [pl.BlockSpec((tm, d // 2), row),
                      pl.BlockSpec((MOE_TOPK, tm), lambda i: (0, i)),
                      pl.BlockSpec((MOE_TOPK, tm), lambda i: (0, i)),
                      pl.BlockSpec((MOE_TOPK, tm), lambda i: (0, i)),
                      pl.BlockSpec((None, n_exp, LANES), lambda i: (i, 0, 0))]
    kern = functools.partial(_fused_norm_kernel, has_resid=has_resid, has_prenorm=has_prenorm,
                             has_router=has_router, gate_idx=gate_idx, shift_idx=shift_idx,
                             scale_idx=scale_idx)
    return pl.pallas_call(
        kern, grid=(n_out_tiles,), in_specs=in_specs, out_specs=out_specs, out_shape=out_shape,
        compiler_params=_params(("parallel",)), name="fused_norm",
    )(*args)


def _mm_kernel(a_ref, w_ref, o_ref):
    o_ref[...] = _dot(a_ref[...], w_ref[...].astype(BF16)).astype(o_ref.dtype)


def _mm_acc_kernel(a_ref, w_ref, o_ref, acc_ref):
    k = pl.program_id(2)

    @pl.when(k == 0)
    def _():
        acc_ref[...] = jnp.zeros_like(acc_ref)

    acc_ref[...] += _dot(a_ref[...], w_ref[...].astype(BF16))

    @pl.when(k == pl.num_programs(2) - 1)
    def _():
        o_ref[...] = acc_ref[...].astype(o_ref.dtype)


def matmul(a, w, layer, out_dtype=BF16):
    m, k = a.shape
    n = w.shape[2]
    tm = _pick(m, (1024, 768, 512, 256))
    tn = _pick(n, (512, 256, 128))
    tk = _pick(k, (4096, 2048, 1024, 512))
    if tk == k:
        return pl.pallas_call(
            _mm_kernel, grid=(m // tm, n // tn),
            in_specs=[pl.BlockSpec((tm, k), lambda i, j: (i, 0)),
                      pl.BlockSpec((None, k, tn), lambda i, j: (layer, 0, j))],
            out_specs=pl.BlockSpec((tm, tn), lambda i, j: (i, j)),
            out_shape=jax.ShapeDtypeStruct((m, n), out_dtype),
            compiler_params=_params(("parallel", "parallel")), name="matmul",
        )(a, w)
    return pl.pallas_call(
        _mm_acc_kernel, grid=(m // tm, n // tn, k // tk),
        in_specs=[pl.BlockSpec((tm, tk), lambda i, j, l: (i, l)),
                  pl.BlockSpec((None, tk, tn), lambda i, j, l: (layer, l, j))],
        out_specs=pl.BlockSpec((tm, tn), lambda i, j, l: (i, j)),
        out_shape=jax.ShapeDtypeStruct((m, n), out_dtype),
        scratch_shapes=[pltpu.VMEM((tm, tn), F32)],
        compiler_params=_params(("parallel", "parallel", "arbitrary")), name="matmul_acc",
    )(a, w)


def _rope_tables(t_len, head_dim):
    rows = t_len // GRID_W
    n_freq = head_dim // 4
    row, col = jnp.meshgrid(jnp.arange(rows, dtype=F32), jnp.arange(GRID_W, dtype=F32), indexing="ij")
    inv_freq = ROPE_BASE ** (-jnp.arange(n_freq, dtype=F32) / n_freq)
    ang_r = row.reshape(-1, 1) * inv_freq
    ang_c = col.reshape(-1, 1) * inv_freq
    cr, sr, cc, sc = jnp.cos(ang_r), jnp.sin(ang_r), jnp.cos(ang_c), jnp.sin(ang_c)
    return (jnp.concatenate([cr, cr, cc, cc], axis=-1), jnp.concatenate([-sr, sr, -sc, sc], axis=-1))


def _swap_quarters(x, quarter):
    lane = lax.broadcasted_iota(jnp.int32, x.shape, 1)
    first = (lane % (2 * quarter)) < quarter
    return jnp.where(first, pltpu.roll(x, LANES - quarter, axis=1), pltpu.roll(x, quarter, axis=1))


def _retention_kernel(lgf_ref, lgb_ref, q_ref, k_ref, v_ref, g_ref, cos_ref, sin_ref, o_ref,
                      qr, kr, oacc_f, oacc_b, state_f, state_b, *, t_len, c_len):
    head = pl.program_id(1)
    L = RET_BLOCK
    s_len = t_len + c_len
    dk = q_ref.shape[-1]
    k_scale = dk ** -0.5
    rt = ROW_TILE

    def rope(x, rows):
        sw = jnp.concatenate([pltpu.roll(x[:, :LANES], LANES // 2, axis=1),
                              pltpu.roll(x[:, LANES:], LANES // 2, axis=1)], axis=1)
        return x * cos_ref[rows, :] + sw * sin_ref[rows, :]

    def rope_body(i, _):
        rows = pl.ds(pl.multiple_of(i * rt, rt), rt)
        qr[rows, :] = rope(q_ref[rows, :].astype(F32), rows).astype(BF16)
        kr[rows, :] = (rope(k_ref[rows, :].astype(F32), rows) * k_scale).astype(BF16)
        return 0

    lax.fori_loop(0, t_len // rt, rope_body, 0)
    ctx_rows = pl.ds(t_len, c_len)
    qr[ctx_rows, :] = q_ref[ctx_rows, :]
    kr[ctx_rows, :] = (k_ref[ctx_rows, :].astype(F32) * k_scale).astype(BF16)

    ii = lax.broadcasted_iota(jnp.int32, (L, L), 0)
    jj = lax.broadcasted_iota(jnp.int32, (L, L), 1)
    rel = (ii - jj).astype(F32)
    idx = lax.broadcasted_iota(jnp.int32, (L, 1), 0).astype(F32)

    lg_f, lg_b = lgf_ref[head], lgb_ref[head]
    fwd = (jnp.where(rel >= 0, jnp.exp(lg_f * jnp.maximum(rel, 0.0)), 0.0),
           jnp.exp(lg_f * (idx + 1.0)),
           jnp.exp(lg_f * (L - 1.0 - idx)),
           jnp.exp(lg_f * L), state_f, oacc_f)
    bwd = (jnp.where(rel <= 0, jnp.exp(lg_b * jnp.maximum(-rel, 0.0)), 0.0),
           jnp.exp(lg_b * (L - idx)),
           jnp.exp(lg_b * idx),
           jnp.exp(lg_b * L), state_b, oacc_b)

    def chunk(row0, direction):
        dmat, q_decay, k_decay, chunk_decay, state, oacc = direction
        rows = pl.ds(pl.multiple_of(row0, L), L)
        qb, kb, vb = qr[rows, :], kr[rows, :], v_ref[rows, :]
        scores = _dot_nt(qb, kb) * dmat
        inner = _dot(scores.astype(BF16), vb)
        st = state[...]
        cross = _dot(qb, st.astype(BF16)) * q_decay
        oacc[rows, :] = inner + cross
        kd = (kb.astype(F32) * k_decay).astype(BF16)
        state[...] = st * chunk_decay + _dot_tn(kd, vb)

    def scan(r0, n_chunks):
        def body(ci, _):
            chunk(r0 + ci * L, fwd)
            chunk(r0 + (n_chunks - 1 - ci) * L, bwd)
            return 0
        lax.fori_loop(0, n_chunks, body, 0)

    state_f[...] = jnp.zeros_like(state_f)
    state_b[...] = jnp.zeros_like(state_b)
    scan(t_len, c_len // L)
    scan(0, t_len // L)

    def out_body(i, _):
        rows = pl.ds(pl.multiple_of(i * rt, rt), rt)
        o = _rms(oacc_f[rows, :] + oacc_b[rows, :])
        o_ref[rows, :] = (_silu(g_ref[rows, :].astype(F32)) * o).astype(BF16)
        return 0

    lax.fori_loop(0, s_len // rt, out_body, 0)


def retention(qkvg, lg_f, lg_b, cos, sin, *, batch, t_len, c_len, heads):
    s_len = t_len + c_len
    dk = cos.shape[-1]
    dv = 2 * dk
    assert dk == 2 * LANES
    kern = functools.partial(_retention_kernel, t_len=t_len, c_len=c_len)
    grid_spec = pltpu.PrefetchScalarGridSpec(
        num_scalar_prefetch=2, grid=(batch, heads),
        in_specs=[pl.BlockSpec((s_len, dk), lambda b, h, *_: (b, h)),
                  pl.BlockSpec((s_len, dk), lambda b, h, *_: (b, heads + h)),
                  pl.BlockSpec((s_len, dv), lambda b, h, *_: (b, heads + h)),
                  pl.BlockSpec((s_len, dv), lambda b, h, *_: (b, 2 * heads + h)),
                  pl.BlockSpec((t_len, dk), lambda b, h, *_: (0, 0)),
                  pl.BlockSpec((t_len, dk), lambda b, h, *_: (0, 0))],
        out_specs=pl.BlockSpec((s_len, dv), lambda b, h, *_: (b, h)),
        scratch_shapes=[pltpu.VMEM((s_len, dk), BF16), pltpu.VMEM((s_len, dk), BF16),
                        pltpu.VMEM((s_len, dv), F32), pltpu.VMEM((s_len, dv), F32),
                        pltpu.VMEM((dk, dv), F32), pltpu.VMEM((dk, dv), F32)])
    return pl.pallas_call(
        kern, grid_spec=grid_spec,
        out_shape=jax.ShapeDtypeStruct((batch * s_len, heads * dv), BF16),
        compiler_params=_params(("parallel", "parallel")), name="retention",
    )(lg_f, lg_b, qkvg, qkvg, qkvg, qkvg, cos, sin)


def _diff_attn_kernel(lam_ref, *refs, n_q, t_len, c_len, lambda_init):
    q_refs = refs[:n_q]
    k_ref, v_ref, cosq_ref, sinq_ref, cosk_ref, sink_ref, subln_ref, o_ref, kr = refs[n_q:]
    hd = DIFF_HEAD_DIM
    quarter = hd // 4
    rt = ROW_TILE

    def rope(x, cos, sin):
        return x * cos + _swap_quarters(x, quarter) * sin

    @pl.when(pl.program_id(2) == 0)
    def _():
        def body(i, _):
            rows = pl.ds(pl.multiple_of(i * rt, rt), rt)
            for c in range(2):
                cols = slice(c * hd, (c + 1) * hd)
                kr[rows, cols] = rope(k_ref[rows, cols].astype(F32), cosk_ref[rows, :],
                                      sink_ref[rows, :]).astype(BF16)
            return 0
        lax.fori_loop(0, t_len // rt, body, 0)
        ctx_rows = pl.ds(t_len, c_len)
        kr[ctx_rows, :] = k_ref[ctx_rows, :]

    lam_v = lam_ref[...]
    lam = (jnp.exp(jnp.sum(lam_v[0:1] * lam_v[1:2], axis=-1, keepdims=True))
           - jnp.exp(jnp.sum(lam_v[2:3] * lam_v[3:4], axis=-1, keepdims=True)) + lambda_init)
    q_scale = (hd ** -0.5) * math.log2(math.e)
    tq = q_refs[0].shape[0]
    for part, q_ref in enumerate(q_refs):
        rows = slice(part * tq, (part + 1) * tq)
        outs = []
        for c in range(2):
            cols = slice(c * hd, (c + 1) * hd)
            qc = (rope(q_ref[:, cols].astype(F32), cosq_ref[rows, :], sinq_ref[rows, :])
                  * q_scale).astype(BF16)
            s = _dot_nt(qc, kr[:, cols])
            e = jnp.exp2(s - jnp.max(s, axis=-1, keepdims=True))
            denom = jnp.sum(e, axis=-1, keepdims=True)
            outs.append(_dot(e.astype(BF16), v_ref[...]) / denom)
        o = outs[0] - lam * outs[1]
        o_ref[rows, :] = ((_rms(o) * subln_ref[...]) * (1.0 - lambda_init)).astype(BF16)


def diff_attention(qkv, lam_vecs, subln_w, cos, sin, *, batch, t_len, c_len, heads, lambda_init):
    s_len = t_len + c_len
    hd = DIFF_HEAD_DIM
    tq = ROW_TILE
    n_q = _pick(t_len // tq, (ATTN_Q_TILES, 2, 1))
    steps, s_tiles = t_len // (n_q * tq), s_len // tq
    kern = functools.partial(_diff_attn_kernel, n_q=n_q, t_len=t_len, c_len=c_len, lambda_init=lambda_init)
    q_specs = [pl.BlockSpec((tq, 2 * hd), lambda b, h, i, j=j: (b * s_tiles + n_q * i + j, h))
               for j in range(n_q)]
    return pl.pallas_call(
        kern, grid=(batch, heads, steps),
        in_specs=[pl.BlockSpec((4, hd), lambda b, h, i: (0, 0)), *q_specs,
                  pl.BlockSpec((s_len, 2 * hd), lambda b, h, i: (b, heads + h)),
                  pl.BlockSpec((s_len, 2 * hd), lambda b, h, i: (b, 2 * heads + h)),
                  pl.BlockSpec((n_q * tq, hd), lambda b, h, i: (i, 0)),
                  pl.BlockSpec((n_q * tq, hd), lambda b, h, i: (i, 0)),
                  pl.BlockSpec((t_len, hd), lambda b, h, i: (0, 0)),
                  pl.BlockSpec((t_len, hd), lambda b, h, i: (0, 0)),
                  pl.BlockSpec((1, 2 * hd), lambda b, h, i: (0, 0))],
        out_specs=pl.BlockSpec((n_q * tq, 2 * hd), lambda b, h, i: (b * steps + i, h)),
        out_shape=jax.ShapeDtypeStruct((batch * t_len, heads * 2 * hd), BF16),
        scratch_shapes=[pltpu.VMEM((s_len, 2 * hd), BF16)],
        compiler_params=_params(("parallel", "parallel", "arbitrary")), name="diff_attention",
    )(lam_vecs, *([qkv] * (n_q + 2)), cos, sin, cos, sin, subln_w.reshape(1, 2 * hd))


def _shared_kernel(x_ref, wg_ref, wu_ref, wd_ref, o_ref, acc_ref):
    f = pl.program_id(1)

    @pl.when(f == 0)
    def _():
        acc_ref[...] = jnp.zeros_like(acc_ref)

    x = x_ref[...]
    hid = _silu(_dot(x, wg_ref[...])) * _dot(x, wu_ref[...])
    acc_ref[...] += _dot(hid.astype(BF16), wd_ref[...])

    @pl.when(f == pl.num_programs(1) - 1)
    def _():
        o_ref[...] = acc_ref[...].astype(o_ref.dtype)


def shared_expert(h, w_gate, w_up, w_down):
    m, d = h.shape
    f_dim = w_gate.shape[1]
    tm = _pick(m, (512, 256))
    tf = _pick(f_dim, (256, 128))
    return pl.pallas_call(
        _shared_kernel, grid=(m // tm, f_dim // tf),
        in_specs=[pl.BlockSpec((tm, d), lambda i, f: (i, 0)),
                  pl.BlockSpec((d, tf), lambda i, f: (0, f)),
                  pl.BlockSpec((d, tf), lambda i, f: (0, f)),
                  pl.BlockSpec((tf, d), lambda i, f: (f, 0))],
        out_specs=pl.BlockSpec((tm, d), lambda i, f: (i, 0)),
        out_shape=jax.ShapeDtypeStruct((m, d), BF16),
        scratch_shapes=[pltpu.VMEM((tm, d), F32)],
        compiler_params=_params(("parallel", "arbitrary")), name="shared_expert",
    )(h, w_gate, w_up, w_down)


def _dispatch_plan(cnt, pairs, tm):
    n_exp = cnt.shape[1]
    total = jnp.sum(cnt, axis=0)
    padded = (total + tm - 1) // tm * tm
    pend = jnp.cumsum(padded)
    base = (pend - padded)[None, :] + jnp.cumsum(cnt, axis=0) - cnt
    n_tiles = pairs // tm + n_exp
    tile_start = jnp.arange(n_tiles, dtype=jnp.int32) * tm
    valid = tile_start < pend[-1]
    tile_e = jnp.minimum(jnp.sum((tile_start[:, None] >= pend[None, :]).astype(jnp.int32), axis=1), n_exp - 1)
    tile_e = jnp.where(valid, tile_e, jnp.max(jnp.where(valid, tile_e, 0)))
    next_e = jnp.concatenate([tile_e[1:], jnp.full((1,), -1, jnp.int32)])
    next_valid = jnp.concatenate([valid[1:], jnp.zeros((1,), bool)])
    zero_fill = (~valid) | (tile_e != next_e) | (~next_valid)
    return (tile_e.astype(jnp.int32), valid.astype(jnp.int32), zero_fill.astype(jnp.int32),
            base.astype(jnp.int32))


def _positions_kernel(idx_ref, rank_ref, base_ref, pos_ref):
    n_exp = base_ref.shape[0]
    tm = idx_ref.shape[1]
    eiota = lax.broadcasted_iota(jnp.int32, (n_exp, tm), 0)
    base = jnp.broadcast_to(base_ref[...].astype(F32), (n_exp, tm))
    for k in range(MOE_TOPK):
        hit = eiota == idx_ref[k:k + 1, :]
        first = jnp.sum(jnp.where(hit, base, 0.0), axis=0, keepdims=True)
        pos_ref[k:k + 1, :] = first.astype(jnp.int32) + rank_ref[k:k + 1, :]


def pair_positions(idx_t, rank_t, base):
    k, n = idx_t.shape
    tiles, n_exp = base.shape
    tm = n // tiles
    spec = pl.BlockSpec((k, tm), lambda i: (0, i))
    return pl.pallas_call(
        _positions_kernel, grid=(tiles,),
        in_specs=[spec, spec, pl.BlockSpec((None, n_exp, 1), lambda i: (i, 0, 0))],
        out_specs=spec, out_shape=jax.ShapeDtypeStruct((k, n), jnp.int32),
        compiler_params=_params(("parallel",)), name="pair_positions",
    )(idx_t, rank_t, base.reshape(tiles, n_exp, 1))


def _dispatch_kernel(pos_ref, zf_ref, hp_ref, xs_hbm, zeros, sem_z, sem_s, *, n_tok, n_tiles):
    i = pl.program_id(0)
    td = hp_ref.shape[0]
    tm = zeros.shape[0]

    def zero_copy(j):
        return pltpu.make_async_copy(zeros, xs_hbm.at[pl.ds(pl.multiple_of(j * tm, tm), tm)], sem_z)

    @pl.when(i == 0)
    def _():
        zeros[...] = jnp.zeros_like(zeros)

        def start(j, _):
            @pl.when(zf_ref[j] == 1)
            def _():
                zero_copy(j).start()
            return 0

        def wait(j, _):
            @pl.when(zf_ref[j] == 1)
            def _():
                zero_copy(j).wait()
            return 0

        lax.fori_loop(0, n_tiles, start, 0)
        lax.fori_loop(0, n_tiles, wait, 0)

    def body(t, _):
        for k in range(MOE_TOPK):
            row = pos_ref[k * n_tok + i * td + t]
            pltpu.make_async_copy(hp_ref.at[pl.ds(t, 1)], xs_hbm.at[pl.ds(row, 1)], sem_s).start()
        return 0

    lax.fori_loop(0, td, body, 0, unroll=2)
    for k in range(MOE_TOPK):
        pltpu.make_async_copy(hp_ref, xs_hbm.at[pl.ds(0, td)], sem_s).wait()


def dispatch(hp, pos, zero_fill, n_tiles):
    n_tok, half = hp.shape
    tm = EXPERT_TILE
    grid_spec = pltpu.PrefetchScalarGridSpec(
        num_scalar_prefetch=2, grid=(n_tok // DISPATCH_TILE,),
        in_specs=[pl.BlockSpec((DISPATCH_TILE, half), lambda i, p, z: (i, 0))],
        out_specs=pl.BlockSpec(memory_space=pl.ANY),
        scratch_shapes=[pltpu.VMEM((tm, half), jnp.uint32), pltpu.SemaphoreType.DMA(()),
                        pltpu.SemaphoreType.DMA(())])
    return pl.pallas_call(
        functools.partial(_dispatch_kernel, n_tok=n_tok, n_tiles=n_tiles), grid_spec=grid_spec,
        out_shape=jax.ShapeDtypeStruct((n_tiles * tm, half), jnp.uint32),
        compiler_params=_params(("arbitrary",)), name="moe_dispatch",
    )(pos, zero_fill, hp)


def _expert_kernel(te_ref, tv_ref, x_ref, wg_ref, wu_ref, wd_ref, o_ref, wgb, wub, wdb):
    i = pl.program_id(0)
    prev = jnp.maximum(i - 1, 0)

    @pl.when((i == 0) | (te_ref[i] != te_ref[prev]))
    def _():
        wgb[...] = wg_ref[...].astype(BF16)
        wub[...] = wu_ref[...].astype(BF16)
        wdb[...] = wd_ref[...].astype(BF16)

    @pl.when(tv_ref[i] == 1)
    def _():
        lo, hi = _unpack_halves(x_ref[...])
        x = jnp.concatenate([lo.astype(BF16), hi.astype(BF16)], axis=1)
        hid = _silu(_dot(x, wgb[...])) * _dot(x, wub[...])
        o_ref[...] = _pack_halves(_dot(hid.astype(BF16), wdb[...]))

    @pl.when(tv_ref[i] == 0)
    def _():
        o_ref[...] = jnp.zeros_like(o_ref)


def routed_experts(xs, tile_e, tile_valid, w_gate, w_up, w_down, layer):
    tm = EXPERT_TILE
    n_tiles = tile_e.shape[0]
    _, n_exp, d, f = w_gate.shape
    wmap = lambda i, te, tv: (layer, te[i], 0, 0)
    grid_spec = pltpu.PrefetchScalarGridSpec(
        num_scalar_prefetch=2, grid=(n_tiles,),
        in_specs=[pl.BlockSpec((tm, d // 2), lambda i, te, tv: (i, 0)),
                  pl.BlockSpec((None, None, d, f), wmap),
                  pl.BlockSpec((None, None, d, f), wmap),
                  pl.BlockSpec((None, None, f, d), wmap)],
        out_specs=pl.BlockSpec((tm, d // 2), lambda i, te, tv: (i, 0)),
        scratch_shapes=[pltpu.VMEM((d, f), BF16), pltpu.VMEM((d, f), BF16), pltpu.VMEM((f, d), BF16)])
    return pl.pallas_call(
        _expert_kernel, grid_spec=grid_spec,
        out_shape=jax.ShapeDtypeStruct((n_tiles * tm, d // 2), jnp.uint32),
        compiler_params=_params(("arbitrary",)), name="routed_experts",
    )(tile_e, tile_valid, xs, w_gate, w_up, w_down)


def _combine_kernel(pos_ref, ys_hbm, sh_ref, w_ref, o_ref, buf_a, buf_b, sem_a, sem_b, *, n_tok):
    i = pl.program_id(0)
    n = pl.num_programs(0)
    tc = buf_a.shape[1]
    half = sh_ref.shape[-1] // 2
    last_tile = 2 * n - 1

    def issue(tile, buf, sem):
        for t in range(tc):
            for k in range(MOE_TOPK):
                p = pos_ref[k * n_tok + tile * tc + t]
                pltpu.make_async_copy(ys_hbm.at[pl.ds(p, 1)], buf.at[k, pl.ds(t, 1)], sem).start()

    def wait(buf, sem):
        for k in range(MOE_TOPK):
            pltpu.make_async_copy(ys_hbm.at[pl.ds(0, tc)], buf.at[k], sem).wait()

    def reduce(buf, rows):
        sh = sh_ref[rows, :].astype(F32)
        w = w_ref[rows, :]
        lo_acc, hi_acc = sh[:, :half], sh[:, half:]
        for k in range(MOE_TOPK):
            lo, hi = _unpack_halves(buf[k])
            lo_acc = lo_acc + w[:, k:k + 1] * lo
            hi_acc = hi_acc + w[:, k:k + 1] * hi
        o_ref[rows, :] = jnp.concatenate([lo_acc, hi_acc], axis=1).astype(o_ref.dtype)

    @pl.when(i == 0)
    def _():
        issue(0, buf_a, sem_a)

    wait(buf_a, sem_a)
    issue(2 * i + 1, buf_b, sem_b)
    reduce(buf_a, slice(0, tc))
    wait(buf_b, sem_b)
    issue(jnp.minimum(2 * i + 2, last_tile), buf_a, sem_a)
    reduce(buf_b, slice(tc, 2 * tc))

    @pl.when(i == n - 1)
    def _():
        wait(buf_a, sem_a)


def combine(ys, pos, shared, wgt):
    n_tok, d = shared.shape
    tc = COMBINE_TILE
    buf = pltpu.VMEM((MOE_TOPK, tc, d // 2), jnp.uint32)
    grid_spec = pltpu.PrefetchScalarGridSpec(
        num_scalar_prefetch=1, grid=(n_tok // (2 * tc),),
        in_specs=[pl.BlockSpec(memory_space=pl.ANY),
                  pl.BlockSpec((2 * tc, d), lambda i, p: (i, 0)),
                  pl.BlockSpec((2 * tc, MOE_TOPK), lambda i, p: (i, 0))],
        out_specs=pl.BlockSpec((2 * tc, d), lambda i, p: (i, 0)),
        scratch_shapes=[buf, buf, pltpu.SemaphoreType.DMA(()), pltpu.SemaphoreType.DMA(())])
    return pl.pallas_call(
        functools.partial(_combine_kernel, n_tok=n_tok), grid_spec=grid_spec,
        out_shape=jax.ShapeDtypeStruct((n_tok, d), BF16),
        compiler_params=_params(("arbitrary",)), name="moe_combine",
    )(pos, ys, shared, wgt)


def moe_ffn(h, hp, route, w_gate, w_up, w_down, sh_gate, sh_up, sh_down, layer):
    idx_t, wgt_t, rank_t, cnt = route
    pairs = idx_t.shape[0] * idx_t.shape[1]
    tile_e, tile_valid, zero_fill, base = _dispatch_plan(cnt[:, :, 0], pairs, EXPERT_TILE)
    pos = pair_positions(idx_t, rank_t, base).reshape(pairs)
    xs = dispatch(hp, pos, zero_fill, tile_e.shape[0])
    ys = routed_experts(xs, tile_e, tile_valid, w_gate, w_up, w_down, layer)
    shared = shared_expert(h, sh_gate[layer].astype(BF16), sh_up[layer].astype(BF16),
                           sh_down[layer].astype(BF16))
    return combine(ys, pos, shared, wgt_t.T)


def kernel(x, c, ctx, c_ctx, ada_w, ada_b, norm_pre_mix, norm_post_mix, norm_pre_ffn, norm_post_ffn, ret_w_in, ret_w_out, ret_decay_fwd, ret_decay_bwd, diff_w_in, diff_w_out, diff_lam_q1, diff_lam_k1, diff_lam_q2, diff_lam_k2, diff_subln_w, moe_router_w, moe_router_b, moe_w_gate, moe_w_up, moe_w_down, moe_shared_gate, moe_shared_up, moe_shared_down):
    batch, t_len, d = x.shape
    c_len = ctx.shape[1]
    s_len = t_len + c_len
    depth = ada_w.shape[0]
    assert depth == 2 and batch + 1 <= 8
    assert t_len % ROW_TILE == 0 and c_len % ROW_TILE == 0 and t_len % GRID_W == 0
    ret_heads = ret_decay_fwd.shape[-1]
    diff_heads = d // (2 * DIFF_HEAD_DIM)
    lat_tiles, all_tiles = t_len // ROW_TILE, s_len // ROW_TILE

    cc = jnp.concatenate([c, c_ctx[None], jnp.zeros((8 - batch - 1, d), F32)], axis=0)
    mods = ada_modulation(cc, ada_w, ada_b)
    xs = jnp.concatenate([x, ctx], axis=1).reshape(batch * s_len, d)
    rope_ret = _rope_tables(t_len, d // ret_heads)
    rope_diff = _rope_tables(t_len, DIFF_HEAD_DIM)

    ident = lambda i: i
    uni_mod = lambda i: jnp.where(i % all_tiles < lat_tiles, i // all_tiles, batch)
    lat_mod = lambda i: i // lat_tiles
    lat_of_uni = lambda i: (i // lat_tiles) * all_tiles + i % lat_tiles

    (h,) = fused_norm(xs, ident, batch * all_tiles, uni_mod, mod_b=mods[0], w_pre=norm_pre_mix[0],
                      shift_idx=0, scale_idx=1)
    qkvg = matmul(h, ret_w_in, 0)
    lg_f = jax.nn.log_sigmoid(ret_decay_fwd[0].astype(F32))
    lg_b = jax.nn.log_sigmoid(ret_decay_bwd[0].astype(F32))
    r = retention(qkvg, lg_f, lg_b, *rope_ret, batch=batch, t_len=t_len, c_len=c_len, heads=ret_heads)
    y = matmul(r, ret_w_out, 0)
    xs, h, hp, *route = fused_norm(
        xs, ident, batch * all_tiles, uni_mod, y=y, mod_a=mods[0], w_post=norm_post_mix[0], gate_idx=2,
        mod_b=mods[0], w_pre=norm_pre_ffn[0], shift_idx=3, scale_idx=4,
        router_wt=moe_router_w[0].T, router_b=moe_router_b[0])
    f = moe_ffn(h, hp, route, moe_w_gate, moe_w_up, moe_w_down,
                moe_shared_gate, moe_shared_up, moe_shared_down, 0)
    xs, h = fused_norm(xs, ident, batch * all_tiles, uni_mod, y=f, mod_a=mods[0], w_post=norm_post_ffn[0],
                       gate_idx=5, mod_b=mods[1], w_pre=norm_pre_mix[1], shift_idx=0, scale_idx=1)

    qkv = matmul(h, diff_w_in, 0)
    lam_vecs = jnp.stack([diff_lam_q1[0], diff_lam_k1[0], diff_lam_q2[0], diff_lam_k2[0]]).astype(F32)
    lambda_init = 0.8 - 0.6 * math.exp(-0.3 * 1)
    a = diff_attention(qkv, lam_vecs, diff_subln_w[0], *rope_diff, batch=batch, t_len=t_len, c_len=c_len,
                       heads=diff_heads, lambda_init=lambda_init)
    y = matmul(a, diff_w_out, 0)
    xl, h, hp, *route = fused_norm(
        xs, lat_of_uni, batch * lat_tiles, lat_mod, y=y, mod_a=mods[1], w_post=norm_post_mix[1], gate_idx=2,
        mod_b=mods[1], w_pre=norm_pre_ffn[1], shift_idx=3, scale_idx=4,
        router_wt=moe_router_w[1].T, router_b=moe_router_b[1])
    f = moe_ffn(h, hp, route, moe_w_gate, moe_w_up, moe_w_down,
                moe_shared_gate, moe_shared_up, moe_shared_down, 1)
    (out,) = fused_norm(xl, ident, batch * lat_tiles, lat_mod, y=f, mod_a=mods[1], w_post=norm_post_ffn[1],
                        gate_idx=5)
    return out.reshape(batch, t_len, d)
```

```python
import functools
import math

import jax
import jax.numpy as jnp
from jax import lax
from jax.experimental import pallas as pl
from jax.experimental.pallas import tpu as pltpu

GRID_W = 64
N_ADA = 6
NORM_EPS = 1e-6
ROPE_BASE = 10000.0
RET_BLOCK = 256
DIFF_HEAD_DIM = 128
MOE_TOPK = 8
MOE_GROUPS = 8
MOE_TOPK_GROUPS = 4
ROUTED_SCALE = 2.5

LANES = 128
ROW_TILE = 256
EXPERT_TILE = 256
COMBINE_TILE = 64
DISPATCH_TILE = 256
ATTN_Q_TILES = 4
VMEM_LIMIT = 56 * 1024 * 1024

F32 = jnp.float32
BF16 = jnp.bfloat16


def _pick(dim, candidates):
    for c in candidates:
        if dim % c == 0:
            return c
    raise ValueError(f"no tile in {candidates} divides {dim}")


def _params(sem, vmem=VMEM_LIMIT):
    return pltpu.CompilerParams(dimension_semantics=sem, vmem_limit_bytes=vmem)


def _dot(a, b):
    return jnp.dot(a, b, preferred_element_type=F32)


def _dot_nt(a, b):
    return lax.dot_general(a, b, (((1,), (1,)), ((), ())), preferred_element_type=F32)


def _dot_tn(a, b):
    return lax.dot_general(a, b, (((0,), (0,)), ((), ())), preferred_element_type=F32)


def _silu(x):
    return x * jax.nn.sigmoid(x)


def _pack_halves(y):
    w = y.shape[-1] // 2
    lo = lax.bitcast_convert_type(y[:, :w].astype(BF16).astype(F32), jnp.uint32)
    hi = lax.bitcast_convert_type(y[:, w:].astype(BF16).astype(F32), jnp.uint32)
    return (hi & jnp.uint32(0xFFFF0000)) | (lo >> 16)


def _unpack_halves(p):
    lo = lax.bitcast_convert_type(p << 16, F32)
    hi = lax.bitcast_convert_type(p & jnp.uint32(0xFFFF0000), F32)
    return lo, hi


def _ada_kernel(c_ref, w_ref, b_ref, o_ref):
    a = _silu(c_ref[...]).astype(BF16)
    o_ref[...] = _dot(a, w_ref[...].astype(BF16)) + b_ref[...]


def ada_modulation(cc, ada_w, ada_b):
    depth, d, n = ada_w.shape
    tn = _pick(n, (512, 256, 128))
    out = pl.pallas_call(
        _ada_kernel,
        grid=(depth, n // tn),
        in_specs=[pl.BlockSpec((8, d), lambda l, j: (0, 0)),
                  pl.BlockSpec((None, d, tn), lambda l, j: (l, 0, j)),
                  pl.BlockSpec((None, 1, tn), lambda l, j: (l, 0, j))],
        out_specs=pl.BlockSpec((None, 8, tn), lambda l, j: (l, 0, j)),
        out_shape=jax.ShapeDtypeStruct((depth, 8, n), F32),
        compiler_params=_params(("parallel", "parallel")),
        name="ada_modulation",
    )(cc, ada_w, ada_b.reshape(depth, 1, n))
    return out.reshape(depth, 8, N_ADA, d)


def _rms(x):
    return x * lax.rsqrt(jnp.mean(x * x, axis=-1, keepdims=True) + NORM_EPS)


def _route(h, rw_ref, rb_ref, idx_ref, wgt_ref, rank_ref, cnt_ref):
    n_exp = rw_ref.shape[0]
    tm = h.shape[0]
    per_group = n_exp // MOE_GROUPS
    w = rw_ref[...]
    w_hi = w.astype(BF16)
    w_lo = (w - w_hi.astype(F32)).astype(BF16)
    h_hi = h.astype(BF16)
    h_lo = (h - h_hi.astype(F32)).astype(BF16)
    logits = _dot_nt(w_hi, h_hi) + (_dot_nt(w_hi, h_lo) + _dot_nt(w_lo, h_hi))
    scores = jax.nn.sigmoid(logits)
    biased = scores + rb_ref[...]
    neg = jnp.float32(-jnp.inf)
    sub = lax.broadcasted_iota(jnp.int32, (per_group, tm), 0)
    giota = lax.broadcasted_iota(jnp.int32, (MOE_GROUPS, tm), 0)
    gs = jnp.zeros((MOE_GROUPS, tm), F32)
    for g in range(MOE_GROUPS):
        blk = biased[g * per_group:(g + 1) * per_group]
        m1 = jnp.max(blk, axis=0, keepdims=True)
        i1 = jnp.min(jnp.where(blk == m1, sub, per_group), axis=0, keepdims=True)
        m2 = jnp.max(jnp.where(sub == i1, neg, blk), axis=0, keepdims=True)
        gs = jnp.where(giota == g, m1 + m2, gs)
    rank = jnp.zeros((MOE_GROUPS, tm), jnp.int32)
    for j in range(MOE_GROUPS):
        gj = gs[j:j + 1]
        beats = (gj > gs) | ((gj == gs) & (giota > j))
        rank = rank + beats.astype(jnp.int32)
    keep = (rank < MOE_TOPK_GROUPS).astype(F32)
    keep_e = jnp.concatenate(
        [jnp.broadcast_to(keep[g:g + 1], (per_group, tm)) for g in range(MOE_GROUPS)], axis=0)
    masked = jnp.where(keep_e > 0.5, biased, neg)
    eiota = lax.broadcasted_iota(jnp.int32, (n_exp, tm), 0)
    sel_w, hits = [], []
    for k in range(MOE_TOPK):
        m = jnp.max(masked, axis=0, keepdims=True)
        idx = jnp.min(jnp.where(masked == m, eiota, n_exp), axis=0, keepdims=True)
        hit = eiota == idx
        hits.append(hit)
        sel_w.append(jnp.sum(jnp.where(hit, scores, 0.0), axis=0, keepdims=True))
        masked = jnp.where(hit, neg, masked)
        idx_ref[k:k + 1, :] = idx
    total = sel_w[0]
    for k in range(1, MOE_TOPK):
        total = total + sel_w[k]
    for k in range(MOE_TOPK):
        wgt_ref[k:k + 1, :] = sel_w[k] / total * ROUTED_SCALE
    chosen = jnp.zeros((n_exp, tm), F32)
    for k in range(MOE_TOPK):
        chosen = jnp.where(hits[k], 1.0, chosen)
    before = (lax.broadcasted_iota(jnp.int32, (tm, tm), 0)
              < lax.broadcasted_iota(jnp.int32, (tm, tm), 1)).astype(BF16)
    prefix = _dot(chosen.astype(BF16), before)
    for k in range(MOE_TOPK):
        rank_ref[k:k + 1, :] = jnp.sum(jnp.where(hits[k], prefix, 0.0), axis=0,
                                       keepdims=True).astype(jnp.int32)
    cnt_ref[...] = jnp.broadcast_to(jnp.sum(chosen, axis=1, keepdims=True),
                                    cnt_ref.shape).astype(jnp.int32)


def _fused_norm_kernel(*refs, has_resid, has_prenorm, has_router, gate_idx, shift_idx, scale_idx):
    refs = list(refs)
    x_ref = refs.pop(0)
    if has_resid:
        y_ref, mod_a_ref, wpost_ref = refs.pop(0), refs.pop(0), refs.pop(0)
    if has_prenorm:
        mod_b_ref, wpre_ref = refs.pop(0), refs.pop(0)
    if has_router:
        rw_ref, rb_ref = refs.pop(0), refs.pop(0)
    x = x_ref[...]
    if has_resid:
        xo_ref = refs.pop(0)
        y = y_ref[...].astype(F32)
        x = x + mod_a_ref[gate_idx:gate_idx + 1, :] * (_rms(y) * wpost_ref[...])
        xo_ref[...] = x
    if has_prenorm:
        h_ref = refs.pop(0)
        h = (_rms(x) * wpre_ref[...]) * (1.0 + mod_b_ref[scale_idx:scale_idx + 1, :]) \
            + mod_b_ref[shift_idx:shift_idx + 1, :]
        h_ref[...] = h.astype(BF16)
        if has_router:
            hp_ref, idx_ref, wgt_ref, rank_ref, cnt_ref = (refs.pop(0) for _ in range(5))
            hp_ref[...] = _pack_halves(h)
            _route(h, rw_ref, rb_ref, idx_ref, wgt_ref, rank_ref, cnt_ref)


def fused_norm(x, x_tile_map, n_out_tiles, mod_row_map, *, y=None, mod_a=None, w_post=None, gate_idx=0,
               mod_b=None, w_pre=None, shift_idx=0, scale_idx=0, router_wt=None, router_b=None):
    d = x.shape[-1]
    tm = ROW_TILE
    has_resid, has_prenorm, has_router = y is not None, mod_b is not None, router_wt is not None
    row = lambda i: (i, 0)
    const = lambda i: (0, 0)
    mod_spec = pl.BlockSpec((None, N_ADA, d), lambda i: (mod_row_map(i), 0, 0))
    vec_spec = pl.BlockSpec((1, d), const)
    args, in_specs = [x], [pl.BlockSpec((tm, d), lambda i: (x_tile_map(i), 0))]
    out_shape, out_specs = [], []
    n_rows = n_out_tiles * tm
    if has_resid:
        args += [y, mod_a, w_post.reshape(1, d)]
        in_specs += [pl.BlockSpec((tm, d), row), mod_spec, vec_spec]
        out_shape.append(jax.ShapeDtypeStruct((n_rows, d), F32))
        out_specs.append(pl.BlockSpec((tm, d), row))
    if has_prenorm:
        args += [mod_b, w_pre.reshape(1, d)]
        in_specs += [mod_spec, vec_spec]
        out_shape.append(jax.ShapeDtypeStruct((n_rows, d), BF16))
        out_specs.append(pl.BlockSpec((tm, d), row))
    if has_router:
        n_exp = router_wt.shape[0]
        args += [router_wt, router_b.reshape(n_exp, 1)]
        in_specs += [pl.BlockSpec((n_exp, d), const), pl.BlockSpec((n_exp, 1), const)]
        out_shape += [jax.ShapeDtypeStruct((n_rows, d // 2), jnp.uint32),
                      jax.ShapeDtypeStruct((MOE_TOPK, n_rows), jnp.int32),
                      jax.ShapeDtypeStruct((MOE_TOPK, n_rows), F32),
                      jax.ShapeDtypeStruct((MOE_TOPK, n_rows), jnp.int32),
                      jax.ShapeDtypeStruct((n_out_tiles, n_exp, LANES), jnp.int32)]
        out_specs += [pl.BlockSpec((tm, d // 2), row),
                      pl.BlockSpec((MOE_TOPK, tm), lambda i: (0, i)),
                      pl.BlockSpec((MOE_TOPK, tm), lambda i: (0, i)),
                      pl.BlockSpec((MOE_TOPK, tm), lambda i: (0, i)),
                      pl.BlockSpec((None, n_exp, LANES), lambda i: (i, 0, 0))]
    kern = functools.partial(_fused_norm_kernel, has_resid=has_resid, has_prenorm=has_prenorm,
                             has_router=has_router, gate_idx=gate_idx, shift_idx=shift_idx,
                             scale_idx=scale_idx)
    return pl.pallas_call(
        kern, grid=(n_out_tiles,), in_specs=in_specs, out_specs=out_specs, out_shape=out_shape,
        compiler_params=_params(("parallel",)), name="fused_norm",
    )(*args)


def _mm_kernel(a_ref, w_ref, o_ref):
    o_ref[...] = _dot(a_ref[...], w_ref[...].astype(BF16)).astype(o_ref.dtype)


def _mm_acc_kernel(a_ref, w_ref, o_ref, acc_ref):
    k = pl.program_id(2)

    @pl.when(k == 0)
    def _():
        acc_ref[...] = jnp.zeros_like(acc_ref)

    acc_ref[...] += _dot(a_ref[...], w_ref[...].astype(BF16))

    @pl.when(k == pl.num_programs(2) - 1)
    def _():
        o_ref[...] = acc_ref[...].astype(o_ref.dtype)


def matmul(a, w, layer, out_dtype=BF16):
    m, k = a.shape
    n = w.shape[2]
    tm = _pick(m, (1024, 768, 512, 256))
    tn = _pick(n, (512, 256, 128))
    tk = _pick(k, (4096, 2048, 1024, 512))
    if tk == k:
        return pl.pallas_call(
            _mm_kernel, grid=(m // tm, n // tn),
            in_specs=[pl.BlockSpec((tm, k), lambda i, j: (i, 0)),
                      pl.BlockSpec((None, k, tn), lambda i, j: (layer, 0, j))],
            out_specs=pl.BlockSpec((tm, tn), lambda i, j: (i, j)),
            out_shape=jax.ShapeDtypeStruct((m, n), out_dtype),
            compiler_params=_params(("parallel", "parallel")), name="matmul",
        )(a, w)
    return pl.pallas_call(
        _mm_acc_kernel, grid=(m // tm, n // tn, k // tk),
        in_specs=[pl.BlockSpec((tm, tk), lambda i, j, l: (i, l)),
                  pl.BlockSpec((None, tk, tn), lambda i, j, l: (layer, l, j))],
        out_specs=pl.BlockSpec((tm, tn), lambda i, j, l: (i, j)),
        out_shape=jax.ShapeDtypeStruct((m, n), out_dtype),
        scratch_shapes=[pltpu.VMEM((tm, tn), F32)],
        compiler_params=_params(("parallel", "parallel", "arbitrary")), name="matmul_acc",
    )(a, w)


def _rope_tables(t_len, head_dim):
    rows = t_len // GRID_W
    n_freq = head_dim // 4
    row, col = jnp.meshgrid(jnp.arange(rows, dtype=F32), jnp.arange(GRID_W, dtype=F32), indexing="ij")
    inv_freq = ROPE_BASE ** (-jnp.arange(n_freq, dtype=F32) / n_freq)
    ang_r = row.reshape(-1, 1) * inv_freq
    ang_c = col.reshape(-1, 1) * inv_freq
    cr, sr, cc, sc = jnp.cos(ang_r), jnp.sin(ang_r), jnp.cos(ang_c), jnp.sin(ang_c)
    return (jnp.concatenate([cr, cr, cc, cc], axis=-1), jnp.concatenate([-sr, sr, -sc, sc], axis=-1))


def _swap_quarters(x, quarter):
    lane = lax.broadcasted_iota(jnp.int32, x.shape, 1)
    first = (lane % (2 * quarter)) < quarter
    return jnp.where(first, pltpu.roll(x, LANES - quarter, axis=1), pltpu.roll(x, quarter, axis=1))


def _retention_kernel(lgf_ref, lgb_ref, q_ref, k_ref, v_ref, g_ref, cos_ref, sin_ref, o_ref,
                      qr, kr, oacc_f, oacc_b, state_f, state_b, *, t_len, c_len):
    head = pl.program_id(1)
    L = RET_BLOCK
    s_len = t_len + c_len
    dk = q_ref.shape[-1]
    k_scale = dk ** -0.5
    rt = ROW_TILE

    def rope(x, rows):
        sw = jnp.concatenate([pltpu.roll(x[:, :LANES], LANES // 2, axis=1),
                              pltpu.roll(x[:, LANES:], LANES // 2, axis=1)], axis=1)
        return x * cos_ref[rows, :] + sw * sin_ref[rows, :]

    def prepare(row0):
        rows = slice(row0, row0 + L)
        if row0 < t_len:
            qr[rows, :] = rope(q_ref[rows, :].astype(F32), rows).astype(BF16)
            kr[rows, :] = (rope(k_ref[rows, :].astype(F32), rows) * k_scale).astype(BF16)
        else:
            qr[rows, :] = q_ref[rows, :]
            kr[rows, :] = (k_ref[rows, :].astype(F32) * k_scale).astype(BF16)

    ii = lax.broadcasted_iota(jnp.int32, (L, L), 0)
    jj = lax.broadcasted_iota(jnp.int32, (L, L), 1)
    rel = (ii - jj).astype(F32)
    idx = lax.broadcasted_iota(jnp.int32, (L, 1), 0).astype(F32)

    lg_f, lg_b = lgf_ref[head], lgb_ref[head]
    fwd = (jnp.where(rel >= 0, jnp.exp(lg_f * jnp.maximum(rel, 0.0)), 0.0),
           jnp.exp(lg_f * (idx + 1.0)),
           jnp.exp(lg_f * (L - 1.0 - idx)),
           jnp.exp(lg_f * L), state_f, oacc_f)
    bwd = (jnp.where(rel <= 0, jnp.exp(lg_b * jnp.maximum(-rel, 0.0)), 0.0),
           jnp.exp(lg_b * (L - idx)),
           jnp.exp(lg_b * idx),
           jnp.exp(lg_b * L), state_b, oacc_b)

    def chunk(row0, direction):
        dmat, q_decay, k_decay, chunk_decay, state, oacc = direction
        rows = slice(row0, row0 + L)
        qb, kb, vb = qr[rows, :], kr[rows, :], v_ref[rows, :]
        scores = _dot_nt(qb, kb) * dmat
        inner = _dot(scores.astype(BF16), vb)
        st = state[...]
        cross = _dot(qb, st.astype(BF16)) * q_decay
        oacc[rows, :] = inner + cross
        kd = (kb.astype(F32) * k_decay).astype(BF16)
        state[...] = st * chunk_decay + _dot_tn(kd, vb)

    def finish(row0):
        rows = slice(row0, row0 + L)
        o = _rms(oacc_f[rows, :] + oacc_b[rows, :])
        o_ref[rows, :] = (_silu(g_ref[rows, :].astype(F32)) * o).astype(BF16)

    def scan(r0, n_chunks):
        ready, done_f, done_b = set(), set(), set()
        for ci in range(n_chunks):
            cf, cb = ci, n_chunks - 1 - ci
            for c in (cf, cb):
                if c not in ready:
                    prepare(r0 + c * L)
                    ready.add(c)
            chunk(r0 + cf * L, fwd)
            chunk(r0 + cb * L, bwd)
            done_f.add(cf)
            done_b.add(cb)
            for c in sorted({cf, cb}):
                if c in done_f and c in done_b:
                    finish(r0 + c * L)

    state_f[...] = jnp.zeros_like(state_f)
    state_b[...] = jnp.zeros_like(state_b)
    scan(t_len, c_len // L)
    scan(0, t_len // L)


def retention(qkvg, lg_f, lg_b, cos, sin, *, batch, t_len, c_len, heads):
    s_len = t_len + c_len
    dk = cos.shape[-1]
    dv = 2 * dk
    assert dk == 2 * LANES
    kern = functools.partial(_retention_kernel, t_len=t_len, c_len=c_len)
    grid_spec = pltpu.PrefetchScalarGridSpec(
        num_scalar_prefetch=2, grid=(batch, heads),
        in_specs=[pl.BlockSpec((s_len, dk), lambda b, h, *_: (b, h)),
                  pl.BlockSpec((s_len, dk), lambda b, h, *_: (b, heads + h)),
                  pl.BlockSpec((s_len, dv), lambda b, h, *_: (b, heads + h)),
                  pl.BlockSpec((s_len, dv), lambda b, h, *_: (b, 2 * heads + h)),
                  pl.BlockSpec((t_len, dk), lambda b, h, *_: (0, 0)),
                  pl.BlockSpec((t_len, dk), lambda b, h, *_: (0, 0))],
        out_specs=pl.BlockSpec((s_len, dv), lambda b, h, *_: (b, h)),
        scratch_shapes=[pltpu.VMEM((s_len, dk), BF16), pltpu.VMEM((s_len, dk), BF16),
                        pltpu.VMEM((s_len, dv), F32), pltpu.VMEM((s_len, dv), F32),
                        pltpu.VMEM((dk, dv), F32), pltpu.VMEM((dk, dv), F32)])
    return pl.pallas_call(
        kern, grid_spec=grid_spec,
        out_shape=jax.ShapeDtypeStruct((batch * s_len, heads * dv), BF16),
        compiler_params=_params(("parallel", "parallel")), name="retention",
    )(lg_f, lg_b, qkvg, qkvg, qkvg, qkvg, cos, sin)


def _diff_attn_kernel(lam_ref, *refs, n_q, t_len, c_len, lambda_init):
    q_refs = refs[:n_q]
    k_ref, v_ref, cosq_ref, sinq_ref, cosk_ref, sink_ref, subln_ref, o_ref, kr = refs[n_q:]
    hd = DIFF_HEAD_DIM
    quarter = hd // 4
    rt = ROW_TILE

    def rope(x, cos, sin):
        return x * cos + _swap_quarters(x, quarter) * sin

    @pl.when(pl.program_id(2) == 0)
    def _():
        def body(i, _):
            rows = pl.ds(pl.multiple_of(i * rt, rt), rt)
            for c in range(2):
                cols = slice(c * hd, (c + 1) * hd)
                kr[rows, cols] = rope(k_ref[rows, cols].astype(F32), cosk_ref[rows, :],
                                      sink_ref[rows, :]).astype(BF16)
            return 0
        lax.fori_loop(0, t_len // rt, body, 0)
        ctx_rows = pl.ds(t_len, c_len)
        kr[ctx_rows, :] = k_ref[ctx_rows, :]

    lam_v = lam_ref[...]
    lam = (jnp.exp(jnp.sum(lam_v[0:1] * lam_v[1:2], axis=-1, keepdims=True))
           - jnp.exp(jnp.sum(lam_v[2:3] * lam_v[3:4], axis=-1, keepdims=True)) + lambda_init)
    q_scale = (hd ** -0.5) * math.log2(math.e)
    tq = q_refs[0].shape[0]
    for part, q_ref in enumerate(q_refs):
        rows = slice(part * tq, (part + 1) * tq)
        outs = []
        for c in range(2):
            cols = slice(c * hd, (c + 1) * hd)
            qc = (rope(q_ref[:, cols].astype(F32), cosq_ref[rows, :], sinq_ref[rows, :])
                  * q_scale).astype(BF16)
            s = _dot_nt(qc, kr[:, cols])
            e = jnp.exp2(s - jnp.max(s, axis=-1, keepdims=True))
            denom = jnp.sum(e, axis=-1, keepdims=True)
            outs.append(_dot(e.astype(BF16), v_ref[...]) / denom)
        o = outs[0] - lam * outs[1]
        o_ref[rows, :] = ((_rms(o) * subln_ref[...]) * (1.0 - lambda_init)).astype(BF16)


def diff_attention(qkv, lam_vecs, subln_w, cos, sin, *, batch, t_len, c_len, heads, lambda_init):
    s_len = t_len + c_len
    hd = DIFF_HEAD_DIM
    tq = ROW_TILE
    n_q = _pick(t_len // tq, (ATTN_Q_TILES, 2, 1))
    steps, s_tiles = t_len // (n_q * tq), s_len // tq
    kern = functools.partial(_diff_attn_kernel, n_q=n_q, t_len=t_len, c_len=c_len, lambda_init=lambda_init)
    q_specs = [pl.BlockSpec((tq, 2 * hd), lambda b, h, i, j=j: (b * s_tiles + n_q * i + j, h))
               for j in range(n_q)]
    return pl.pallas_call(
        kern, grid=(batch, heads, steps),
        in_specs=[pl.BlockSpec((4, hd), lambda b, h, i: (0, 0)), *q_specs,
                  pl.BlockSpec((s_len, 2 * hd), lambda b, h, i: (b, heads + h)),
                  pl.BlockSpec((s_len, 2 * hd), lambda b, h, i: (b, 2 * heads + h)),
                  pl.BlockSpec((n_q * tq, hd), lambda b, h, i: (i, 0)),
                  pl.BlockSpec((n_q * tq, hd), lambda b, h, i: (i, 0)),
                  pl.BlockSpec((t_len, hd), lambda b, h, i: (0, 0)),
                  pl.BlockSpec((t_len, hd), lambda b, h, i: (0, 0)),
                  pl.BlockSpec((1, 2 * hd), lambda b, h, i: (0, 0))],
        out_specs=pl.BlockSpec((n_q * tq, 2 * hd), lambda b, h, i: (b * steps + i, h)),
        out_shape=jax.ShapeDtypeStruct((batch * t_len, heads * 2 * hd), BF16),
        scratch_shapes=[pltpu.VMEM((s_len, 2 * hd), BF16)],
        compiler_params=_params(("parallel", "parallel", "arbitrary")), name="diff_attention",
    )(lam_vecs, *([qkv] * (n_q + 2)), cos, sin, cos, sin, subln_w.reshape(1, 2 * hd))


def _shared_kernel(x_ref, wg_ref, wu_ref, wd_ref, o_ref, acc_ref):
    f = pl.program_id(1)

    @pl.when(f == 0)
    def _():
        acc_ref[...] = jnp.zeros_like(acc_ref)

    x = x_ref[...]
    hid = _silu(_dot(x, wg_ref[...])) * _dot(x, wu_ref[...])
    acc_ref[...] += _dot(hid.astype(BF16), wd_ref[...])

    @pl.when(f == pl.num_programs(1) - 1)
    def _():
        o_ref[...] = acc_ref[...].astype(o_ref.dtype)


def shared_expert(h, w_gate, w_up, w_down):
    m, d = h.shape
    f_dim = w_gate.shape[1]
    tm = _pick(m, (512, 256))
    tf = _pick(f_dim, (512, 256, 128))
    return pl.pallas_call(
        _shared_kernel, grid=(m // tm, f_dim // tf),
        in_specs=[pl.BlockSpec((tm, d), lambda i, f: (i, 0)),
                  pl.BlockSpec((d, tf), lambda i, f: (0, f)),
                  pl.BlockSpec((d, tf), lambda i, f: (0, f)),
                  pl.BlockSpec((tf, d), lambda i, f: (f, 0))],
        out_specs=pl.BlockSpec((tm, d), lambda i, f: (i, 0)),
        out_shape=jax.ShapeDtypeStruct((m, d), BF16),
        scratch_shapes=[pltpu.VMEM((tm, d), F32)],
        compiler_params=_params(("parallel", "arbitrary")), name="shared_expert",
    )(h, w_gate, w_up, w_down)


def _dispatch_plan(cnt, pairs, tm):
    n_exp = cnt.shape[1]
    total = jnp.sum(cnt, axis=0)
    padded = (total + tm - 1) // tm * tm
    pend = jnp.cumsum(padded)
    base = (pend - padded)[None, :] + jnp.cumsum(cnt, axis=0) - cnt
    n_tiles = pairs // tm + n_exp
    tile_start = jnp.arange(n_tiles, dtype=jnp.int32) * tm
    valid = tile_start < pend[-1]
    tile_e = jnp.minimum(jnp.sum((tile_start[:, None] >= pend[None, :]).astype(jnp.int32), axis=1), n_exp - 1)
    tile_e = jnp.where(valid, tile_e, jnp.max(jnp.where(valid, tile_e, 0)))
    next_e = jnp.concatenate([tile_e[1:], jnp.full((1,), -1, jnp.int32)])
    next_valid = jnp.concatenate([valid[1:], jnp.zeros((1,), bool)])
    zero_fill = (~valid) | (tile_e != next_e) | (~next_valid)
    return (tile_e.astype(jnp.int32), valid.astype(jnp.int32), zero_fill.astype(jnp.int32),
            base.astype(jnp.int32))


def _positions_kernel(idx_ref, rank_ref, base_ref, pos_ref):
    n_exp = base_ref.shape[0]
    tm = idx_ref.shape[1]
    eiota = lax.broadcasted_iota(jnp.int32, (n_exp, tm), 0)
    base = jnp.broadcast_to(base_ref[...].astype(F32), (n_exp, tm))
    for k in range(MOE_TOPK):
        hit = eiota == idx_ref[k:k + 1, :]
        first = jnp.sum(jnp.where(hit, base, 0.0), axis=0, keepdims=True)
        pos_ref[k:k + 1, :] = first.astype(jnp.int32) + rank_ref[k:k + 1, :]


def pair_positions(idx_t, rank_t, base):
    k, n = idx_t.shape
    tiles, n_exp = base.shape
    tm = n // tiles
    spec = pl.BlockSpec((k, tm), lambda i: (0, i))
    return pl.pallas_call(
        _positions_kernel, grid=(tiles,),
        in_specs=[spec, spec, pl.BlockSpec((None, n_exp, 1), lambda i: (i, 0, 0))],
        out_specs=spec, out_shape=jax.ShapeDtypeStruct((k, n), jnp.int32),
        compiler_params=_params(("parallel",)), name="pair_positions",
    )(idx_t, rank_t, base.reshape(tiles, n_exp, 1))


def _dispatch_kernel(pos_ref, zf_ref, hp_ref, xs_hbm, zeros, sem_z, sem_s, *, n_tok, n_tiles):
    i = pl.program_id(0)
    td = hp_ref.shape[0]
    tm = zeros.shape[0]

    def zero_copy(j):
        return pltpu.make_async_copy(zeros, xs_hbm.at[pl.ds(pl.multiple_of(j * tm, tm), tm)], sem_z)

    @pl.when(i == 0)
    def _():
        zeros[...] = jnp.zeros_like(zeros)

        def start(j, _):
            @pl.when(zf_ref[j] == 1)
            def _():
                zero_copy(j).start()
            return 0

        def wait(j, _):
            @pl.when(zf_ref[j] == 1)
            def _():
                zero_copy(j).wait()
            return 0

        lax.fori_loop(0, n_tiles, start, 0)
        lax.fori_loop(0, n_tiles, wait, 0)

    def body(t, _):
        for k in range(MOE_TOPK):
            row = pos_ref[k * n_tok + i * td + t]
            pltpu.make_async_copy(hp_ref.at[pl.ds(t, 1)], xs_hbm.at[pl.ds(row, 1)], sem_s).start()
        return 0

    lax.fori_loop(0, td, body, 0, unroll=2)
    for k in range(MOE_TOPK):
        pltpu.make_async_copy(hp_ref, xs_hbm.at[pl.ds(0, td)], sem_s).wait()


def dispatch(hp, pos, zero_fill, n_tiles):
    n_tok, half = hp.shape
    tm = EXPERT_TILE
    grid_spec = pltpu.PrefetchScalarGridSpec(
        num_scalar_prefetch=2, grid=(n_tok // DISPATCH_TILE,),
        in_specs=[pl.BlockSpec((DISPATCH_TILE, half), lambda i, p, z: (i, 0))],
        out_specs=pl.BlockSpec(memory_space=pl.ANY),
        scratch_shapes=[pltpu.VMEM((tm, half), jnp.uint32), pltpu.SemaphoreType.DMA(()),
                        pltpu.SemaphoreType.DMA(())])
    return pl.pallas_call(
        functools.partial(_dispatch_kernel, n_tok=n_tok, n_tiles=n_tiles), grid_spec=grid_spec,
        out_shape=jax.ShapeDtypeStruct((n_tiles * tm, half), jnp.uint32),
        compiler_params=_params(("arbitrary",)), name="moe_dispatch",
    )(pos, zero_fill, hp)


def _expert_kernel(te_ref, tv_ref, first_ref, slot_ref, next_ref, x_ref, wg_hbm, wu_hbm, wd_hbm, o_ref,
                   stage_g, stage_u, stage_d, wgb, wub, wdb, sems, *, layer):
    i = pl.program_id(0)

    def weight_copies(e, s):
        return (pltpu.make_async_copy(wg_hbm.at[layer, e], stage_g.at[s], sems.at[s]),
                pltpu.make_async_copy(wu_hbm.at[layer, e], stage_u.at[s], sems.at[s]),
                pltpu.make_async_copy(wd_hbm.at[layer, e], stage_d.at[s], sems.at[s]))

    @pl.when(i == 0)
    def _():
        for cp in weight_copies(te_ref[0], 0):
            cp.start()

    @pl.when(first_ref[i] == 1)
    def _():
        s = slot_ref[i]
        for cp in weight_copies(te_ref[i], s):
            cp.wait()

        @pl.when(next_ref[i] >= 0)
        def _():
            for cp in weight_copies(next_ref[i], 1 - s):
                cp.start()

        wgb[...] = stage_g[s].astype(BF16)
        wub[...] = stage_u[s].astype(BF16)
        wdb[...] = stage_d[s].astype(BF16)

    @pl.when(tv_ref[i] == 1)
    def _():
        lo, hi = _unpack_halves(x_ref[...])
        x = jnp.concatenate([lo.astype(BF16), hi.astype(BF16)], axis=1)
        hid = _silu(_dot(x, wgb[...])) * _dot(x, wub[...])
        o_ref[...] = _pack_halves(_dot(hid.astype(BF16), wdb[...]))

    @pl.when(tv_ref[i] == 0)
    def _():
        o_ref[...] = jnp.zeros_like(o_ref)


def routed_experts(xs, tile_e, tile_valid, w_gate, w_up, w_down, layer):
    tm = EXPERT_TILE
    n_tiles = tile_e.shape[0]
    _, n_exp, d, f = w_gate.shape
    prev_e = jnp.concatenate([jnp.full((1,), -1, jnp.int32), tile_e[:-1]])
    first = (tile_e != prev_e).astype(jnp.int32)
    slot = (jnp.cumsum(first) - 1) % 2
    tiles = jnp.arange(n_tiles, dtype=jnp.int32)
    run_start = jnp.where(first == 1, tiles, n_tiles)
    next_start = jnp.min(jnp.where(run_start[None, :] > tiles[:, None], run_start[None, :], n_tiles), axis=1)
    next_e = jnp.where(next_start < n_tiles, tile_e[jnp.minimum(next_start, n_tiles - 1)], -1)
    idx_map = lambda i, *_: (i, 0)
    anyspace = pl.BlockSpec(memory_space=pl.ANY)
    grid_spec = pltpu.PrefetchScalarGridSpec(
        num_scalar_prefetch=5, grid=(n_tiles,),
        in_specs=[pl.BlockSpec((tm, d // 2), idx_map), anyspace, anyspace, anyspace],
        out_specs=pl.BlockSpec((tm, d // 2), idx_map),
        scratch_shapes=[pltpu.VMEM((2, d, f), F32), pltpu.VMEM((2, d, f), F32), pltpu.VMEM((2, f, d), F32),
                        pltpu.VMEM((d, f), BF16), pltpu.VMEM((d, f), BF16), pltpu.VMEM((f, d), BF16),
                        pltpu.SemaphoreType.DMA((2,))])
    return pl.pallas_call(
        functools.partial(_expert_kernel, layer=layer), grid_spec=grid_spec,
        out_shape=jax.ShapeDtypeStruct((n_tiles * tm, d // 2), jnp.uint32),
        compiler_params=_params(("arbitrary",)), name="routed_experts",
    )(tile_e, tile_valid, first, slot.astype(jnp.int32), next_e.astype(jnp.int32), xs, w_gate, w_up, w_down)


def _combine_kernel(pos_ref, ys_hbm, sh_ref, w_ref, o_ref, buf_a, buf_b, sem_a, sem_b, *, n_tok):
    i = pl.program_id(0)
    n = pl.num_programs(0)
    tc = buf_a.shape[1]
    half = sh_ref.shape[-1] // 2
    last_tile = 2 * n - 1

    def issue(tile, buf, sem):
        for t in range(tc):
            for k in range(MOE_TOPK):
                p = pos_ref[k * n_tok + tile * tc + t]
                pltpu.make_async_copy(ys_hbm.at[pl.ds(p, 1)], buf.at[k, pl.ds(t, 1)], sem).start()

    def wait(buf, sem):
        for k in range(MOE_TOPK):
            pltpu.make_async_copy(ys_hbm.at[pl.ds(0, tc)], buf.at[k], sem).wait()

    def reduce(buf, rows):
        sh = sh_ref[rows, :].astype(F32)
        w = w_ref[rows, :]
        lo_acc, hi_acc = sh[:, :half], sh[:, half:]
        for k in range(MOE_TOPK):
            lo, hi = _unpack_halves(buf[k])
            lo_acc = lo_acc + w[:, k:k + 1] * lo
            hi_acc = hi_acc + w[:, k:k + 1] * hi
        o_ref[rows, :] = jnp.concatenate([lo_acc, hi_acc], axis=1).astype(o_ref.dtype)

    @pl.when(i == 0)
    def _():
        issue(0, buf_a, sem_a)

    wait(buf_a, sem_a)
    issue(2 * i + 1, buf_b, sem_b)
    reduce(buf_a, slice(0, tc))
    wait(buf_b, sem_b)
    issue(jnp.minimum(2 * i + 2, last_tile), buf_a, sem_a)
    reduce(buf_b, slice(tc, 2 * tc))

    @pl.when(i == n - 1)
    def _():
        wait(buf_a, sem_a)


def combine(ys, pos, shared, wgt):
    n_tok, d = shared.shape
    tc = COMBINE_TILE
    buf = pltpu.VMEM((MOE_TOPK, tc, d // 2), jnp.uint32)
    grid_spec = pltpu.PrefetchScalarGridSpec(
        num_scalar_prefetch=1, grid=(n_tok // (2 * tc),),
        in_specs=[pl.BlockSpec(memory_space=pl.ANY),
                  pl.BlockSpec((2 * tc, d), lambda i, p: (i, 0)),
                  pl.BlockSpec((2 * tc, MOE_TOPK), lambda i, p: (i, 0))],
        out_specs=pl.BlockSpec((2 * tc, d), lambda i, p: (i, 0)),
        scratch_shapes=[buf, buf, pltpu.SemaphoreType.DMA(()), pltpu.SemaphoreType.DMA(())])
    return pl.pallas_call(
        functools.partial(_combine_kernel, n_tok=n_tok), grid_spec=grid_spec,
        out_shape=jax.ShapeDtypeStruct((n_tok, d), BF16),
        compiler_params=_params(("arbitrary",)), name="moe_combine",
    )(pos, ys, shared, wgt)


def moe_ffn(h, hp, route, w_gate, w_up, w_down, sh_gate, sh_up, sh_down, layer):
    idx_t, wgt_t, rank_t, cnt = route
    pairs = idx_t.shape[0] * idx_t.shape[1]
    tile_e, tile_valid, zero_fill, base = _dispatch_plan(cnt[:, :, 0], pairs, EXPERT_TILE)
    pos = pair_positions(idx_t, rank_t, base).reshape(pairs)
    xs = dispatch(hp, pos, zero_fill, tile_e.shape[0])
    ys = routed_experts(xs, tile_e, tile_valid, w_gate, w_up, w_down, layer)
    shared = shared_expert(h, sh_gate[layer].astype(BF16), sh_up[layer].astype(BF16),
                           sh_down[layer].astype(BF16))
    return combine(ys, pos, shared, wgt_t.T)


def kernel(x, c, ctx, c_ctx, ada_w, ada_b, norm_pre_mix, norm_post_mix, norm_pre_ffn, norm_post_ffn, ret_w_in, ret_w_out, ret_decay_fwd, ret_decay_bwd, diff_w_in, diff_w_out, diff_lam_q1, diff_lam_k1, diff_lam_q2, diff_lam_k2, diff_subln_w, moe_router_w, moe_router_b, moe_w_gate, moe_w_up, moe_w_down, moe_shared_gate, moe_shared_up, moe_shared_down):
    batch, t_len, d = x.shape
    c_len = ctx.shape[1]
    s_len = t_len + c_len
    depth = ada_w.shape[0]
    assert depth == 2 and batch + 1 <= 8
    assert t_len % ROW_TILE == 0 and c_len % ROW_TILE == 0 and t_len % GRID_W == 0
    ret_heads = ret_decay_fwd.shape[-1]
    diff_heads = d // (2 * DIFF_HEAD_DIM)
    lat_tiles, all_tiles = t_len // ROW_TILE, s_len // ROW_TILE

    cc = jnp.concatenate([c, c_ctx[None], jnp.zeros((8 - batch - 1, d), F32)], axis=0)
    mods = ada_modulation(cc, ada_w, ada_b)
    xs = jnp.concatenate([x, ctx], axis=1).reshape(batch * s_len, d)
    rope_ret = _rope_tables(t_len, d // ret_heads)
    rope_diff = _rope_tables(t_len, DIFF_HEAD_DIM)

    ident = lambda i: i
    uni_mod = lambda i: jnp.where(i % all_tiles < lat_tiles, i // all_tiles, batch)
    lat_mod = lambda i: i // lat_tiles
    lat_of_uni = lambda i: (i // lat_tiles) * all_tiles + i % lat_tiles

    (h,) = fused_norm(xs, ident, batch * all_tiles, uni_mod, mod_b=mods[0], w_pre=norm_pre_mix[0],
                      shift_idx=0, scale_idx=1)
    qkvg = matmul(h, ret_w_in, 0)
    lg_f = jax.nn.log_sigmoid(ret_decay_fwd[0].astype(F32))
    lg_b = jax.nn.log_sigmoid(ret_decay_bwd[0].astype(F32))
    r = retention(qkvg, lg_f, lg_b, *rope_ret, batch=batch, t_len=t_len, c_len=c_len, heads=ret_heads)
    y = matmul(r, ret_w_out, 0)
    xs, h, hp, *route = fused_norm(
        xs, ident, batch * all_tiles, uni_mod, y=y, mod_a=mods[0], w_post=norm_post_mix[0], gate_idx=2,
        mod_b=mods[0], w_pre=norm_pre_ffn[0], shift_idx=3, scale_idx=4,
        router_wt=moe_router_w[0].T, router_b=moe_router_b[0])
    f = moe_ffn(h, hp, route, moe_w_gate, moe_w_up, moe_w_down,
                moe_shared_gate, moe_shared_up, moe_shared_down, 0)
    xs, h = fused_norm(xs, ident, batch * all_tiles, uni_mod, y=f, mod_a=mods[0], w_post=norm_post_ffn[0],
                       gate_idx=5, mod_b=mods[1], w_pre=norm_pre_mix[1], shift_idx=0, scale_idx=1)

    qkv = matmul(h, diff_w_in, 0)
    lam_vecs = jnp.stack([diff_lam_q1[0], diff_lam_k1[0], diff_lam_q2[0], diff_lam_k2[0]]).astype(F32)
    lambda_init = 0.8 - 0.6 * math.exp(-0.3 * 1)
    a = diff_attention(qkv, lam_vecs, diff_subln_w[0], *rope_diff, batch=batch, t_len=t_len, c_len=c_len,
                       heads=diff_heads, lambda_init=lambda_init)
    y = matmul(a, diff_w_out, 0)
    xl, h, hp, *route = fused_norm(
        xs, lat_of_uni, batch * lat_tiles, lat_mod, y=y, mod_a=mods[1], w_post=norm_post_mix[1], gate_idx=2,
        mod_b=mods[1], w_pre=norm_pre_ffn[1], shift_idx=3, scale_idx=4,
        router_wt=moe_router_w[1].T, router_b=moe_router_b[1])
    f = moe_ffn(h, hp, route, moe_w_gate, moe_w_up, moe_w_down,
                moe_shared_gate, moe_shared_up, moe_shared_down, 1)
    (out,) = fused_norm(xl, ident, batch * lat_tiles, lat_mod, y=f, mod_a=mods[1], w_post=norm_post_ffn[1],
                        gate_idx=5)
    return out.reshape(batch, t_len, d)
```

```python
import functools
import math

import jax
import jax.numpy as jnp
from jax import lax
from jax.experimental import pallas as pl
from jax.experimental.pallas import tpu as pltpu

GRID_W = 64
N_ADA = 6
NORM_EPS = 1e-6
ROPE_BASE = 10000.0
RET_BLOCK = 256
DIFF_HEAD_DIM = 128
MOE_TOPK = 8
MOE_GROUPS = 8
MOE_TOPK_GROUPS = 4
ROUTED_SCALE = 2.5

LANES = 128
SUBLANES = 8
MOD_ROWS = SUBLANES
ROW_TILE = 256
EXPERT_TILE = 256
COMBINE_TILE = 64
DISPATCH_TILE = 256
ATTN_Q_TILES = 4
VMEM_LIMIT = 56 * 1024 * 1024

F32 = jnp.float32
BF16 = jnp.bfloat16


def _pick(dim, candidates):
    for c in candidates:
        if dim % c == 0:
            return c
    raise ValueError(f"no tile in {candidates} divides {dim}")


def _params(sem, vmem=VMEM_LIMIT):
    return pltpu.CompilerParams(dimension_semantics=sem, vmem_limit_bytes=vmem)


def _dot(a, b):
    return jnp.dot(a, b, preferred_element_type=F32)


def _dot_nt(a, b):
    return lax.dot_general(a, b, (((1,), (1,)), ((), ())), preferred_element_type=F32)


def _dot_tn(a, b):
    return lax.dot_general(a, b, (((0,), (0,)), ((), ())), preferred_element_type=F32)


def _silu(x):
    return x * jax.nn.sigmoid(x)


def _pack_halves(y):
    w = y.shape[-1] // 2
    lo = lax.bitcast_convert_type(y[:, :w].astype(BF16).astype(F32), jnp.uint32)
    hi = lax.bitcast_convert_type(y[:, w:].astype(BF16).astype(F32), jnp.uint32)
    return (hi & jnp.uint32(0xFFFF0000)) | (lo >> 16)


def _unpack_halves(p):
    lo = lax.bitcast_convert_type(p << 16, F32)
    hi = lax.bitcast_convert_type(p & jnp.uint32(0xFFFF0000), F32)
    return lo, hi


def _ada_kernel(c_ref, w_ref, b_ref, o_ref):
    a = _silu(c_ref[...]).astype(BF16)
    o_ref[...] = _dot(a, w_ref[...].astype(BF16)) + b_ref[...]


def ada_modulation(cc, ada_w, ada_b):
    depth, d, n = ada_w.shape
    tn = _pick(n, (512, 256, 128))
    out = pl.pallas_call(
        _ada_kernel,
        grid=(depth, n // tn),
        in_specs=[pl.BlockSpec((MOD_ROWS, d), lambda l, j: (0, 0)),
                  pl.BlockSpec((None, d, tn), lambda l, j: (l, 0, j)),
                  pl.BlockSpec((None, 1, tn), lambda l, j: (l, 0, j))],
        out_specs=pl.BlockSpec((None, MOD_ROWS, tn), lambda l, j: (l, 0, j)),
        out_shape=jax.ShapeDtypeStruct((depth, MOD_ROWS, n), F32),
        compiler_params=_params(("parallel", "parallel")),
        name="ada_modulation",
    )(cc, ada_w, ada_b.reshape(depth, 1, n))
    return out.reshape(depth, MOD_ROWS, N_ADA, d)


def _rms(x):
    return x * lax.rsqrt(jnp.mean(x * x, axis=-1, keepdims=True) + NORM_EPS)


def _route(h, rw_ref, rb_ref, idx_ref, wgt_ref, rank_ref, cnt_ref):
    n_exp = rw_ref.shape[0]
    tm = h.shape[0]
    per_group = n_exp // MOE_GROUPS
    w = rw_ref[...]
    w_hi = w.astype(BF16)
    w_lo = (w - w_hi.astype(F32)).astype(BF16)
    h_hi = h.astype(BF16)
    h_lo = (h - h_hi.astype(F32)).astype(BF16)
    logits = _dot_nt(w_hi, h_hi) + (_dot_nt(w_hi, h_lo) + _dot_nt(w_lo, h_hi))
    scores = jax.nn.sigmoid(logits)
    biased = scores + rb_ref[...]
    neg = jnp.float32(-jnp.inf)
    sub = lax.broadcasted_iota(jnp.int32, (per_group, tm), 0)
    giota = lax.broadcasted_iota(jnp.int32, (MOE_GROUPS, tm), 0)
    gs = jnp.zeros((MOE_GROUPS, tm), F32)
    for g in range(MOE_GROUPS):
        blk = biased[g * per_group:(g + 1) * per_group]
        m1 = jnp.max(blk, axis=0, keepdims=True)
        i1 = jnp.min(jnp.where(blk == m1, sub, per_group), axis=0, keepdims=True)
        m2 = jnp.max(jnp.where(sub == i1, neg, blk), axis=0, keepdims=True)
        gs = jnp.where(giota == g, m1 + m2, gs)
    rank = jnp.zeros((MOE_GROUPS, tm), jnp.int32)
    for j in range(MOE_GROUPS):
        gj = gs[j:j + 1]
        beats = (gj > gs) | ((gj == gs) & (giota > j))
        rank = rank + beats.astype(jnp.int32)
    keep = (rank < MOE_TOPK_GROUPS).astype(F32)
    keep_e = jnp.concatenate(
        [jnp.broadcast_to(keep[g:g + 1], (per_group, tm)) for g in range(MOE_GROUPS)], axis=0)
    masked = jnp.where(keep_e > 0.5, biased, neg)
    eiota = lax.broadcasted_iota(jnp.int32, (n_exp, tm), 0)
    sel_w, hits = [], []
    for k in range(MOE_TOPK):
        m = jnp.max(masked, axis=0, keepdims=True)
        idx = jnp.min(jnp.where(masked == m, eiota, n_exp), axis=0, keepdims=True)
        hit = eiota == idx
        hits.append(hit)
        sel_w.append(jnp.sum(jnp.where(hit, scores, 0.0), axis=0, keepdims=True))
        masked = jnp.where(hit, neg, masked)
        idx_ref[k:k + 1, :] = idx
    total = sel_w[0]
    for k in range(1, MOE_TOPK):
        total = total + sel_w[k]
    for k in range(MOE_TOPK):
        wgt_ref[k:k + 1, :] = sel_w[k] / total * ROUTED_SCALE
    chosen = jnp.zeros((n_exp, tm), F32)
    for k in range(MOE_TOPK):
        chosen = jnp.where(hits[k], 1.0, chosen)
    before = (lax.broadcasted_iota(jnp.int32, (tm, tm), 0)
              < lax.broadcasted_iota(jnp.int32, (tm, tm), 1)).astype(BF16)
    prefix = _dot(chosen.astype(BF16), before)
    for k in range(MOE_TOPK):
        rank_ref[k:k + 1, :] = jnp.sum(jnp.where(hits[k], prefix, 0.0), axis=0,
                                       keepdims=True).astype(jnp.int32)
    cnt_ref[...] = jnp.broadcast_to(jnp.sum(chosen, axis=1, keepdims=True),
                                    cnt_ref.shape).astype(jnp.int32)


def _fused_norm_kernel(*refs, has_resid, has_prenorm, has_router, gate_idx, shift_idx, scale_idx):
    refs = list(refs)
    x_ref = refs.pop(0)
    if has_resid:
        y_ref, mod_a_ref, wpost_ref = refs.pop(0), refs.pop(0), refs.pop(0)
    if has_prenorm:
        mod_b_ref, wpre_ref = refs.pop(0), refs.pop(0)
    if has_router:
        rw_ref, rb_ref = refs.pop(0), refs.pop(0)
    x = x_ref[...]
    if has_resid:
        xo_ref = refs.pop(0)
        y = y_ref[...].astype(F32)
        x = x + mod_a_ref[gate_idx:gate_idx + 1, :] * (_rms(y) * wpost_ref[...])
        xo_ref[...] = x
    if has_prenorm:
        h_ref = refs.pop(0)
        h = (_rms(x) * wpre_ref[...]) * (1.0 + mod_b_ref[scale_idx:scale_idx + 1, :]) \
            + mod_b_ref[shift_idx:shift_idx + 1, :]
        h_ref[...] = h.astype(BF16)
        if has_router:
            hp_ref, idx_ref, wgt_ref, rank_ref, cnt_ref = (refs.pop(0) for _ in range(5))
            hp_ref[...] = _pack_halves(h)
            _route(h, rw_ref, rb_ref, idx_ref, wgt_ref, rank_ref, cnt_ref)


def fused_norm(x, x_tile_map, n_out_tiles, mod_row_map, *, y=None, mod_a=None, w_post=None, gate_idx=0,
               mod_b=None, w_pre=None, shift_idx=0, scale_idx=0, router_wt=None, router_b=None):
    d = x.shape[-1]
    tm = ROW_TILE
    has_resid, has_prenorm, has_router = y is not None, mod_b is not None, router_wt is not None
    row = lambda i: (i, 0)
    const = lambda i: (0, 0)
    mod_spec = pl.BlockSpec((None, N_ADA, d), lambda i: (mod_row_map(i), 0, 0))
    vec_spec = pl.BlockSpec((1, d), const)
    args, in_specs = [x], [pl.BlockSpec((tm, d), lambda i: (x_tile_map(i), 0))]
    out_shape, out_specs = [], []
    n_rows = n_out_tiles * tm
    if has_resid:
        args += [y, mod_a, w_post.reshape(1, d)]
        in_specs += [pl.BlockSpec((tm, d), row), mod_spec, vec_spec]
        out_shape.append(jax.ShapeDtypeStruct((n_rows, d), F32))
        out_specs.append(pl.BlockSpec((tm, d), row))
    if has_prenorm:
        args += [mod_b, w_pre.reshape(1, d)]
        in_specs += [mod_spec, vec_spec]
        out_shape.append(jax.ShapeDtypeStruct((n_rows, d), BF16))
        out_specs.append(pl.BlockSpec((tm, d), row))
    if has_router:
        n_exp = router_wt.shape[0]
        args += [router_wt, router_b.reshape(n_exp, 1)]
        in_specs += [pl.BlockSpec((n_exp, d), const), pl.BlockSpec((n_exp, 1), const)]
        out_shape += [jax.ShapeDtypeStruct((n_rows, d // 2), jnp.uint32),
                      jax.ShapeDtypeStruct((MOE_TOPK, n_rows), jnp.int32),
                      jax.ShapeDtypeStruct((MOE_TOPK, n_rows), F32),
                      jax.ShapeDtypeStruct((MOE_TOPK, n_rows), jnp.int32),
                      jax.ShapeDtypeStruct((n_out_tiles, n_exp, LANES), jnp.int32)]
        out_specs += [pl.BlockSpec((tm, d // 2), row),
                      pl.BlockSpec((MOE_TOPK, tm), lambda i: (0, i)),
                      pl.BlockSpec((MOE_TOPK, tm), lambda i: (0, i)),
                      pl.BlockSpec((MOE_TOPK, tm), lambda i: (0, i)),
                      pl.BlockSpec((None, n_exp, LANES), lambda i: (i, 0, 0))]
    kern = functools.partial(_fused_norm_kernel, has_resid=has_resid, has_prenorm=has_prenorm,
                             has_router=has_router, gate_idx=gate_idx, shift_idx=shift_idx,
                             scale_idx=scale_idx)
    return pl.pallas_call(
        kern, grid=(n_out_tiles,), in_specs=in_specs, out_specs=out_specs, out_shape=out_shape,
        compiler_params=_params(("parallel",)), name="fused_norm",
    )(*args)


def _mm_kernel(a_ref, w_ref, o_ref):
    o_ref[...] = _dot(a_ref[...], w_ref[...].astype(BF16)).astype(o_ref.dtype)


def _mm_acc_kernel(a_ref, w_ref, o_ref, acc_ref):
    k = pl.program_id(2)

    @pl.when(k == 0)
    def _():
        acc_ref[...] = jnp.zeros_like(acc_ref)

    acc_ref[...] += _dot(a_ref[...], w_ref[...].astype(BF16))

    @pl.when(k == pl.num_programs(2) - 1)
    def _():
        o_ref[...] = acc_ref[...].astype(o_ref.dtype)


def matmul(a, w, layer, out_dtype=BF16):
    m, k = a.shape
    n = w.shape[2]
    tm = _pick(m, (1024, 768, 512, 256))
    tn = _pick(n, (512, 256, 128))
    tk = _pick(k, (4096, 2048, 1024, 512))
    if tk == k:
        return pl.pallas_call(
            _mm_kernel, grid=(m // tm, n // tn),
            in_specs=[pl.BlockSpec((tm, k), lambda i, j: (i, 0)),
                      pl.BlockSpec((None, k, tn), lambda i, j: (layer, 0, j))],
            out_specs=pl.BlockSpec((tm, tn), lambda i, j: (i, j)),
            out_shape=jax.ShapeDtypeStruct((m, n), out_dtype),
            compiler_params=_params(("parallel", "parallel")), name="matmul",
        )(a, w)
    return pl.pallas_call(
        _mm_acc_kernel, grid=(m // tm, n // tn, k // tk),
        in_specs=[pl.BlockSpec((tm, tk), lambda i, j, l: (i, l)),
                  pl.BlockSpec((None, tk, tn), lambda i, j, l: (layer, l, j))],
        out_specs=pl.BlockSpec((tm, tn), lambda i, j, l: (i, j)),
        out_shape=jax.ShapeDtypeStruct((m, n), out_dtype),
        scratch_shapes=[pltpu.VMEM((tm, tn), F32)],
        compiler_params=_params(("parallel", "parallel", "arbitrary")), name="matmul_acc",
    )(a, w)


def _rope_tables(t_len, head_dim):
    rows = t_len // GRID_W
    n_freq = head_dim // 4
    row, col = jnp.meshgrid(jnp.arange(rows, dtype=F32), jnp.arange(GRID_W, dtype=F32), indexing="ij")
    inv_freq = ROPE_BASE ** (-jnp.arange(n_freq, dtype=F32) / n_freq)
    ang_r = row.reshape(-1, 1) * inv_freq
    ang_c = col.reshape(-1, 1) * inv_freq
    cr, sr, cc, sc = jnp.cos(ang_r), jnp.sin(ang_r), jnp.cos(ang_c), jnp.sin(ang_c)
    return (jnp.concatenate([cr, cr, cc, cc], axis=-1), jnp.concatenate([-sr, sr, -sc, sc], axis=-1))


def _swap_quarters(x, quarter):
    lane = lax.broadcasted_iota(jnp.int32, x.shape, 1)
    first = (lane % (2 * quarter)) < quarter
    return jnp.where(first, pltpu.roll(x, LANES - quarter, axis=1), pltpu.roll(x, quarter, axis=1))


def _retention_kernel(lgf_ref, lgb_ref, q_ref, k_ref, v_ref, g_ref, cos_ref, sin_ref, o_ref,
                      qr, kr, oacc_f, oacc_b, state_f, state_b, *, t_len, c_len):
    head = pl.program_id(1)
    L = RET_BLOCK
    dk = q_ref.shape[-1]
    k_scale = dk ** -0.5

    def rope(x, rows):
        sw = jnp.concatenate([pltpu.roll(x[:, :LANES], LANES // 2, axis=1),
                              pltpu.roll(x[:, LANES:], LANES // 2, axis=1)], axis=1)
        return x * cos_ref[rows, :] + sw * sin_ref[rows, :]

    def prepare(row0):
        rows = slice(row0, row0 + L)
        if row0 < t_len:
            qr[rows, :] = rope(q_ref[rows, :].astype(F32), rows).astype(BF16)
            kr[rows, :] = (rope(k_ref[rows, :].astype(F32), rows) * k_scale).astype(BF16)
        else:
            qr[rows, :] = q_ref[rows, :]
            kr[rows, :] = (k_ref[rows, :].astype(F32) * k_scale).astype(BF16)

    ii = lax.broadcasted_iota(jnp.int32, (L, L), 0)
    jj = lax.broadcasted_iota(jnp.int32, (L, L), 1)
    rel = (ii - jj).astype(F32)
    idx = lax.broadcasted_iota(jnp.int32, (L, 1), 0).astype(F32)

    lg_f, lg_b = lgf_ref[head], lgb_ref[head]
    fwd = (jnp.where(rel >= 0, jnp.exp(lg_f * jnp.maximum(rel, 0.0)), 0.0),
           jnp.exp(lg_f * (idx + 1.0)),
           jnp.exp(lg_f * (L - 1.0 - idx)),
           jnp.exp(lg_f * L), state_f, oacc_f)
    bwd = (jnp.where(rel <= 0, jnp.exp(lg_b * jnp.maximum(-rel, 0.0)), 0.0),
           jnp.exp(lg_b * (L - idx)),
           jnp.exp(lg_b * idx),
           jnp.exp(lg_b * L), state_b, oacc_b)

    def chunk(row0, direction):
        dmat, q_decay, k_decay, chunk_decay, state, oacc = direction
        rows = slice(row0, row0 + L)
        qb, kb, vb = qr[rows, :], kr[rows, :], v_ref[rows, :]
        scores = _dot_nt(qb, kb) * dmat
        inner = _dot(scores.astype(BF16), vb)
        st = state[...]
        cross = _dot(qb, st.astype(BF16)) * q_decay
        oacc[rows, :] = inner + cross
        kd = (kb.astype(F32) * k_decay).astype(BF16)
        state[...] = st * chunk_decay + _dot_tn(kd, vb)

    def finish(row0):
        rows = slice(row0, row0 + L)
        o = _rms(oacc_f[rows, :] + oacc_b[rows, :])
        o_ref[rows, :] = (_silu(g_ref[rows, :].astype(F32)) * o).astype(BF16)

    def scan(r0, n_chunks):
        ready, done_f, done_b = set(), set(), set()
        for ci in range(n_chunks):
            cf, cb = ci, n_chunks - 1 - ci
            for c in (cf, cb):
                if c not in ready:
                    prepare(r0 + c * L)
                    ready.add(c)
            chunk(r0 + cf * L, fwd)
            chunk(r0 + cb * L, bwd)
            done_f.add(cf)
            done_b.add(cb)
            for c in sorted({cf, cb}):
                if c in done_f and c in done_b:
                    finish(r0 + c * L)

    state_f[...] = jnp.zeros_like(state_f)
    state_b[...] = jnp.zeros_like(state_b)
    scan(t_len, c_len // L)
    scan(0, t_len // L)


def retention(qkvg, lg_f, lg_b, cos, sin, *, batch, t_len, c_len, heads):
    s_len = t_len + c_len
    dk = cos.shape[-1]
    dv = 2 * dk
    assert dk == 2 * LANES
    kern = functools.partial(_retention_kernel, t_len=t_len, c_len=c_len)
    grid_spec = pltpu.PrefetchScalarGridSpec(
        num_scalar_prefetch=2, grid=(batch, heads),
        in_specs=[pl.BlockSpec((s_len, dk), lambda b, h, *_: (b, h)),
                  pl.BlockSpec((s_len, dk), lambda b, h, *_: (b, heads + h)),
                  pl.BlockSpec((s_len, dv), lambda b, h, *_: (b, heads + h)),
                  pl.BlockSpec((s_len, dv), lambda b, h, *_: (b, 2 * heads + h)),
                  pl.BlockSpec((t_len, dk), lambda b, h, *_: (0, 0)),
                  pl.BlockSpec((t_len, dk), lambda b, h, *_: (0, 0))],
        out_specs=pl.BlockSpec((s_len, dv), lambda b, h, *_: (b, h)),
        scratch_shapes=[pltpu.VMEM((s_len, dk), BF16), pltpu.VMEM((s_len, dk), BF16),
                        pltpu.VMEM((s_len, dv), F32), pltpu.VMEM((s_len, dv), F32),
                        pltpu.VMEM((dk, dv), F32), pltpu.VMEM((dk, dv), F32)])
    return pl.pallas_call(
        kern, grid_spec=grid_spec,
        out_shape=jax.ShapeDtypeStruct((batch * s_len, heads * dv), BF16),
        compiler_params=_params(("parallel", "parallel")), name="retention",
    )(lg_f, lg_b, qkvg, qkvg, qkvg, qkvg, cos, sin)


def _diff_attn_kernel(lam_ref, *refs, n_q, t_len, c_len, lambda_init):
    q_refs = refs[:n_q]
    k_ref, v_ref, cosq_ref, sinq_ref, cosk_ref, sink_ref, subln_ref, o_ref, kr = refs[n_q:]
    hd = DIFF_HEAD_DIM
    quarter = hd // 4
    rt = ROW_TILE

    def rope(x, cos, sin):
        return x * cos + _swap_quarters(x, quarter) * sin

    @pl.when(pl.program_id(2) == 0)
    def _():
        def body(i, _):
            rows = pl.ds(pl.multiple_of(i * rt, rt), rt)
            for c in range(2):
                cols = slice(c * hd, (c + 1) * hd)
                kr[rows, cols] = rope(k_ref[rows, cols].astype(F32), cosk_ref[rows, :],
                                      sink_ref[rows, :]).astype(BF16)
            return 0
        lax.fori_loop(0, t_len // rt, body, 0)
        ctx_rows = pl.ds(t_len, c_len)
        kr[ctx_rows, :] = k_ref[ctx_rows, :]

    lam_v = lam_ref[...]
    lam = (jnp.exp(jnp.sum(lam_v[0:1] * lam_v[1:2], axis=-1, keepdims=True))
           - jnp.exp(jnp.sum(lam_v[2:3] * lam_v[3:4], axis=-1, keepdims=True)) + lambda_init)
    q_scale = (hd ** -0.5) * math.log2(math.e)
    tq = q_refs[0].shape[0]
    for part, q_ref in enumerate(q_refs):
        rows = slice(part * tq, (part + 1) * tq)
        outs = []
        for c in range(2):
            cols = slice(c * hd, (c + 1) * hd)
            qc = (rope(q_ref[:, cols].astype(F32), cosq_ref[rows, :], sinq_ref[rows, :])
                  * q_scale).astype(BF16)
            s = _dot_nt(qc, kr[:, cols])
            e = jnp.exp2(s - jnp.max(s, axis=-1, keepdims=True))
            denom = jnp.sum(e, axis=-1, keepdims=True)
            outs.append(_dot(e.astype(BF16), v_ref[...]) / denom)
        o = outs[0] - lam * outs[1]
        o_ref[rows, :] = ((_rms(o) * subln_ref[...]) * (1.0 - lambda_init)).astype(BF16)


def diff_attention(qkv, lam_vecs, subln_w, cos, sin, *, batch, t_len, c_len, heads, lambda_init):
    s_len = t_len + c_len
    hd = DIFF_HEAD_DIM
    tq = ROW_TILE
    n_q = _pick(t_len // tq, (ATTN_Q_TILES, 2, 1))
    steps, s_tiles = t_len // (n_q * tq), s_len // tq
    kern = functools.partial(_diff_attn_kernel, n_q=n_q, t_len=t_len, c_len=c_len, lambda_init=lambda_init)
    q_specs = [pl.BlockSpec((tq, 2 * hd), lambda b, h, i, j=j: (b * s_tiles + n_q * i + j, h))
               for j in range(n_q)]
    return pl.pallas_call(
        kern, grid=(batch, heads, steps),
        in_specs=[pl.BlockSpec((4, hd), lambda b, h, i: (0, 0)), *q_specs,
                  pl.BlockSpec((s_len, 2 * hd), lambda b, h, i: (b, heads + h)),
                  pl.BlockSpec((s_len, 2 * hd), lambda b, h, i: (b, 2 * heads + h)),
                  pl.BlockSpec((n_q * tq, hd), lambda b, h, i: (i, 0)),
                  pl.BlockSpec((n_q * tq, hd), lambda b, h, i: (i, 0)),
                  pl.BlockSpec((t_len, hd), lambda b, h, i: (0, 0)),
                  pl.BlockSpec((t_len, hd), lambda b, h, i: (0, 0)),
                  pl.BlockSpec((1, 2 * hd), lambda b, h, i: (0, 0))],
        out_specs=pl.BlockSpec((n_q * tq, 2 * hd), lambda b, h, i: (b * steps + i, h)),
        out_shape=jax.ShapeDtypeStruct((batch * t_len, heads * 2 * hd), BF16),
        scratch_shapes=[pltpu.VMEM((s_len, 2 * hd), BF16)],
        compiler_params=_params(("parallel", "parallel", "arbitrary")), name="diff_attention",
    )(lam_vecs, *([qkv] * (n_q + 2)), cos, sin, cos, sin, subln_w.reshape(1, 2 * hd))


def _shared_kernel(x_ref, wg_ref, wu_ref, wd_ref, o_ref, acc_ref):
    f = pl.program_id(1)

    @pl.when(f == 0)
    def _():
        acc_ref[...] = jnp.zeros_like(acc_ref)

    x = x_ref[...]
    hid = _silu(_dot(x, wg_ref[...])) * _dot(x, wu_ref[...])
    acc_ref[...] += _dot(hid.astype(BF16), wd_ref[...])

    @pl.when(f == pl.num_programs(1) - 1)
    def _():
        o_ref[...] = acc_ref[...].astype(o_ref.dtype)


def shared_expert(h, w_gate, w_up, w_down):
    m, d = h.shape
    f_dim = w_gate.shape[1]
    tm = _pick(m, (512, 256))
    tf = _pick(f_dim, (512, 256, 128))
    return pl.pallas_call(
        _shared_kernel, grid=(m // tm, f_dim // tf),
        in_specs=[pl.BlockSpec((tm, d), lambda i, f: (i, 0)),
                  pl.BlockSpec((d, tf), lambda i, f: (0, f)),
                  pl.BlockSpec((d, tf), lambda i, f: (0, f)),
                  pl.BlockSpec((tf, d), lambda i, f: (f, 0))],
        out_specs=pl.BlockSpec((tm, d), lambda i, f: (i, 0)),
        out_shape=jax.ShapeDtypeStruct((m, d), BF16),
        scratch_shapes=[pltpu.VMEM((tm, d), F32)],
        compiler_params=_params(("parallel", "arbitrary")), name="shared_expert",
    )(h, w_gate, w_up, w_down)


def _dispatch_plan(cnt, pairs, tm):
    n_exp = cnt.shape[1]
    total = jnp.sum(cnt, axis=0)
    padded = (total + tm - 1) // tm * tm
    pend = jnp.cumsum(padded)
    base = (pend - padded)[None, :] + jnp.cumsum(cnt, axis=0) - cnt
    n_tiles = pairs // tm + n_exp
    tile_start = jnp.arange(n_tiles, dtype=jnp.int32) * tm
    valid = tile_start < pend[-1]
    tile_e = jnp.minimum(jnp.sum((tile_start[:, None] >= pend[None, :]).astype(jnp.int32), axis=1), n_exp - 1)
    tile_e = jnp.where(valid, tile_e, jnp.max(jnp.where(valid, tile_e, 0)))
    next_e = jnp.concatenate([tile_e[1:], jnp.full((1,), -1, jnp.int32)])
    next_valid = jnp.concatenate([valid[1:], jnp.zeros((1,), bool)])
    zero_fill = (~valid) | (tile_e != next_e) | (~next_valid)
    return (tile_e.astype(jnp.int32), valid.astype(jnp.int32), zero_fill.astype(jnp.int32),
            base.astype(jnp.int32))


def _positions_kernel(idx_ref, rank_ref, base_ref, pos_ref):
    n_exp = base_ref.shape[0]
    tm = idx_ref.shape[1]
    eiota = lax.broadcasted_iota(jnp.int32, (n_exp, tm), 0)
    base = jnp.broadcast_to(base_ref[...].astype(F32), (n_exp, tm))
    for k in range(MOE_TOPK):
        hit = eiota == idx_ref[k:k + 1, :]
        first = jnp.sum(jnp.where(hit, base, 0.0), axis=0, keepdims=True)
        pos_ref[k:k + 1, :] = first.astype(jnp.int32) + rank_ref[k:k + 1, :]


def pair_positions(idx_t, rank_t, base):
    k, n = idx_t.shape
    tiles, n_exp = base.shape
    tm = n // tiles
    spec = pl.BlockSpec((k, tm), lambda i: (0, i))
    return pl.pallas_call(
        _positions_kernel, grid=(tiles,),
        in_specs=[spec, spec, pl.BlockSpec((None, n_exp, 1), lambda i: (i, 0, 0))],
        out_specs=spec, out_shape=jax.ShapeDtypeStruct((k, n), jnp.int32),
        compiler_params=_params(("parallel",)), name="pair_positions",
    )(idx_t, rank_t, base.reshape(tiles, n_exp, 1))


def _dispatch_kernel(pos_ref, zf_ref, hp_ref, xs_hbm, zeros, sem_z, sem_s, *, n_tok, n_tiles):
    i = pl.program_id(0)
    td = hp_ref.shape[0]
    tm = zeros.shape[0]

    def zero_copy(j):
        return pltpu.make_async_copy(zeros, xs_hbm.at[pl.ds(pl.multiple_of(j * tm, tm), tm)], sem_z)

    @pl.when(i == 0)
    def _():
        zeros[...] = jnp.zeros_like(zeros)

        def start(j, _):
            @pl.when(zf_ref[j] == 1)
            def _():
                zero_copy(j).start()
            return 0

        def wait(j, _):
            @pl.when(zf_ref[j] == 1)
            def _():
                zero_copy(j).wait()
            return 0

        lax.fori_loop(0, n_tiles, start, 0)
        lax.fori_loop(0, n_tiles, wait, 0)

    def body(t, _):
        for k in range(MOE_TOPK):
            row = pos_ref[k * n_tok + i * td + t]
            pltpu.make_async_copy(hp_ref.at[pl.ds(t, 1)], xs_hbm.at[pl.ds(row, 1)], sem_s).start()
        return 0

    lax.fori_loop(0, td, body, 0, unroll=2)
    for k in range(MOE_TOPK):
        pltpu.make_async_copy(hp_ref, xs_hbm.at[pl.ds(0, td)], sem_s).wait()


def dispatch(hp, pos, zero_fill, n_tiles):
    n_tok, half = hp.shape
    tm = EXPERT_TILE
    grid_spec = pltpu.PrefetchScalarGridSpec(
        num_scalar_prefetch=2, grid=(n_tok // DISPATCH_TILE,),
        in_specs=[pl.BlockSpec((DISPATCH_TILE, half), lambda i, p, z: (i, 0))],
        out_specs=pl.BlockSpec(memory_space=pl.ANY),
        scratch_shapes=[pltpu.VMEM((tm, half), jnp.uint32), pltpu.SemaphoreType.DMA(()),
                        pltpu.SemaphoreType.DMA(())])
    return pl.pallas_call(
        functools.partial(_dispatch_kernel, n_tok=n_tok, n_tiles=n_tiles), grid_spec=grid_spec,
        out_shape=jax.ShapeDtypeStruct((n_tiles * tm, half), jnp.uint32),
        compiler_params=_params(("arbitrary",)), name="moe_dispatch",
    )(pos, zero_fill, hp)


def _expert_kernel(te_ref, tv_ref, first_ref, slot_ref, next_ref, x_ref, wg_hbm, wu_hbm, wd_hbm, o_ref,
                   stage_g, stage_u, stage_d, wgb, wub, wdb, sems, *, layer):
    i = pl.program_id(0)

    def weight_copies(e, s):
        return (pltpu.make_async_copy(wg_hbm.at[layer, e], stage_g.at[s], sems.at[s]),
                pltpu.make_async_copy(wu_hbm.at[layer, e], stage_u.at[s], sems.at[s]),
                pltpu.make_async_copy(wd_hbm.at[layer, e], stage_d.at[s], sems.at[s]))

    @pl.when(i == 0)
    def _():
        for cp in weight_copies(te_ref[0], 0):
            cp.start()

    @pl.when(first_ref[i] == 1)
    def _():
        s = slot_ref[i]
        for cp in weight_copies(te_ref[i], s):
            cp.wait()

        @pl.when(next_ref[i] >= 0)
        def _():
            for cp in weight_copies(next_ref[i], 1 - s):
                cp.start()

        wgb[...] = stage_g[s].astype(BF16)
        wub[...] = stage_u[s].astype(BF16)
        wdb[...] = stage_d[s].astype(BF16)

    @pl.when(tv_ref[i] == 1)
    def _():
        lo, hi = _unpack_halves(x_ref[...])
        x = jnp.concatenate([lo.astype(BF16), hi.astype(BF16)], axis=1)
        hid = _silu(_dot(x, wgb[...])) * _dot(x, wub[...])
        o_ref[...] = _pack_halves(_dot(hid.astype(BF16), wdb[...]))

    @pl.when(tv_ref[i] == 0)
    def _():
        o_ref[...] = jnp.zeros_like(o_ref)


def routed_experts(xs, tile_e, tile_valid, w_gate, w_up, w_down, layer):
    tm = EXPERT_TILE
    n_tiles = tile_e.shape[0]
    _, n_exp, d, f = w_gate.shape
    prev_e = jnp.concatenate([jnp.full((1,), -1, jnp.int32), tile_e[:-1]])
    first = (tile_e != prev_e).astype(jnp.int32)
    slot = (jnp.cumsum(first) - 1) % 2
    tiles = jnp.arange(n_tiles, dtype=jnp.int32)
    run_start = jnp.where(first == 1, tiles, n_tiles)
    next_start = jnp.min(jnp.where(run_start[None, :] > tiles[:, None], run_start[None, :], n_tiles), axis=1)
    next_e = jnp.where(next_start < n_tiles, tile_e[jnp.minimum(next_start, n_tiles - 1)], -1)
    idx_map = lambda i, *_: (i, 0)
    anyspace = pl.BlockSpec(memory_space=pl.ANY)
    grid_spec = pltpu.PrefetchScalarGridSpec(
        num_scalar_prefetch=5, grid=(n_tiles,),
        in_specs=[pl.BlockSpec((tm, d // 2), idx_map), anyspace, anyspace, anyspace],
        out_specs=pl.BlockSpec((tm, d // 2), idx_map),
        scratch_shapes=[pltpu.VMEM((2, d, f), F32), pltpu.VMEM((2, d, f), F32), pltpu.VMEM((2, f, d), F32),
                        pltpu.VMEM((d, f), BF16), pltpu.VMEM((d, f), BF16), pltpu.VMEM((f, d), BF16),
                        pltpu.SemaphoreType.DMA((2,))])
    return pl.pallas_call(
        functools.partial(_expert_kernel, layer=layer), grid_spec=grid_spec,
        out_shape=jax.ShapeDtypeStruct((n_tiles * tm, d // 2), jnp.uint32),
        compiler_params=_params(("arbitrary",)), name="routed_experts",
    )(tile_e, tile_valid, first, slot.astype(jnp.int32), next_e.astype(jnp.int32), xs, w_gate, w_up, w_down)


def _combine_kernel(pos_ref, ys_hbm, sh_ref, w_ref, x_ref, mod_a_ref, wpost_ref, *refs, n_tok, has_prenorm,
                    gate_idx, shift_idx, scale_idx):
    if has_prenorm:
        mod_b_ref, wpre_ref, xo_ref, h_ref, buf_a, buf_b, sem_a, sem_b = refs
    else:
        xo_ref, buf_a, buf_b, sem_a, sem_b = refs
    i = pl.program_id(0)
    n = pl.num_programs(0)
    tc = buf_a.shape[1]
    half = sh_ref.shape[-1] // 2
    last_tile = 2 * n - 1

    def issue(tile, buf, sem):
        for t in range(tc):
            for k in range(MOE_TOPK):
                p = pos_ref[k * n_tok + tile * tc + t]
                pltpu.make_async_copy(ys_hbm.at[pl.ds(p, 1)], buf.at[k, pl.ds(t, 1)], sem).start()

    def wait(buf, sem):
        for k in range(MOE_TOPK):
            pltpu.make_async_copy(ys_hbm.at[pl.ds(0, tc)], buf.at[k], sem).wait()

    def reduce(buf, rows):
        sh = sh_ref[rows, :].astype(F32)
        w = w_ref[rows, :]
        lo_acc, hi_acc = sh[:, :half], sh[:, half:]
        for k in range(MOE_TOPK):
            lo, hi = _unpack_halves(buf[k])
            lo_acc = lo_acc + w[:, k:k + 1] * lo
            hi_acc = hi_acc + w[:, k:k + 1] * hi
        ffn = jnp.concatenate([lo_acc, hi_acc], axis=1)
        x = x_ref[rows, :] + mod_a_ref[gate_idx:gate_idx + 1, :] * (_rms(ffn) * wpost_ref[...])
        xo_ref[rows, :] = x
        if has_prenorm:
            h = (_rms(x) * wpre_ref[...]) * (1.0 + mod_b_ref[scale_idx:scale_idx + 1, :]) \
                + mod_b_ref[shift_idx:shift_idx + 1, :]
            h_ref[rows, :] = h.astype(BF16)

    @pl.when(i == 0)
    def _():
        issue(0, buf_a, sem_a)

    wait(buf_a, sem_a)
    issue(2 * i + 1, buf_b, sem_b)
    reduce(buf_a, slice(0, tc))
    wait(buf_b, sem_b)
    issue(jnp.minimum(2 * i + 2, last_tile), buf_a, sem_a)
    reduce(buf_b, slice(tc, 2 * tc))

    @pl.when(i == n - 1)
    def _():
        wait(buf_a, sem_a)


def combine(ys, pos, shared, wgt, x, mod_row_map, *, mod_a, w_post, gate_idx, mod_b=None, w_pre=None,
            shift_idx=0, scale_idx=0):
    n_tok, d = shared.shape
    tc = COMBINE_TILE
    step = 2 * tc
    has_prenorm = mod_b is not None
    buf = pltpu.VMEM((MOE_TOPK, tc, d // 2), jnp.uint32)
    row = lambda i, p: (i, 0)
    const = lambda i, p: (0, 0)
    mod_spec = pl.BlockSpec((None, N_ADA, d), lambda i, p: (mod_row_map(i * step), 0, 0))
    vec_spec = pl.BlockSpec((1, d), const)
    args = [pos, ys, shared, wgt, x, mod_a, w_post.reshape(1, d)]
    in_specs = [pl.BlockSpec(memory_space=pl.ANY), pl.BlockSpec((step, d), row),
                pl.BlockSpec((step, MOE_TOPK), row), pl.BlockSpec((step, d), row), mod_spec, vec_spec]
    out_shape = [jax.ShapeDtypeStruct((n_tok, d), F32)]
    out_specs = [pl.BlockSpec((step, d), row)]
    if has_prenorm:
        args += [mod_b, w_pre.reshape(1, d)]
        in_specs += [mod_spec, vec_spec]
        out_shape.append(jax.ShapeDtypeStruct((n_tok, d), BF16))
        out_specs.append(pl.BlockSpec((step, d), row))
    grid_spec = pltpu.PrefetchScalarGridSpec(
        num_scalar_prefetch=1, grid=(n_tok // step,), in_specs=in_specs, out_specs=out_specs,
        scratch_shapes=[buf, buf, pltpu.SemaphoreType.DMA(()), pltpu.SemaphoreType.DMA(())])
    kern = functools.partial(_combine_kernel, n_tok=n_tok, has_prenorm=has_prenorm, gate_idx=gate_idx,
                             shift_idx=shift_idx, scale_idx=scale_idx)
    return pl.pallas_call(
        kern, grid_spec=grid_spec, out_shape=out_shape,
        compiler_params=_params(("arbitrary",)), name="moe_combine",
    )(*args)


def moe_ffn(h, hp, route, w_gate, w_up, w_down, sh_gate, sh_up, sh_down, layer, x, mod_row_map, **epilogue):
    idx_t, wgt_t, rank_t, cnt = route
    pairs = idx_t.shape[0] * idx_t.shape[1]
    tile_e, tile_valid, zero_fill, base = _dispatch_plan(cnt[:, :, 0], pairs, EXPERT_TILE)
    pos = pair_positions(idx_t, rank_t, base).reshape(pairs)
    xs = dispatch(hp, pos, zero_fill, tile_e.shape[0])
    ys = routed_experts(xs, tile_e, tile_valid, w_gate, w_up, w_down, layer)
    shared = shared_expert(h, sh_gate[layer].astype(BF16), sh_up[layer].astype(BF16),
                           sh_down[layer].astype(BF16))
    return combine(ys, pos, shared, wgt_t.T, x, mod_row_map, **epilogue)


def kernel(x, c, ctx, c_ctx, ada_w, ada_b, norm_pre_mix, norm_post_mix, norm_pre_ffn, norm_post_ffn, ret_w_in, ret_w_out, ret_decay_fwd, ret_decay_bwd, diff_w_in, diff_w_out, diff_lam_q1, diff_lam_k1, diff_lam_q2, diff_lam_k2, diff_subln_w, moe_router_w, moe_router_b, moe_w_gate, moe_w_up, moe_w_down, moe_shared_gate, moe_shared_up, moe_shared_down):
    batch, t_len, d = x.shape
    c_len = ctx.shape[1]
    s_len = t_len + c_len
    depth = ada_w.shape[0]
    assert depth == 2 and batch + 1 <= MOD_ROWS
    assert t_len % ROW_TILE == 0 and c_len % ROW_TILE == 0 and t_len % GRID_W == 0
    ret_heads = ret_decay_fwd.shape[-1]
    diff_heads = d // (2 * DIFF_HEAD_DIM)
    lat_tiles, all_tiles = t_len // ROW_TILE, s_len // ROW_TILE

    cc = jnp.concatenate([c, c_ctx[None], jnp.zeros((MOD_ROWS - batch - 1, d), F32)], axis=0)
    mods = ada_modulation(cc, ada_w, ada_b)
    xs = jnp.concatenate([x, ctx], axis=1).reshape(batch * s_len, d)
    rope_ret = _rope_tables(t_len, d // ret_heads)
    rope_diff = _rope_tables(t_len, DIFF_HEAD_DIM)

    ident = lambda i: i
    uni_mod = lambda i: jnp.where(i % all_tiles < lat_tiles, i // all_tiles, batch)
    lat_mod = lambda i: i // lat_tiles
    uni_row_mod = lambda r: jnp.where(r % s_len < t_len, r // s_len, batch)
    lat_row_mod = lambda r: r // t_len
    lat_of_uni = lambda i: (i // lat_tiles) * all_tiles + i % lat_tiles

    (h,) = fused_norm(xs, ident, batch * all_tiles, uni_mod, mod_b=mods[0], w_pre=norm_pre_mix[0],
                      shift_idx=0, scale_idx=1)
    qkvg = matmul(h, ret_w_in, 0)
    lg_f = jax.nn.log_sigmoid(ret_decay_fwd[0].astype(F32))
    lg_b = jax.nn.log_sigmoid(ret_decay_bwd[0].astype(F32))
    r = retention(qkvg, lg_f, lg_b, *rope_ret, batch=batch, t_len=t_len, c_len=c_len, heads=ret_heads)
    y = matmul(r, ret_w_out, 0)
    xs, h, hp, *route = fused_norm(
        xs, ident, batch * all_tiles, uni_mod, y=y, mod_a=mods[0], w_post=norm_post_mix[0], gate_idx=2,
        mod_b=mods[0], w_pre=norm_pre_ffn[0], shift_idx=3, scale_idx=4,
        router_wt=moe_router_w[0].T, router_b=moe_router_b[0])
    xs, h = moe_ffn(h, hp, route, moe_w_gate, moe_w_up, moe_w_down,
                    moe_shared_gate, moe_shared_up, moe_shared_down, 0, xs, uni_row_mod,
                    mod_a=mods[0], w_post=norm_post_ffn[0], gate_idx=5,
                    mod_b=mods[1], w_pre=norm_pre_mix[1], shift_idx=0, scale_idx=1)

    qkv = matmul(h, diff_w_in, 0)
    lam_vecs = jnp.stack([diff_lam_q1[0], diff_lam_k1[0], diff_lam_q2[0], diff_lam_k2[0]]).astype(F32)
    lambda_init = 0.8 - 0.6 * math.exp(-0.3 * 1)
    a = diff_attention(qkv, lam_vecs, diff_subln_w[0], *rope_diff, batch=batch, t_len=t_len, c_len=c_len,
                       heads=diff_heads, lambda_init=lambda_init)
    y = matmul(a, diff_w_out, 0)
    xl, h, hp, *route = fused_norm(
        xs, lat_of_uni, batch * lat_tiles, lat_mod, y=y, mod_a=mods[1], w_post=norm_post_mix[1], gate_idx=2,
        mod_b=mods[1], w_pre=norm_pre_ffn[1], shift_idx=3, scale_idx=4,
        router_wt=moe_router_w[1].T, router_b=moe_router_b[1])
    (out,) = moe_ffn(h, hp, route, moe_w_gate, moe_w_up, moe_w_down,
                     moe_shared_gate, moe_shared_up, moe_shared_down, 1, xl, lat_row_mod,
                     mod_a=mods[1], w_post=norm_post_ffn[1], gate_idx=5)
    return out.reshape(batch, t_len, d)
```

```python
import functools
import math

import jax
import jax.numpy as jnp
from jax import lax
from jax.experimental import pallas as pl
from jax.experimental.pallas import tpu as pltpu

GRID_W = 64
N_ADA = 6
NORM_EPS = 1e-6
ROPE_BASE = 10000.0
RET_BLOCK = 256
DIFF_HEAD_DIM = 128
MOE_TOPK = 8
MOE_GROUPS = 8
MOE_TOPK_GROUPS = 4
ROUTED_SCALE = 2.5

LANES = 128
SUBLANES = 8
MOD_ROWS = SUBLANES
ROW_TILE = 256
EXPERT_TILE = 256
COMBINE_TILE = 64
DISPATCH_TILE = 256
ATTN_Q_TILES = 8
VMEM_LIMIT = 56 * 1024 * 1024

F32 = jnp.float32
BF16 = jnp.bfloat16


def _pick(dim, candidates):
    for c in candidates:
        if dim % c == 0:
            return c
    raise ValueError(f"no tile in {candidates} divides {dim}")


def _params(sem, vmem=VMEM_LIMIT):
    return pltpu.CompilerParams(dimension_semantics=sem, vmem_limit_bytes=vmem)


def _dot(a, b):
    return jnp.dot(a, b, preferred_element_type=F32)


def _dot_nt(a, b):
    return lax.dot_general(a, b, (((1,), (1,)), ((), ())), preferred_element_type=F32)


def _dot_tn(a, b):
    return lax.dot_general(a, b, (((0,), (0,)), ((), ())), preferred_element_type=F32)


def _silu(x):
    return x * jax.nn.sigmoid(x)


def _pack_halves(y):
    w = y.shape[-1] // 2
    lo = lax.bitcast_convert_type(y[:, :w].astype(BF16).astype(F32), jnp.uint32)
    hi = lax.bitcast_convert_type(y[:, w:].astype(BF16).astype(F32), jnp.uint32)
    return (hi & jnp.uint32(0xFFFF0000)) | (lo >> 16)


def _unpack_halves(p):
    lo = lax.bitcast_convert_type(p << 16, F32)
    hi = lax.bitcast_convert_type(p & jnp.uint32(0xFFFF0000), F32)
    return lo, hi


def _ada_kernel(c_ref, w_ref, b_ref, o_ref):
    a = _silu(c_ref[...]).astype(BF16)
    o_ref[...] = _dot(a, w_ref[...].astype(BF16)) + b_ref[...]


def ada_modulation(cc, ada_w, ada_b):
    depth, d, n = ada_w.shape
    tn = _pick(n, (512, 256, 128))
    out = pl.pallas_call(
        _ada_kernel,
        grid=(depth, n // tn),
        in_specs=[pl.BlockSpec((MOD_ROWS, d), lambda l, j: (0, 0)),
                  pl.BlockSpec((None, d, tn), lambda l, j: (l, 0, j)),
                  pl.BlockSpec((None, 1, tn), lambda l, j: (l, 0, j))],
        out_specs=pl.BlockSpec((None, MOD_ROWS, tn), lambda l, j: (l, 0, j)),
        out_shape=jax.ShapeDtypeStruct((depth, MOD_ROWS, n), F32),
        compiler_params=_params(("parallel", "parallel")),
        name="ada_modulation",
    )(cc, ada_w, ada_b.reshape(depth, 1, n))
    return out.reshape(depth, MOD_ROWS, N_ADA, d)


def _rms(x):
    return x * lax.rsqrt(jnp.mean(x * x, axis=-1, keepdims=True) + NORM_EPS)


def _route(h, rw_ref, rb_ref, idx_ref, wgt_ref, rank_ref, cnt_ref):
    n_exp = rw_ref.shape[0]
    tm = h.shape[0]
    per_group = n_exp // MOE_GROUPS
    w = rw_ref[...]
    w_hi = w.astype(BF16)
    w_lo = (w - w_hi.astype(F32)).astype(BF16)
    h_hi = h.astype(BF16)
    h_lo = (h - h_hi.astype(F32)).astype(BF16)
    logits = _dot_nt(w_hi, h_hi) + (_dot_nt(w_hi, h_lo) + _dot_nt(w_lo, h_hi))
    scores = jax.nn.sigmoid(logits)
    biased = scores + rb_ref[...]
    neg = jnp.float32(-jnp.inf)
    sub = lax.broadcasted_iota(jnp.int32, (per_group, tm), 0)
    giota = lax.broadcasted_iota(jnp.int32, (MOE_GROUPS, tm), 0)
    gs = jnp.zeros((MOE_GROUPS, tm), F32)
    for g in range(MOE_GROUPS):
        blk = biased[g * per_group:(g + 1) * per_group]
        m1 = jnp.max(blk, axis=0, keepdims=True)
        i1 = jnp.min(jnp.where(blk == m1, sub, per_group), axis=0, keepdims=True)
        m2 = jnp.max(jnp.where(sub == i1, neg, blk), axis=0, keepdims=True)
        gs = jnp.where(giota == g, m1 + m2, gs)
    rank = jnp.zeros((MOE_GROUPS, tm), jnp.int32)
    for j in range(MOE_GROUPS):
        gj = gs[j:j + 1]
        beats = (gj > gs) | ((gj == gs) & (giota > j))
        rank = rank + beats.astype(jnp.int32)
    keep = (rank < MOE_TOPK_GROUPS).astype(F32)
    keep_e = jnp.concatenate(
        [jnp.broadcast_to(keep[g:g + 1], (per_group, tm)) for g in range(MOE_GROUPS)], axis=0)
    masked = jnp.where(keep_e > 0.5, biased, neg)
    eiota = lax.broadcasted_iota(jnp.int32, (n_exp, tm), 0)
    sel_w, hits = [], []
    for k in range(MOE_TOPK):
        m = jnp.max(masked, axis=0, keepdims=True)
        idx = jnp.min(jnp.where(masked == m, eiota, n_exp), axis=0, keepdims=True)
        hit = eiota == idx
        hits.append(hit)
        sel_w.append(jnp.sum(jnp.where(hit, scores, 0.0), axis=0, keepdims=True))
        masked = jnp.where(hit, neg, masked)
        idx_ref[k:k + 1, :] = idx
    total = sel_w[0]
    for k in range(1, MOE_TOPK):
        total = total + sel_w[k]
    for k in range(MOE_TOPK):
        wgt_ref[k:k + 1, :] = sel_w[k] / total * ROUTED_SCALE
    chosen = jnp.zeros((n_exp, tm), F32)
    for k in range(MOE_TOPK):
        chosen = jnp.where(hits[k], 1.0, chosen)
    before = (lax.broadcasted_iota(jnp.int32, (tm, tm), 0)
              < lax.broadcasted_iota(jnp.int32, (tm, tm), 1)).astype(BF16)
    prefix = _dot(chosen.astype(BF16), before)
    for k in range(MOE_TOPK):
        rank_ref[k:k + 1, :] = jnp.sum(jnp.where(hits[k], prefix, 0.0), axis=0,
                                       keepdims=True).astype(jnp.int32)
    cnt_ref[...] = jnp.broadcast_to(jnp.sum(chosen, axis=1, keepdims=True),
                                    cnt_ref.shape).astype(jnp.int32)


def _mixer_norm_kernel(x_ref, y_ref, mod_ref, wpost_ref, wpre_ref, rw_ref, rb_ref,
                       xo_ref, h_ref, hp_ref, idx_ref, wgt_ref, rank_ref, cnt_ref, *, gate_idx, shift_idx,
                       scale_idx):
    x = x_ref[...] + mod_ref[gate_idx:gate_idx + 1, :] * (_rms(y_ref[...].astype(F32)) * wpost_ref[...])
    xo_ref[...] = x
    h = (_rms(x) * wpre_ref[...]) * (1.0 + mod_ref[scale_idx:scale_idx + 1, :]) \
        + mod_ref[shift_idx:shift_idx + 1, :]
    h_ref[...] = h.astype(BF16)
    hp_ref[...] = _pack_halves(h)
    _route(h, rw_ref, rb_ref, idx_ref, wgt_ref, rank_ref, cnt_ref)


def _ingest_kernel(x_ref, c_ref, mod_ref, wpre_ref, xs_ref, h_ref, *, lat_tiles, all_tiles, shift_idx,
                   scale_idx):
    is_latent = pl.program_id(0) % all_tiles < lat_tiles

    def emit(x):
        xs_ref[...] = x
        h = (_rms(x) * wpre_ref[...]) * (1.0 + mod_ref[scale_idx:scale_idx + 1, :]) \
            + mod_ref[shift_idx:shift_idx + 1, :]
        h_ref[...] = h.astype(BF16)

    @pl.when(is_latent)
    def _():
        emit(x_ref[...])

    @pl.when(jnp.logical_not(is_latent))
    def _():
        emit(c_ref[...])


def ingest(x, ctx, mod_row_map, mod, w_pre, shift_idx, scale_idx):
    batch, t_len, d = x.shape
    c_len = ctx.shape[1]
    tm = ROW_TILE
    lat_tiles, ctx_tiles = t_len // tm, c_len // tm
    all_tiles = lat_tiles + ctx_tiles
    x_map = lambda i: ((i // all_tiles) * lat_tiles + jnp.minimum(i % all_tiles, lat_tiles - 1), 0)
    c_map = lambda i: ((i // all_tiles) * ctx_tiles + jnp.maximum(i % all_tiles - lat_tiles, 0), 0)
    row = lambda i: (i, 0)
    kern = functools.partial(_ingest_kernel, lat_tiles=lat_tiles, all_tiles=all_tiles, shift_idx=shift_idx,
                             scale_idx=scale_idx)
    n_rows = batch * all_tiles * tm
    return pl.pallas_call(
        kern, grid=(batch * all_tiles,),
        in_specs=[pl.BlockSpec((tm, d), x_map), pl.BlockSpec((tm, d), c_map),
                  pl.BlockSpec((None, N_ADA, d), lambda i: (mod_row_map(i), 0, 0)),
                  pl.BlockSpec((1, d), lambda i: (0, 0))],
        out_specs=[pl.BlockSpec((tm, d), row), pl.BlockSpec((tm, d), row)],
        out_shape=[jax.ShapeDtypeStruct((n_rows, d), F32), jax.ShapeDtypeStruct((n_rows, d), BF16)],
        compiler_params=_params(("arbitrary",)), name="ingest",
    )(x.reshape(batch * t_len, d), ctx.reshape(batch * c_len, d), mod, w_pre.reshape(1, d))


def mixer_norm(x, x_tile_map, n_out_tiles, mod_row_map, y, mod, w_post, w_pre, router_wt, router_b, *,
               gate_idx, shift_idx, scale_idx):
    d = x.shape[-1]
    tm = ROW_TILE
    n_exp = router_wt.shape[0]
    n_rows = n_out_tiles * tm
    row = lambda i: (i, 0)
    const = lambda i: (0, 0)
    per_token = pl.BlockSpec((MOE_TOPK, tm), lambda i: (0, i))
    kern = functools.partial(_mixer_norm_kernel, gate_idx=gate_idx, shift_idx=shift_idx, scale_idx=scale_idx)
    return pl.pallas_call(
        kern, grid=(n_out_tiles,),
        in_specs=[pl.BlockSpec((tm, d), lambda i: (x_tile_map(i), 0)), pl.BlockSpec((tm, d), row),
                  pl.BlockSpec((None, N_ADA, d), lambda i: (mod_row_map(i), 0, 0)),
                  pl.BlockSpec((1, d), const), pl.BlockSpec((1, d), const),
                  pl.BlockSpec((n_exp, d), const), pl.BlockSpec((n_exp, 1), const)],
        out_specs=[pl.BlockSpec((tm, d), row), pl.BlockSpec((tm, d), row), pl.BlockSpec((tm, d // 2), row),
                   per_token, per_token, per_token,
                   pl.BlockSpec((None, n_exp, LANES), lambda i: (i, 0, 0))],
        out_shape=[jax.ShapeDtypeStruct((n_rows, d), F32), jax.ShapeDtypeStruct((n_rows, d), BF16),
                   jax.ShapeDtypeStruct((n_rows, d // 2), jnp.uint32),
                   jax.ShapeDtypeStruct((MOE_TOPK, n_rows), jnp.int32),
                   jax.ShapeDtypeStruct((MOE_TOPK, n_rows), F32),
                   jax.ShapeDtypeStruct((MOE_TOPK, n_rows), jnp.int32),
                   jax.ShapeDtypeStruct((n_out_tiles, n_exp, LANES), jnp.int32)],
        compiler_params=_params(("parallel",)), name="mixer_norm",
    )(x, y, mod, w_post.reshape(1, d), w_pre.reshape(1, d), router_wt, router_b.reshape(n_exp, 1))


def _mm_kernel(a_ref, w_ref, o_ref):
    o_ref[...] = _dot(a_ref[...], w_ref[...].astype(BF16)).astype(o_ref.dtype)


def _mm_acc_kernel(a_ref, w_ref, o_ref, acc_ref):
    k = pl.program_id(2)

    @pl.when(k == 0)
    def _():
        acc_ref[...] = jnp.zeros_like(acc_ref)

    acc_ref[...] += _dot(a_ref[...], w_ref[...].astype(BF16))

    @pl.when(k == pl.num_programs(2) - 1)
    def _():
        o_ref[...] = acc_ref[...].astype(o_ref.dtype)


def matmul(a, w, layer, out_dtype=BF16):
    m, k = a.shape
    n = w.shape[2]
    tm = _pick(m, (1024, 768, 512, 256))
    tn = _pick(n, (512, 256, 128))
    tk = _pick(k, (4096, 2048, 1024, 512))
    if tk == k:
        return pl.pallas_call(
            _mm_kernel, grid=(m // tm, n // tn),
            in_specs=[pl.BlockSpec((tm, k), lambda i, j: (i, 0)),
                      pl.BlockSpec((None, k, tn), lambda i, j: (layer, 0, j))],
            out_specs=pl.BlockSpec((tm, tn), lambda i, j: (i, j)),
            out_shape=jax.ShapeDtypeStruct((m, n), out_dtype),
            compiler_params=_params(("parallel", "parallel")), name="matmul",
        )(a, w)
    return pl.pallas_call(
        _mm_acc_kernel, grid=(m // tm, n // tn, k // tk),
        in_specs=[pl.BlockSpec((tm, tk), lambda i, j, l: (i, l)),
                  pl.BlockSpec((None, tk, tn), lambda i, j, l: (layer, l, j))],
        out_specs=pl.BlockSpec((tm, tn), lambda i, j, l: (i, j)),
        out_shape=jax.ShapeDtypeStruct((m, n), out_dtype),
        scratch_shapes=[pltpu.VMEM((tm, tn), F32)],
        compiler_params=_params(("parallel", "parallel", "arbitrary")), name="matmul_acc",
    )(a, w)


def _rope_tables(t_len, head_dim):
    rows = t_len // GRID_W
    n_freq = head_dim // 4
    row, col = jnp.meshgrid(jnp.arange(rows, dtype=F32), jnp.arange(GRID_W, dtype=F32), indexing="ij")
    inv_freq = ROPE_BASE ** (-jnp.arange(n_freq, dtype=F32) / n_freq)
    ang_r = row.reshape(-1, 1) * inv_freq
    ang_c = col.reshape(-1, 1) * inv_freq
    cr, sr, cc, sc = jnp.cos(ang_r), jnp.sin(ang_r), jnp.cos(ang_c), jnp.sin(ang_c)
    return (jnp.concatenate([cr, cr, cc, cc], axis=-1), jnp.concatenate([-sr, sr, -sc, sc], axis=-1))


def _swap_quarters(x, quarter):
    lane = lax.broadcasted_iota(jnp.int32, x.shape, 1)
    first = (lane % (2 * quarter)) < quarter
    return jnp.where(first, pltpu.roll(x, LANES - quarter, axis=1), pltpu.roll(x, quarter, axis=1))


def _retention_kernel(lgf_ref, lgb_ref, q_ref, k_ref, v_ref, g_ref, cos_ref, sin_ref, o_ref,
                      qr, kr, oacc_f, oacc_b, state_f, state_b, *, t_len, c_len):
    head = pl.program_id(1)
    L = RET_BLOCK
    dk = q_ref.shape[-1]
    k_scale = dk ** -0.5

    def rope(x, rows):
        sw = jnp.concatenate([pltpu.roll(x[:, :LANES], LANES // 2, axis=1),
                              pltpu.roll(x[:, LANES:], LANES // 2, axis=1)], axis=1)
        return x * cos_ref[rows, :] + sw * sin_ref[rows, :]

    def prepare(row0):
        rows = slice(row0, row0 + L)
        if row0 < t_len:
            qr[rows, :] = rope(q_ref[rows, :].astype(F32), rows).astype(BF16)
            kr[rows, :] = (rope(k_ref[rows, :].astype(F32), rows) * k_scale).astype(BF16)
        else:
            qr[rows, :] = q_ref[rows, :]
            kr[rows, :] = (k_ref[rows, :].astype(F32) * k_scale).astype(BF16)

    ii = lax.broadcasted_iota(jnp.int32, (L, L), 0)
    jj = lax.broadcasted_iota(jnp.int32, (L, L), 1)
    rel = (ii - jj).astype(F32)
    idx = lax.broadcasted_iota(jnp.int32, (L, 1), 0).astype(F32)

    lg_f, lg_b = lgf_ref[head], lgb_ref[head]
    fwd = (jnp.where(rel >= 0, jnp.exp(lg_f * jnp.maximum(rel, 0.0)), 0.0),
           jnp.exp(lg_f * (idx + 1.0)),
           jnp.exp(lg_f * (L - 1.0 - idx)),
           jnp.exp(lg_f * L), state_f, oacc_f)
    bwd = (jnp.where(rel <= 0, jnp.exp(lg_b * jnp.maximum(-rel, 0.0)), 0.0),
           jnp.exp(lg_b * (L - idx)),
           jnp.exp(lg_b * idx),
           jnp.exp(lg_b * L), state_b, oacc_b)

    def chunk(row0, direction):
        dmat, q_decay, k_decay, chunk_decay, state, oacc = direction
        rows = slice(row0, row0 + L)
        qb, kb, vb = qr[rows, :], kr[rows, :], v_ref[rows, :]
        scores = _dot_nt(qb, kb) * dmat
        inner = _dot(scores.astype(BF16), vb)
        st = state[...]
        cross = _dot(qb, st.astype(BF16)) * q_decay
        oacc[rows, :] = inner + cross
        kd = (kb.astype(F32) * k_decay).astype(BF16)
        state[...] = st * chunk_decay + _dot_tn(kd, vb)

    def finish(row0):
        rows = slice(row0, row0 + L)
        o = _rms(oacc_f[rows, :] + oacc_b[rows, :])
        o_ref[rows, :] = (_silu(g_ref[rows, :].astype(F32)) * o).astype(BF16)

    def scan(r0, n_chunks):
        ready, done_f, done_b = set(), set(), set()
        for ci in range(n_chunks):
            cf, cb = ci, n_chunks - 1 - ci
            for c in (cf, cb):
                if c not in ready:
                    prepare(r0 + c * L)
                    ready.add(c)
            chunk(r0 + cf * L, fwd)
            chunk(r0 + cb * L, bwd)
            done_f.add(cf)
            done_b.add(cb)
            for c in sorted({cf, cb}):
                if c in done_f and c in done_b:
                    finish(r0 + c * L)

    state_f[...] = jnp.zeros_like(state_f)
    state_b[...] = jnp.zeros_like(state_b)
    scan(t_len, c_len // L)
    scan(0, t_len // L)


def retention(qkvg, lg_f, lg_b, cos, sin, *, batch, t_len, c_len, heads):
    s_len = t_len + c_len
    dk = cos.shape[-1]
    dv = 2 * dk
    assert dk == 2 * LANES
    kern = functools.partial(_retention_kernel, t_len=t_len, c_len=c_len)
    grid_spec = pltpu.PrefetchScalarGridSpec(
        num_scalar_prefetch=2, grid=(batch, heads),
        in_specs=[pl.BlockSpec((s_len, dk), lambda b, h, *_: (b, h)),
                  pl.BlockSpec((s_len, dk), lambda b, h, *_: (b, heads + h)),
                  pl.BlockSpec((s_len, dv), lambda b, h, *_: (b, heads + h)),
                  pl.BlockSpec((s_len, dv), lambda b, h, *_: (b, 2 * heads + h)),
                  pl.BlockSpec((t_len, dk), lambda b, h, *_: (0, 0)),
                  pl.BlockSpec((t_len, dk), lambda b, h, *_: (0, 0))],
        out_specs=pl.BlockSpec((s_len, dv), lambda b, h, *_: (b, h)),
        scratch_shapes=[pltpu.VMEM((s_len, dk), BF16), pltpu.VMEM((s_len, dk), BF16),
                        pltpu.VMEM((s_len, dv), F32), pltpu.VMEM((s_len, dv), F32),
                        pltpu.VMEM((dk, dv), F32), pltpu.VMEM((dk, dv), F32)])
    return pl.pallas_call(
        kern, grid_spec=grid_spec,
        out_shape=jax.ShapeDtypeStruct((batch * s_len, heads * dv), BF16),
        compiler_params=_params(("parallel", "parallel")), name="retention",
    )(lg_f, lg_b, qkvg, qkvg, qkvg, qkvg, cos, sin)


def _diff_attn_kernel(lam_ref, *refs, n_q, t_len, c_len, lambda_init):
    q_refs = refs[:n_q]
    k_ref, v_ref, cosq_ref, sinq_ref, cosk_ref, sink_ref, subln_ref, o_ref, kr = refs[n_q:]
    hd = DIFF_HEAD_DIM
    quarter = hd // 4
    rt = ROW_TILE

    def rope(x, cos, sin):
        return x * cos + _swap_quarters(x, quarter) * sin

    @pl.when(pl.program_id(2) == 0)
    def _():
        def body(i, _):
            rows = pl.ds(pl.multiple_of(i * rt, rt), rt)
            for c in range(2):
                cols = slice(c * hd, (c + 1) * hd)
                kr[rows, cols] = rope(k_ref[rows, cols].astype(F32), cosk_ref[rows, :],
                                      sink_ref[rows, :]).astype(BF16)
            return 0
        lax.fori_loop(0, t_len // rt, body, 0)
        ctx_rows = pl.ds(t_len, c_len)
        kr[ctx_rows, :] = k_ref[ctx_rows, :]

    lam_v = lam_ref[...]
    lam = (jnp.exp(jnp.sum(lam_v[0:1] * lam_v[1:2], axis=-1, keepdims=True))
           - jnp.exp(jnp.sum(lam_v[2:3] * lam_v[3:4], axis=-1, keepdims=True)) + lambda_init)
    q_scale = (hd ** -0.5) * math.log2(math.e)
    tq = q_refs[0].shape[0]
    for part, q_ref in enumerate(q_refs):
        rows = slice(part * tq, (part + 1) * tq)
        outs = []
        for c in range(2):
            cols = slice(c * hd, (c + 1) * hd)
            qc = (rope(q_ref[:, cols].astype(F32), cosq_ref[rows, :], sinq_ref[rows, :])
                  * q_scale).astype(BF16)
            s = _dot_nt(qc, kr[:, cols])
            e = jnp.exp2(s - jnp.max(s, axis=-1, keepdims=True))
            denom = jnp.sum(e, axis=-1, keepdims=True)
            outs.append(_dot(e.astype(BF16), v_ref[...]) / denom)
        o = outs[0] - lam * outs[1]
        o_ref[rows, :] = ((_rms(o) * subln_ref[...]) * (1.0 - lambda_init)).astype(BF16)


def diff_attention(qkv, lam_vecs, subln_w, cos, sin, *, batch, t_len, c_len, heads, lambda_init):
    s_len = t_len + c_len
    hd = DIFF_HEAD_DIM
    tq = ROW_TILE
    n_q = _pick(t_len // tq, (ATTN_Q_TILES, 2, 1))
    steps, s_tiles = t_len // (n_q * tq), s_len // tq
    kern = functools.partial(_diff_attn_kernel, n_q=n_q, t_len=t_len, c_len=c_len, lambda_init=lambda_init)
    q_specs = [pl.BlockSpec((tq, 2 * hd), lambda b, h, i, j=j: (b * s_tiles + n_q * i + j, h))
               for j in range(n_q)]
    return pl.pallas_call(
        kern, grid=(batch, heads, steps),
        in_specs=[pl.BlockSpec((4, hd), lambda b, h, i: (0, 0)), *q_specs,
                  pl.BlockSpec((s_len, 2 * hd), lambda b, h, i: (b, heads + h)),
                  pl.BlockSpec((s_len, 2 * hd), lambda b, h, i: (b, 2 * heads + h)),
                  pl.BlockSpec((n_q * tq, hd), lambda b, h, i: (i, 0)),
                  pl.BlockSpec((n_q * tq, hd), lambda b, h, i: (i, 0)),
                  pl.BlockSpec((t_len, hd), lambda b, h, i: (0, 0)),
                  pl.BlockSpec((t_len, hd), lambda b, h, i: (0, 0)),
                  pl.BlockSpec((1, 2 * hd), lambda b, h, i: (0, 0))],
        out_specs=pl.BlockSpec((n_q * tq, 2 * hd), lambda b, h, i: (b * steps + i, h)),
        out_shape=jax.ShapeDtypeStruct((batch * t_len, heads * 2 * hd), BF16),
        scratch_shapes=[pltpu.VMEM((s_len, 2 * hd), BF16)],
        compiler_params=_params(("parallel", "parallel", "arbitrary")), name="diff_attention",
    )(lam_vecs, *([qkv] * (n_q + 2)), cos, sin, cos, sin, subln_w.reshape(1, 2 * hd))


def _shared_kernel(x_ref, wg_ref, wu_ref, wd_ref, o_ref, acc_ref):
    f = pl.program_id(1)

    @pl.when(f == 0)
    def _():
        acc_ref[...] = jnp.zeros_like(acc_ref)

    x = x_ref[...]
    hid = _silu(_dot(x, wg_ref[...])) * _dot(x, wu_ref[...])
    acc_ref[...] += _dot(hid.astype(BF16), wd_ref[...])

    @pl.when(f == pl.num_programs(1) - 1)
    def _():
        o_ref[...] = acc_ref[...].astype(o_ref.dtype)


def shared_expert(h, w_gate, w_up, w_down):
    m, d = h.shape
    f_dim = w_gate.shape[1]
    tm = _pick(m, (512, 256))
    tf = _pick(f_dim, (512, 256, 128))
    return pl.pallas_call(
        _shared_kernel, grid=(m // tm, f_dim // tf),
        in_specs=[pl.BlockSpec((tm, d), lambda i, f: (i, 0)),
                  pl.BlockSpec((d, tf), lambda i, f: (0, f)),
                  pl.BlockSpec((d, tf), lambda i, f: (0, f)),
                  pl.BlockSpec((tf, d), lambda i, f: (f, 0))],
        out_specs=pl.BlockSpec((tm, d), lambda i, f: (i, 0)),
        out_shape=jax.ShapeDtypeStruct((m, d), BF16),
        scratch_shapes=[pltpu.VMEM((tm, d), F32)],
        compiler_params=_params(("parallel", "arbitrary")), name="shared_expert",
    )(h, w_gate, w_up, w_down)


def _dispatch_plan(cnt, pairs, tm):
    n_exp = cnt.shape[1]
    total = jnp.sum(cnt, axis=0)
    padded = (total + tm - 1) // tm * tm
    pend = jnp.cumsum(padded)
    base = (pend - padded)[None, :] + jnp.cumsum(cnt, axis=0) - cnt
    n_tiles = pairs // tm + n_exp
    tile_start = jnp.arange(n_tiles, dtype=jnp.int32) * tm
    valid = tile_start < pend[-1]
    tile_e = jnp.minimum(jnp.sum((tile_start[:, None] >= pend[None, :]).astype(jnp.int32), axis=1), n_exp - 1)
    tile_e = jnp.where(valid, tile_e, jnp.max(jnp.where(valid, tile_e, 0)))
    next_e = jnp.concatenate([tile_e[1:], jnp.full((1,), -1, jnp.int32)])
    next_valid = jnp.concatenate([valid[1:], jnp.zeros((1,), bool)])
    zero_fill = (~valid) | (tile_e != next_e) | (~next_valid)
    return (tile_e.astype(jnp.int32), valid.astype(jnp.int32), zero_fill.astype(jnp.int32),
            base.astype(jnp.int32))


def _positions_kernel(idx_ref, rank_ref, base_ref, pos_ref):
    n_exp = base_ref.shape[0]
    tm = idx_ref.shape[1]
    eiota = lax.broadcasted_iota(jnp.int32, (n_exp, tm), 0)
    base = jnp.broadcast_to(base_ref[...].astype(F32), (n_exp, tm))
    for k in range(MOE_TOPK):
        hit = eiota == idx_ref[k:k + 1, :]
        first = jnp.sum(jnp.where(hit, base, 0.0), axis=0, keepdims=True)
        pos_ref[k:k + 1, :] = first.astype(jnp.int32) + rank_ref[k:k + 1, :]


def pair_positions(idx_t, rank_t, base):
    k, n = idx_t.shape
    tiles, n_exp = base.shape
    tm = n // tiles
    spec = pl.BlockSpec((k, tm), lambda i: (0, i))
    return pl.pallas_call(
        _positions_kernel, grid=(tiles,),
        in_specs=[spec, spec, pl.BlockSpec((None, n_exp, 1), lambda i: (i, 0, 0))],
        out_specs=spec, out_shape=jax.ShapeDtypeStruct((k, n), jnp.int32),
        compiler_params=_params(("parallel",)), name="pair_positions",
    )(idx_t, rank_t, base.reshape(tiles, n_exp, 1))


def _dispatch_kernel(pos_ref, zf_ref, hp_ref, wg_ref, wu_ref, wd_ref, xs_hbm, wgo_ref, wuo_ref, wdo_ref,
                     zeros, sem_z, sem_s, *, n_tok, n_tiles):
    i = pl.program_id(0)
    td = hp_ref.shape[0]
    tm = zeros.shape[0]
    wgo_ref[...] = wg_ref[...].astype(BF16)
    wuo_ref[...] = wu_ref[...].astype(BF16)
    wdo_ref[...] = wd_ref[...].astype(BF16)

    def zero_copy(j):
        return pltpu.make_async_copy(zeros, xs_hbm.at[pl.ds(pl.multiple_of(j * tm, tm), tm)], sem_z)

    @pl.when(i == 0)
    def _():
        zeros[...] = jnp.zeros_like(zeros)

        def start(j, _):
            @pl.when(zf_ref[j] == 1)
            def _():
                zero_copy(j).start()
            return 0

        def wait(j, _):
            @pl.when(zf_ref[j] == 1)
            def _():
                zero_copy(j).wait()
            return 0

        lax.fori_loop(0, n_tiles, start, 0)
        lax.fori_loop(0, n_tiles, wait, 0)

    def body(t, _):
        for k in range(MOE_TOPK):
            row = pos_ref[k * n_tok + i * td + t]
            pltpu.make_async_copy(hp_ref.at[pl.ds(t, 1)], xs_hbm.at[pl.ds(row, 1)], sem_s).start()
        return 0

    lax.fori_loop(0, td, body, 0, unroll=2)
    for k in range(MOE_TOPK):
        pltpu.make_async_copy(hp_ref, xs_hbm.at[pl.ds(0, td)], sem_s).wait()


def dispatch(hp, pos, zero_fill, n_tiles, sh_gate, sh_up, sh_down, layer):
    n_tok, half = hp.shape
    tm = EXPERT_TILE
    steps = n_tok // DISPATCH_TILE
    _, d, f = sh_gate.shape
    bf16_rows = 2 * SUBLANES
    nblk = max(n for n in range(1, steps + 1)
               if d % n == 0 and f % n == 0 and (d // n) % bf16_rows == 0 and (f // n) % bf16_rows == 0)
    wblk = lambda i, p, z: (layer, jnp.minimum(i, nblk - 1), 0)
    oblk = lambda i, p, z: (jnp.minimum(i, nblk - 1), 0)
    grid_spec = pltpu.PrefetchScalarGridSpec(
        num_scalar_prefetch=2, grid=(steps,),
        in_specs=[pl.BlockSpec((DISPATCH_TILE, half), lambda i, p, z: (i, 0)),
                  pl.BlockSpec((None, d // nblk, f), wblk),
                  pl.BlockSpec((None, d // nblk, f), wblk),
                  pl.BlockSpec((None, f // nblk, d), wblk)],
        out_specs=[pl.BlockSpec(memory_space=pl.ANY),
                   pl.BlockSpec((d // nblk, f), oblk),
                   pl.BlockSpec((d // nblk, f), oblk),
                   pl.BlockSpec((f // nblk, d), oblk)],
        scratch_shapes=[pltpu.VMEM((tm, half), jnp.uint32), pltpu.SemaphoreType.DMA(()),
                        pltpu.SemaphoreType.DMA(())])
    return pl.pallas_call(
        functools.partial(_dispatch_kernel, n_tok=n_tok, n_tiles=n_tiles), grid_spec=grid_spec,
        out_shape=[jax.ShapeDtypeStruct((n_tiles * tm, half), jnp.uint32),
                   jax.ShapeDtypeStruct((d, f), BF16), jax.ShapeDtypeStruct((d, f), BF16),
                   jax.ShapeDtypeStruct((f, d), BF16)],
        compiler_params=_params(("arbitrary",)), name="moe_dispatch",
    )(pos, zero_fill, hp, sh_gate, sh_up, sh_down)


def _expert_kernel(te_ref, tv_ref, first_ref, slot_ref, next_ref, x_ref, wg_hbm, wu_hbm, wd_hbm, o_ref,
                   stage_g, stage_u, stage_d, wgb, wub, wdb, sems, *, layer):
    i = pl.program_id(0)

    def weight_copies(e, s):
        return (pltpu.make_async_copy(wg_hbm.at[layer, e], stage_g.at[s], sems.at[s]),
                pltpu.make_async_copy(wu_hbm.at[layer, e], stage_u.at[s], sems.at[s]),
                pltpu.make_async_copy(wd_hbm.at[layer, e], stage_d.at[s], sems.at[s]))

    @pl.when(i == 0)
    def _():
        for cp in weight_copies(te_ref[0], 0):
            cp.start()

    @pl.when(first_ref[i] == 1)
    def _():
        s = slot_ref[i]
        for cp in weight_copies(te_ref[i], s):
            cp.wait()

        @pl.when(next_ref[i] >= 0)
        def _():
            for cp in weight_copies(next_ref[i], 1 - s):
                cp.start()

        wgb[...] = stage_g[s].astype(BF16)
        wub[...] = stage_u[s].astype(BF16)
        wdb[...] = stage_d[s].astype(BF16)

    @pl.when(tv_ref[i] == 1)
    def _():
        lo, hi = _unpack_halves(x_ref[...])
        x = jnp.concatenate([lo.astype(BF16), hi.astype(BF16)], axis=1)
        hid = _silu(_dot(x, wgb[...])) * _dot(x, wub[...])
        o_ref[...] = _pack_halves(_dot(hid.astype(BF16), wdb[...]))

    @pl.when(tv_ref[i] == 0)
    def _():
        o_ref[...] = jnp.zeros_like(o_ref)


def routed_experts(xs, tile_e, tile_valid, w_gate, w_up, w_down, layer):
    tm = EXPERT_TILE
    n_tiles = tile_e.shape[0]
    _, n_exp, d, f = w_gate.shape
    prev_e = jnp.concatenate([jnp.full((1,), -1, jnp.int32), tile_e[:-1]])
    first = (tile_e != prev_e).astype(jnp.int32)
    slot = (jnp.cumsum(first) - 1) % 2
    tiles = jnp.arange(n_tiles, dtype=jnp.int32)
    run_start = jnp.where(first == 1, tiles, n_tiles)
    next_start = jnp.min(jnp.where(run_start[None, :] > tiles[:, None], run_start[None, :], n_tiles), axis=1)
    next_e = jnp.where(next_start < n_tiles, tile_e[jnp.minimum(next_start, n_tiles - 1)], -1)
    idx_map = lambda i, *_: (i, 0)
    anyspace = pl.BlockSpec(memory_space=pl.ANY)
    grid_spec = pltpu.PrefetchScalarGridSpec(
        num_scalar_prefetch=5, grid=(n_tiles,),
        in_specs=[pl.BlockSpec((tm, d // 2), idx_map), anyspace, anyspace, anyspace],
        out_specs=pl.BlockSpec((tm, d // 2), idx_map),
        scratch_shapes=[pltpu.VMEM((2, d, f), F32), pltpu.VMEM((2, d, f), F32), pltpu.VMEM((2, f, d), F32),
                        pltpu.VMEM((d, f), BF16), pltpu.VMEM((d, f), BF16), pltpu.VMEM((f, d), BF16),
                        pltpu.SemaphoreType.DMA((2,))])
    return pl.pallas_call(
        functools.partial(_expert_kernel, layer=layer), grid_spec=grid_spec,
        out_shape=jax.ShapeDtypeStruct((n_tiles * tm, d // 2), jnp.uint32),
        compiler_params=_params(("arbitrary",)), name="routed_experts",
    )(tile_e, tile_valid, first, slot.astype(jnp.int32), next_e.astype(jnp.int32), xs, w_gate, w_up, w_down)


def _combine_kernel(pos_ref, ys_hbm, sh_ref, w_ref, x_ref, mod_a_ref, wpost_ref, *refs, n_tok, has_prenorm,
                    gate_idx, shift_idx, scale_idx):
    if has_prenorm:
        mod_b_ref, wpre_ref, xo_ref, h_ref, buf_a, buf_b, sem_a, sem_b = refs
    else:
        xo_ref, buf_a, buf_b, sem_a, sem_b = refs
    i = pl.program_id(0)
    n = pl.num_programs(0)
    tc = buf_a.shape[1]
    half = sh_ref.shape[-1] // 2
    last_tile = 2 * n - 1

    def issue(tile, buf, sem):
        for t in range(tc):
            for k in range(MOE_TOPK):
                p = pos_ref[k * n_tok + tile * tc + t]
                pltpu.make_async_copy(ys_hbm.at[pl.ds(p, 1)], buf.at[k, pl.ds(t, 1)], sem).start()

    def wait(buf, sem):
        for k in range(MOE_TOPK):
            pltpu.make_async_copy(ys_hbm.at[pl.ds(0, tc)], buf.at[k], sem).wait()

    def reduce(buf, rows):
        sh = sh_ref[rows, :].astype(F32)
        w = w_ref[rows, :]
        lo_acc, hi_acc = sh[:, :half], sh[:, half:]
        for k in range(MOE_TOPK):
            lo, hi = _unpack_halves(buf[k])
            lo_acc = lo_acc + w[:, k:k + 1] * lo
            hi_acc = hi_acc + w[:, k:k + 1] * hi
        ffn = jnp.concatenate([lo_acc, hi_acc], axis=1)
        x = x_ref[rows, :] + mod_a_ref[gate_idx:gate_idx + 1, :] * (_rms(ffn) * wpost_ref[...])
        xo_ref[rows, :] = x
        if has_prenorm:
            h = (_rms(x) * wpre_ref[...]) * (1.0 + mod_b_ref[scale_idx:scale_idx + 1, :]) \
                + mod_b_ref[shift_idx:shift_idx + 1, :]
            h_ref[rows, :] = h.astype(BF16)

    @pl.when(i == 0)
    def _():
        issue(0, buf_a, sem_a)

    wait(buf_a, sem_a)
    issue(2 * i + 1, buf_b, sem_b)
    reduce(buf_a, slice(0, tc))
    wait(buf_b, sem_b)
    issue(jnp.minimum(2 * i + 2, last_tile), buf_a, sem_a)
    reduce(buf_b, slice(tc, 2 * tc))

    @pl.when(i == n - 1)
    def _():
        wait(buf_a, sem_a)


def combine(ys, pos, shared, wgt, x, mod_row_map, *, mod_a, w_post, gate_idx, mod_b=None, w_pre=None,
            shift_idx=0, scale_idx=0):
    n_tok, d = shared.shape
    tc = COMBINE_TILE
    step = 2 * tc
    has_prenorm = mod_b is not None
    buf = pltpu.VMEM((MOE_TOPK, tc, d // 2), jnp.uint32)
    row = lambda i, p: (i, 0)
    const = lambda i, p: (0, 0)
    mod_spec = pl.BlockSpec((None, N_ADA, d), lambda i, p: (mod_row_map(i * step), 0, 0))
    vec_spec = pl.BlockSpec((1, d), const)
    args = [pos, ys, shared, wgt, x, mod_a, w_post.reshape(1, d)]
    in_specs = [pl.BlockSpec(memory_space=pl.ANY), pl.BlockSpec((step, d), row),
                pl.BlockSpec((step, MOE_TOPK), row), pl.BlockSpec((step, d), row), mod_spec, vec_spec]
    out_shape = [jax.ShapeDtypeStruct((n_tok, d), F32)]
    out_specs = [pl.BlockSpec((step, d), row)]
    if has_prenorm:
        args += [mod_b, w_pre.reshape(1, d)]
        in_specs += [mod_spec, vec_spec]
        out_shape.append(jax.ShapeDtypeStruct((n_tok, d), BF16))
        out_specs.append(pl.BlockSpec((step, d), row))
    grid_spec = pltpu.PrefetchScalarGridSpec(
        num_scalar_prefetch=1, grid=(n_tok // step,), in_specs=in_specs, out_specs=out_specs,
        scratch_shapes=[buf, buf, pltpu.SemaphoreType.DMA(()), pltpu.SemaphoreType.DMA(())])
    kern = functools.partial(_combine_kernel, n_tok=n_tok, has_prenorm=has_prenorm, gate_idx=gate_idx,
                             shift_idx=shift_idx, scale_idx=scale_idx)
    return pl.pallas_call(
        kern, grid_spec=grid_spec, out_shape=out_shape,
        compiler_params=_params(("arbitrary",)), name="moe_combine",
    )(*args)


def moe_ffn(h, hp, route, w_gate, w_up, w_down, sh_gate, sh_up, sh_down, layer, x, mod_row_map, **epilogue):
    idx_t, wgt_t, rank_t, cnt = route
    pairs = idx_t.shape[0] * idx_t.shape[1]
    tile_e, tile_valid, zero_fill, base = _dispatch_plan(cnt[:, :, 0], pairs, EXPERT_TILE)
    pos = pair_positions(idx_t, rank_t, base).reshape(pairs)
    xs, sg, su, sd = dispatch(hp, pos, zero_fill, tile_e.shape[0], sh_gate, sh_up, sh_down, layer)
    ys = routed_experts(xs, tile_e, tile_valid, w_gate, w_up, w_down, layer)
    shared = shared_expert(h, sg, su, sd)
    return combine(ys, pos, shared, wgt_t.T, x, mod_row_map, **epilogue)


def kernel(x, c, ctx, c_ctx, ada_w, ada_b, norm_pre_mix, norm_post_mix, norm_pre_ffn, norm_post_ffn, ret_w_in, ret_w_out, ret_decay_fwd, ret_decay_bwd, diff_w_in, diff_w_out, diff_lam_q1, diff_lam_k1, diff_lam_q2, diff_lam_k2, diff_subln_w, moe_router_w, moe_router_b, moe_w_gate, moe_w_up, moe_w_down, moe_shared_gate, moe_shared_up, moe_shared_down):
    batch, t_len, d = x.shape
    c_len = ctx.shape[1]
    s_len = t_len + c_len
    depth = ada_w.shape[0]
    assert depth == 2 and batch + 1 <= MOD_ROWS
    assert t_len % ROW_TILE == 0 and c_len % ROW_TILE == 0 and t_len % GRID_W == 0
    ret_heads = ret_decay_fwd.shape[-1]
    diff_heads = d // (2 * DIFF_HEAD_DIM)
    lat_tiles, all_tiles = t_len // ROW_TILE, s_len // ROW_TILE

    cc = jnp.concatenate([c, c_ctx[None], jnp.zeros((MOD_ROWS - batch - 1, d), F32)], axis=0)
    mods = ada_modulation(cc, ada_w, ada_b)
    rope_ret = _rope_tables(t_len, d // ret_heads)
    rope_diff = _rope_tables(t_len, DIFF_HEAD_DIM)

    ident = lambda i: i
    uni_mod = lambda i: jnp.where(i % all_tiles < lat_tiles, i // all_tiles, batch)
    lat_mod = lambda i: i // lat_tiles
    uni_row_mod = lambda r: jnp.where(r % s_len < t_len, r // s_len, batch)
    lat_row_mod = lambda r: r // t_len
    lat_of_uni = lambda i: (i // lat_tiles) * all_tiles + i % lat_tiles

    xs, h = ingest(x, ctx, uni_mod, mods[0], norm_pre_mix[0], 0, 1)
    qkvg = matmul(h, ret_w_in, 0)
    lg_f = jax.nn.log_sigmoid(ret_decay_fwd[0].astype(F32))
    lg_b = jax.nn.log_sigmoid(ret_decay_bwd[0].astype(F32))
    r = retention(qkvg, lg_f, lg_b, *rope_ret, batch=batch, t_len=t_len, c_len=c_len, heads=ret_heads)
    y = matmul(r, ret_w_out, 0)
    xs, h, hp, *route = mixer_norm(
        xs, ident, batch * all_tiles, uni_mod, y, mods[0], norm_post_mix[0], norm_pre_ffn[0],
        moe_router_w[0].T, moe_router_b[0], gate_idx=2, shift_idx=3, scale_idx=4)
    xs, h = moe_ffn(h, hp, route, moe_w_gate, moe_w_up, moe_w_down,
                    moe_shared_gate, moe_shared_up, moe_shared_down, 0, xs, uni_row_mod,
                    mod_a=mods[0], w_post=norm_post_ffn[0], gate_idx=5,
                    mod_b=mods[1], w_pre=norm_pre_mix[1], shift_idx=0, scale_idx=1)

    qkv = matmul(h, diff_w_in, 0)
    lam_vecs = jnp.stack([diff_lam_q1[0], diff_lam_k1[0], diff_lam_q2[0], diff_lam_k2[0]]).astype(F32)
    lambda_init = 0.8 - 0.6 * math.exp(-0.3 * 1)
    a = diff_attention(qkv, lam_vecs, diff_subln_w[0], *rope_diff, batch=batch, t_len=t_len, c_len=c_len,
                       heads=diff_heads, lambda_init=lambda_init)
    y = matmul(a, diff_w_out, 0)
    xl, h, hp, *route = mixer_norm(
        xs, lat_of_uni, batch * lat_tiles, lat_mod, y, mods[1], norm_post_mix[1], norm_pre_ffn[1],
        moe_router_w[1].T, moe_router_b[1], gate_idx=2, shift_idx=3, scale_idx=4)
    (out,) = moe_ffn(h, hp, route, moe_w_gate, moe_w_up, moe_w_down,
                     moe_shared_gate, moe_shared_up, moe_shared_down, 1, xl, lat_row_mod,
                     mod_a=mods[1], w_post=norm_post_ffn[1], gate_idx=5)
    return out.reshape(batch, t_len, d)
```

```python
import functools
import math

import jax
import jax.numpy as jnp
from jax import lax
from jax.experimental import pallas as pl
from jax.experimental.pallas import tpu as pltpu

GRID_W = 64
N_ADA = 6
NORM_EPS = 1e-6
ROPE_BASE = 10000.0
RET_BLOCK = 256
DIFF_HEAD_DIM = 128
MOE_TOPK = 8
MOE_GROUPS = 8
MOE_TOPK_GROUPS = 4
ROUTED_SCALE = 2.5

LANES = 128
SUBLANES = 8
MOD_ROWS = SUBLANES
ROW_TILE = 256
EXPERT_TILE = 256
COMBINE_TILE = 64
DISPATCH_TILE = 256
ATTN_Q_TILES = 8
VMEM_LIMIT = 56 * 1024 * 1024
MATMUL_VMEM_BUDGET = 53 * 1024 * 1024
MATMUL_VMEM_LIMIT = 60 * 1024 * 1024

F32 = jnp.float32
BF16 = jnp.bfloat16


def _pick(dim, candidates):
    for c in candidates:
        if dim % c == 0:
            return c
    raise ValueError(f"no tile in {candidates} divides {dim}")


def _params(sem, vmem=VMEM_LIMIT):
    return pltpu.CompilerParams(dimension_semantics=sem, vmem_limit_bytes=vmem)


def _dot(a, b):
    return jnp.dot(a, b, preferred_element_type=F32)


def _dot_nt(a, b):
    return lax.dot_general(a, b, (((1,), (1,)), ((), ())), preferred_element_type=F32)


def _dot_tn(a, b):
    return lax.dot_general(a, b, (((0,), (0,)), ((), ())), preferred_element_type=F32)


def _silu(x):
    return x * jax.nn.sigmoid(x)


def _pack_halves(y):
    w = y.shape[-1] // 2
    lo = lax.bitcast_convert_type(y[:, :w].astype(BF16).astype(F32), jnp.uint32)
    hi = lax.bitcast_convert_type(y[:, w:].astype(BF16).astype(F32), jnp.uint32)
    return (hi & jnp.uint32(0xFFFF0000)) | (lo >> 16)


def _unpack_halves(p):
    lo = lax.bitcast_convert_type(p << 16, F32)
    hi = lax.bitcast_convert_type(p & jnp.uint32(0xFFFF0000), F32)
    return lo, hi


def _ada_kernel(c_ref, w_ref, b_ref, o_ref):
    a = _silu(c_ref[...]).astype(BF16)
    o_ref[...] = _dot(a, w_ref[...].astype(BF16)) + b_ref[...]


def ada_modulation(cc, ada_w, ada_b):
    depth, d, n = ada_w.shape
    tn = _pick(n, (512, 256, 128))
    out = pl.pallas_call(
        _ada_kernel,
        grid=(depth, n // tn),
        in_specs=[pl.BlockSpec((MOD_ROWS, d), lambda l, j: (0, 0)),
                  pl.BlockSpec((None, d, tn), lambda l, j: (l, 0, j)),
                  pl.BlockSpec((None, 1, tn), lambda l, j: (l, 0, j))],
        out_specs=pl.BlockSpec((None, MOD_ROWS, tn), lambda l, j: (l, 0, j)),
        out_shape=jax.ShapeDtypeStruct((depth, MOD_ROWS, n), F32),
        compiler_params=_params(("parallel", "parallel")),
        name="ada_modulation",
    )(cc, ada_w, ada_b.reshape(depth, 1, n))
    return out.reshape(depth, MOD_ROWS, N_ADA, d)


def _rms(x):
    return x * lax.rsqrt(jnp.mean(x * x, axis=-1, keepdims=True) + NORM_EPS)


def _route(h, rw_ref, rb_ref, idx_ref, wgt_ref, rank_ref, cnt_ref):
    n_exp = rw_ref.shape[0]
    tm = h.shape[0]
    per_group = n_exp // MOE_GROUPS
    w = rw_ref[...]
    w_hi = w.astype(BF16)
    w_lo = (w - w_hi.astype(F32)).astype(BF16)
    h_hi = h.astype(BF16)
    h_lo = (h - h_hi.astype(F32)).astype(BF16)
    logits = _dot_nt(w_hi, h_hi) + (_dot_nt(w_hi, h_lo) + _dot_nt(w_lo, h_hi))
    scores = jax.nn.sigmoid(logits)
    biased = scores + rb_ref[...]
    neg = jnp.float32(-jnp.inf)
    sub = lax.broadcasted_iota(jnp.int32, (per_group, tm), 0)
    giota = lax.broadcasted_iota(jnp.int32, (MOE_GROUPS, tm), 0)
    gs = jnp.zeros((MOE_GROUPS, tm), F32)
    for g in range(MOE_GROUPS):
        blk = biased[g * per_group:(g + 1) * per_group]
        m1 = jnp.max(blk, axis=0, keepdims=True)
        i1 = jnp.min(jnp.where(blk == m1, sub, per_group), axis=0, keepdims=True)
        m2 = jnp.max(jnp.where(sub == i1, neg, blk), axis=0, keepdims=True)
        gs = jnp.where(giota == g, m1 + m2, gs)
    rank = jnp.zeros((MOE_GROUPS, tm), jnp.int32)
    for j in range(MOE_GROUPS):
        gj = gs[j:j + 1]
        beats = (gj > gs) | ((gj == gs) & (giota > j))
        rank = rank + beats.astype(jnp.int32)
    keep = (rank < MOE_TOPK_GROUPS).astype(F32)
    keep_e = jnp.concatenate(
        [jnp.broadcast_to(keep[g:g + 1], (per_group, tm)) for g in range(MOE_GROUPS)], axis=0)
    masked = jnp.where(keep_e > 0.5, biased, neg)
    eiota = lax.broadcasted_iota(jnp.int32, (n_exp, tm), 0)
    sel_w, hits = [], []
    for k in range(MOE_TOPK):
        m = jnp.max(masked, axis=0, keepdims=True)
        idx = jnp.min(jnp.where(masked == m, eiota, n_exp), axis=0, keepdims=True)
        hit = eiota == idx
        hits.append(hit)
        sel_w.append(jnp.sum(jnp.where(hit, scores, 0.0), axis=0, keepdims=True))
        masked = jnp.where(hit, neg, masked)
        idx_ref[k:k + 1, :] = idx
    total = sel_w[0]
    for k in range(1, MOE_TOPK):
        total = total + sel_w[k]
    for k in range(MOE_TOPK):
        wgt_ref[k:k + 1, :] = sel_w[k] / total * ROUTED_SCALE
    chosen = jnp.zeros((n_exp, tm), F32)
    for k in range(MOE_TOPK):
        chosen = jnp.where(hits[k], 1.0, chosen)
    before = (lax.broadcasted_iota(jnp.int32, (tm, tm), 0)
              < lax.broadcasted_iota(jnp.int32, (tm, tm), 1)).astype(BF16)
    prefix = _dot(chosen.astype(BF16), before)
    for k in range(MOE_TOPK):
        rank_ref[k:k + 1, :] = jnp.sum(jnp.where(hits[k], prefix, 0.0), axis=0,
                                       keepdims=True).astype(jnp.int32)
    cnt_ref[...] = jnp.broadcast_to(jnp.sum(chosen, axis=1, keepdims=True),
                                    cnt_ref.shape).astype(jnp.int32)


def _mixer_norm_kernel(x_ref, y_ref, mod_ref, wpost_ref, wpre_ref, rw_ref, rb_ref,
                       xo_ref, h_ref, hp_ref, idx_ref, wgt_ref, rank_ref, cnt_ref, *, gate_idx, shift_idx,
                       scale_idx):
    x = x_ref[...] + mod_ref[gate_idx:gate_idx + 1, :] * (_rms(y_ref[...].astype(F32)) * wpost_ref[...])
    xo_ref[...] = x
    h = (_rms(x) * wpre_ref[...]) * (1.0 + mod_ref[scale_idx:scale_idx + 1, :]) \
        + mod_ref[shift_idx:shift_idx + 1, :]
    h_ref[...] = h.astype(BF16)
    hp_ref[...] = _pack_halves(h)
    _route(h, rw_ref, rb_ref, idx_ref, wgt_ref, rank_ref, cnt_ref)


def _ingest_kernel(x_ref, c_ref, mod_ref, wpre_ref, xs_ref, h_ref, *, lat_tiles, all_tiles, shift_idx,
                   scale_idx):
    is_latent = pl.program_id(0) % all_tiles < lat_tiles

    def emit(x):
        xs_ref[...] = x
        h = (_rms(x) * wpre_ref[...]) * (1.0 + mod_ref[scale_idx:scale_idx + 1, :]) \
            + mod_ref[shift_idx:shift_idx + 1, :]
        h_ref[...] = h.astype(BF16)

    @pl.when(is_latent)
    def _():
        emit(x_ref[...])

    @pl.when(jnp.logical_not(is_latent))
    def _():
        emit(c_ref[...])


def ingest(x, ctx, mod_row_map, mod, w_pre, shift_idx, scale_idx):
    batch, t_len, d = x.shape
    c_len = ctx.shape[1]
    tm = ROW_TILE
    lat_tiles, ctx_tiles = t_len // tm, c_len // tm
    all_tiles = lat_tiles + ctx_tiles
    x_map = lambda i: ((i // all_tiles) * lat_tiles + jnp.minimum(i % all_tiles, lat_tiles - 1), 0)
    c_map = lambda i: ((i // all_tiles) * ctx_tiles + jnp.maximum(i % all_tiles - lat_tiles, 0), 0)
    row = lambda i: (i, 0)
    kern = functools.partial(_ingest_kernel, lat_tiles=lat_tiles, all_tiles=all_tiles, shift_idx=shift_idx,
                             scale_idx=scale_idx)
    n_rows = batch * all_tiles * tm
    return pl.pallas_call(
        kern, grid=(batch * all_tiles,),
        in_specs=[pl.BlockSpec((tm, d), x_map), pl.BlockSpec((tm, d), c_map),
                  pl.BlockSpec((None, N_ADA, d), lambda i: (mod_row_map(i), 0, 0)),
                  pl.BlockSpec((1, d), lambda i: (0, 0))],
        out_specs=[pl.BlockSpec((tm, d), row), pl.BlockSpec((tm, d), row)],
        out_shape=[jax.ShapeDtypeStruct((n_rows, d), F32), jax.ShapeDtypeStruct((n_rows, d), BF16)],
        compiler_params=_params(("arbitrary",)), name="ingest",
    )(x.reshape(batch * t_len, d), ctx.reshape(batch * c_len, d), mod, w_pre.reshape(1, d))


def mixer_norm(x, x_tile_map, n_out_tiles, mod_row_map, y, mod, w_post, w_pre, router_wt, router_b, *,
               gate_idx, shift_idx, scale_idx):
    d = x.shape[-1]
    tm = ROW_TILE
    n_exp = router_wt.shape[0]
    n_rows = n_out_tiles * tm
    row = lambda i: (i, 0)
    const = lambda i: (0, 0)
    per_token = pl.BlockSpec((MOE_TOPK, tm), lambda i: (0, i))
    kern = functools.partial(_mixer_norm_kernel, gate_idx=gate_idx, shift_idx=shift_idx, scale_idx=scale_idx)
    return pl.pallas_call(
        kern, grid=(n_out_tiles,),
        in_specs=[pl.BlockSpec((tm, d), lambda i: (x_tile_map(i), 0)), pl.BlockSpec((tm, d), row),
                  pl.BlockSpec((None, N_ADA, d), lambda i: (mod_row_map(i), 0, 0)),
                  pl.BlockSpec((1, d), const), pl.BlockSpec((1, d), const),
                  pl.BlockSpec((n_exp, d), const), pl.BlockSpec((n_exp, 1), const)],
        out_specs=[pl.BlockSpec((tm, d), row), pl.BlockSpec((tm, d), row), pl.BlockSpec((tm, d // 2), row),
                   per_token, per_token, per_token,
                   pl.BlockSpec((None, n_exp, LANES), lambda i: (i, 0, 0))],
        out_shape=[jax.ShapeDtypeStruct((n_rows, d), F32), jax.ShapeDtypeStruct((n_rows, d), BF16),
                   jax.ShapeDtypeStruct((n_rows, d // 2), jnp.uint32),
                   jax.ShapeDtypeStruct((MOE_TOPK, n_rows), jnp.int32),
                   jax.ShapeDtypeStruct((MOE_TOPK, n_rows), F32),
                   jax.ShapeDtypeStruct((MOE_TOPK, n_rows), jnp.int32),
                   jax.ShapeDtypeStruct((n_out_tiles, n_exp, LANES), jnp.int32)],
        compiler_params=_params(("parallel",)), name="mixer_norm",
    )(x, y, mod, w_post.reshape(1, d), w_pre.reshape(1, d), router_wt, router_b.reshape(n_exp, 1))


def _mm_kernel(a_ref, w_ref, o_ref):
    o_ref[...] = _dot(a_ref[...], w_ref[...].astype(BF16)).astype(o_ref.dtype)


def _mm_acc_kernel(a_ref, w_ref, o_ref, acc_ref):
    k = pl.program_id(2)

    @pl.when(k == 0)
    def _():
        acc_ref[...] = jnp.zeros_like(acc_ref)

    acc_ref[...] += _dot(a_ref[...], w_ref[...].astype(BF16))

    @pl.when(k == pl.num_programs(2) - 1)
    def _():
        o_ref[...] = acc_ref[...].astype(o_ref.dtype)


def matmul(a, w, layer, out_dtype=BF16):
    m, k = a.shape
    n = w.shape[2]
    tm = _pick(m, (1024, 768, 512, 256))

    def vmem_bytes(tn, tk):
        return (2 * tm * tk * 2 + 2 * tk * tn * 4 + tk * tn * 2 + 2 * tm * tn * 2
                + (tm * tn * 4 if tk < k else 0))

    tn, tk = next(((tn, tk) for tk in (k, k // 2, k // 4) for tn in (512, 256)
                   if n % tn == 0 and tk % LANES == 0 and vmem_bytes(tn, tk) <= MATMUL_VMEM_BUDGET),
                  (_pick(n, (256, 128)), _pick(k, (1024, 512, 256, 128))))
    if tk == k:
        return pl.pallas_call(
            _mm_kernel, grid=(m // tm, n // tn),
            in_specs=[pl.BlockSpec((tm, k), lambda i, j: (i, 0)),
                      pl.BlockSpec((None, k, tn), lambda i, j: (layer, 0, j))],
            out_specs=pl.BlockSpec((tm, tn), lambda i, j: (i, j)),
            out_shape=jax.ShapeDtypeStruct((m, n), out_dtype),
            compiler_params=_params(("parallel", "parallel"), MATMUL_VMEM_LIMIT), name="matmul",
        )(a, w)
    return pl.pallas_call(
        _mm_acc_kernel, grid=(m // tm, n // tn, k // tk),
        in_specs=[pl.BlockSpec((tm, tk), lambda i, j, l: (i, l)),
                  pl.BlockSpec((None, tk, tn), lambda i, j, l: (layer, l, j))],
        out_specs=pl.BlockSpec((tm, tn), lambda i, j, l: (i, j)),
        out_shape=jax.ShapeDtypeStruct((m, n), out_dtype),
        scratch_shapes=[pltpu.VMEM((tm, tn), F32)],
        compiler_params=_params(("parallel", "parallel", "arbitrary")), name="matmul_acc",
    )(a, w)


def _rope_tables(t_len, head_dim):
    rows = t_len // GRID_W
    n_freq = head_dim // 4
    row, col = jnp.meshgrid(jnp.arange(rows, dtype=F32), jnp.arange(GRID_W, dtype=F32), indexing="ij")
    inv_freq = ROPE_BASE ** (-jnp.arange(n_freq, dtype=F32) / n_freq)
    ang_r = row.reshape(-1, 1) * inv_freq
    ang_c = col.reshape(-1, 1) * inv_freq
    cr, sr, cc, sc = jnp.cos(ang_r), jnp.sin(ang_r), jnp.cos(ang_c), jnp.sin(ang_c)
    return (jnp.concatenate([cr, cr, cc, cc], axis=-1), jnp.concatenate([-sr, sr, -sc, sc], axis=-1))


def _swap_quarters(x, quarter):
    lane = lax.broadcasted_iota(jnp.int32, x.shape, 1)
    first = (lane % (2 * quarter)) < quarter
    return jnp.where(first, pltpu.roll(x, LANES - quarter, axis=1), pltpu.roll(x, quarter, axis=1))


def _retention_kernel(lgf_ref, lgb_ref, q_ref, k_ref, v_ref, g_ref, cos_ref, sin_ref, o_ref,
                      qr, kr, oacc_f, oacc_b, state_f, state_b, *, t_len, c_len):
    head = pl.program_id(1)
    L = RET_BLOCK
    dk = q_ref.shape[-1]
    k_scale = dk ** -0.5

    def rope(x, rows):
        sw = jnp.concatenate([pltpu.roll(x[:, :LANES], LANES // 2, axis=1),
                              pltpu.roll(x[:, LANES:], LANES // 2, axis=1)], axis=1)
        return x * cos_ref[rows, :] + sw * sin_ref[rows, :]

    def prepare(row0):
        rows = slice(row0, row0 + L)
        if row0 < t_len:
            qr[rows, :] = rope(q_ref[rows, :].astype(F32), rows).astype(BF16)
            kr[rows, :] = (rope(k_ref[rows, :].astype(F32), rows) * k_scale).astype(BF16)
        else:
            qr[rows, :] = q_ref[rows, :]
            kr[rows, :] = (k_ref[rows, :].astype(F32) * k_scale).astype(BF16)

    ii = lax.broadcasted_iota(jnp.int32, (L, L), 0)
    jj = lax.broadcasted_iota(jnp.int32, (L, L), 1)
    rel = (ii - jj).astype(F32)
    idx = lax.broadcasted_iota(jnp.int32, (L, 1), 0).astype(F32)

    lg_f, lg_b = lgf_ref[head], lgb_ref[head]
    fwd = (jnp.where(rel >= 0, jnp.exp(lg_f * jnp.maximum(rel, 0.0)), 0.0),
           jnp.exp(lg_f * (idx + 1.0)),
           jnp.exp(lg_f * (L - 1.0 - idx)),
           jnp.exp(lg_f * L), state_f, oacc_f)
    bwd = (jnp.where(rel <= 0, jnp.exp(lg_b * jnp.maximum(-rel, 0.0)), 0.0),
           jnp.exp(lg_b * (L - idx)),
           jnp.exp(lg_b * idx),
           jnp.exp(lg_b * L), state_b, oacc_b)

    def chunk(row0, direction):
        dmat, q_decay, k_decay, chunk_decay, state, oacc = direction
        rows = slice(row0, row0 + L)
        qb, kb, vb = qr[rows, :], kr[rows, :], v_ref[rows, :]
        scores = _dot_nt(qb, kb) * dmat
        inner = _dot(scores.astype(BF16), vb)
        st = state[...]
        cross = _dot(qb, st.astype(BF16)) * q_decay
        oacc[rows, :] = inner + cross
        kd = (kb.astype(F32) * k_decay).astype(BF16)
        state[...] = st * chunk_decay + _dot_tn(kd, vb)

    def finish(row0):
        rows = slice(row0, row0 + L)
        o = _rms(oacc_f[rows, :] + oacc_b[rows, :])
        o_ref[rows, :] = (_silu(g_ref[rows, :].astype(F32)) * o).astype(BF16)

    def scan(r0, n_chunks):
        ready, done_f, done_b = set(), set(), set()
        for ci in range(n_chunks):
            cf, cb = ci, n_chunks - 1 - ci
            for c in (cf, cb):
                if c not in ready:
                    prepare(r0 + c * L)
                    ready.add(c)
            chunk(r0 + cf * L, fwd)
            chunk(r0 + cb * L, bwd)
            done_f.add(cf)
            done_b.add(cb)
            for c in sorted({cf, cb}):
                if c in done_f and c in done_b:
                    finish(r0 + c * L)

    state_f[...] = jnp.zeros_like(state_f)
    state_b[...] = jnp.zeros_like(state_b)
    scan(t_len, c_len // L)
    scan(0, t_len // L)


def retention(qkvg, lg_f, lg_b, cos, sin, *, batch, t_len, c_len, heads):
    s_len = t_len + c_len
    dk = cos.shape[-1]
    dv = 2 * dk
    assert dk == 2 * LANES
    kern = functools.partial(_retention_kernel, t_len=t_len, c_len=c_len)
    grid_spec = pltpu.PrefetchScalarGridSpec(
        num_scalar_prefetch=2, grid=(batch, heads),
        in_specs=[pl.BlockSpec((s_len, dk), lambda b, h, *_: (b, h)),
                  pl.BlockSpec((s_len, dk), lambda b, h, *_: (b, heads + h)),
                  pl.BlockSpec((s_len, dv), lambda b, h, *_: (b, heads + h)),
                  pl.BlockSpec((s_len, dv), lambda b, h, *_: (b, 2 * heads + h)),
                  pl.BlockSpec((t_len, dk), lambda b, h, *_: (0, 0)),
                  pl.BlockSpec((t_len, dk), lambda b, h, *_: (0, 0))],
        out_specs=pl.BlockSpec((s_len, dv), lambda b, h, *_: (b, h)),
        scratch_shapes=[pltpu.VMEM((s_len, dk), BF16), pltpu.VMEM((s_len, dk), BF16),
                        pltpu.VMEM((s_len, dv), F32), pltpu.VMEM((s_len, dv), F32),
                        pltpu.VMEM((dk, dv), F32), pltpu.VMEM((dk, dv), F32)])
    return pl.pallas_call(
        kern, grid_spec=grid_spec,
        out_shape=jax.ShapeDtypeStruct((batch * s_len, heads * dv), BF16),
        compiler_params=_params(("parallel", "parallel")), name="retention",
    )(lg_f, lg_b, qkvg, qkvg, qkvg, qkvg, cos, sin)


def _diff_attn_kernel(lam_ref, *refs, n_q, t_len, c_len, lambda_init):
    q_refs = refs[:n_q]
    k_ref, v_ref, cosq_ref, sinq_ref, cosk_ref, sink_ref, subln_ref, o_ref, kr = refs[n_q:]
    hd = DIFF_HEAD_DIM
    quarter = hd // 4
    rt = ROW_TILE

    def rope(x, cos, sin):
        return x * cos + _swap_quarters(x, quarter) * sin

    @pl.when(pl.program_id(2) == 0)
    def _():
        def body(i, _):
            rows = pl.ds(pl.multiple_of(i * rt, rt), rt)
            for c in range(2):
                cols = slice(c * hd, (c + 1) * hd)
                kr[rows, cols] = rope(k_ref[rows, cols].astype(F32), cosk_ref[rows, :],
                                      sink_ref[rows, :]).astype(BF16)
            return 0
        lax.fori_loop(0, t_len // rt, body, 0)
        ctx_rows = pl.ds(t_len, c_len)
        kr[ctx_rows, :] = k_ref[ctx_rows, :]

    lam_v = lam_ref[...]
    lam = (jnp.exp(jnp.sum(lam_v[0:1] * lam_v[1:2], axis=-1, keepdims=True))
           - jnp.exp(jnp.sum(lam_v[2:3] * lam_v[3:4], axis=-1, keepdims=True)) + lambda_init)
    q_scale = (hd ** -0.5) * math.log2(math.e)
    tq = q_refs[0].shape[0]
    for part, q_ref in enumerate(q_refs):
        rows = slice(part * tq, (part + 1) * tq)
        outs = []
        for c in range(2):
            cols = slice(c * hd, (c + 1) * hd)
            qc = (rope(q_ref[:, cols].astype(F32), cosq_ref[rows, :], sinq_ref[rows, :])
                  * q_scale).astype(BF16)
            s = _dot_nt(qc, kr[:, cols])
            e = jnp.exp2(s - jnp.max(s, axis=-1, keepdims=True))
            denom = jnp.sum(e, axis=-1, keepdims=True)
            outs.append(_dot(e.astype(BF16), v_ref[...]) / denom)
        o = outs[0] - lam * outs[1]
        o_ref[rows, :] = ((_rms(o) * subln_ref[...]) * (1.0 - lambda_init)).astype(BF16)


def diff_attention(qkv, lam_vecs, subln_w, cos, sin, *, batch, t_len, c_len, heads, lambda_init):
    s_len = t_len + c_len
    hd = DIFF_HEAD_DIM
    tq = ROW_TILE
    n_q = _pick(t_len // tq, (ATTN_Q_TILES, 2, 1))
    steps, s_tiles = t_len // (n_q * tq), s_len // tq
    kern = functools.partial(_diff_attn_kernel, n_q=n_q, t_len=t_len, c_len=c_len, lambda_init=lambda_init)
    q_specs = [pl.BlockSpec((tq, 2 * hd), lambda b, h, i, j=j: (b * s_tiles + n_q * i + j, h))
               for j in range(n_q)]
    return pl.pallas_call(
        kern, grid=(batch, heads, steps),
        in_specs=[pl.BlockSpec((4, hd), lambda b, h, i: (0, 0)), *q_specs,
                  pl.BlockSpec((s_len, 2 * hd), lambda b, h, i: (b, heads + h)),
                  pl.BlockSpec((s_len, 2 * hd), lambda b, h, i: (b, 2 * heads + h)),
                  pl.BlockSpec((n_q * tq, hd), lambda b, h, i: (i, 0)),
                  pl.BlockSpec((n_q * tq, hd), lambda b, h, i: (i, 0)),
                  pl.BlockSpec((t_len, hd), lambda b, h, i: (0, 0)),
                  pl.BlockSpec((t_len, hd), lambda b, h, i: (0, 0)),
                  pl.BlockSpec((1, 2 * hd), lambda b, h, i: (0, 0))],
        out_specs=pl.BlockSpec((n_q * tq, 2 * hd), lambda b, h, i: (b * steps + i, h)),
        out_shape=jax.ShapeDtypeStruct((batch * t_len, heads * 2 * hd), BF16),
        scratch_shapes=[pltpu.VMEM((s_len, 2 * hd), BF16)],
        compiler_params=_params(("parallel", "parallel", "arbitrary")), name="diff_attention",
    )(lam_vecs, *([qkv] * (n_q + 2)), cos, sin, cos, sin, subln_w.reshape(1, 2 * hd))


def _shared_kernel(x_ref, wg_ref, wu_ref, wd_ref, o_ref, acc_ref):
    f = pl.program_id(1)

    @pl.when(f == 0)
    def _():
        acc_ref[...] = jnp.zeros_like(acc_ref)

    x = x_ref[...]
    hid = _silu(_dot(x, wg_ref[...])) * _dot(x, wu_ref[...])
    acc_ref[...] += _dot(hid.astype(BF16), wd_ref[...])

    @pl.when(f == pl.num_programs(1) - 1)
    def _():
        o_ref[...] = acc_ref[...].astype(o_ref.dtype)


def shared_expert(h, w_gate, w_up, w_down):
    m, d = h.shape
    f_dim = w_gate.shape[1]
    tm = _pick(m, (512, 256))
    tf = _pick(f_dim, (512, 256, 128))
    return pl.pallas_call(
        _shared_kernel, grid=(m // tm, f_dim // tf),
        in_specs=[pl.BlockSpec((tm, d), lambda i, f: (i, 0)),
                  pl.BlockSpec((d, tf), lambda i, f: (0, f)),
                  pl.BlockSpec((d, tf), lambda i, f: (0, f)),
                  pl.BlockSpec((tf, d), lambda i, f: (f, 0))],
        out_specs=pl.BlockSpec((tm, d), lambda i, f: (i, 0)),
        out_shape=jax.ShapeDtypeStruct((m, d), BF16),
        scratch_shapes=[pltpu.VMEM((tm, d), F32)],
        compiler_params=_params(("parallel", "arbitrary")), name="shared_expert",
    )(h, w_gate, w_up, w_down)


def _dispatch_plan(cnt, pairs, tm):
    n_exp = cnt.shape[1]
    total = jnp.sum(cnt, axis=0)
    padded = (total + tm - 1) // tm * tm
    pend = jnp.cumsum(padded)
    base = (pend - padded)[None, :] + jnp.cumsum(cnt, axis=0) - cnt
    n_tiles = pairs // tm + n_exp
    tile_start = jnp.arange(n_tiles, dtype=jnp.int32) * tm
    valid = tile_start < pend[-1]
    tile_e = jnp.minimum(jnp.sum((tile_start[:, None] >= pend[None, :]).astype(jnp.int32), axis=1), n_exp - 1)
    tile_e = jnp.where(valid, tile_e, jnp.max(jnp.where(valid, tile_e, 0)))
    next_e = jnp.concatenate([tile_e[1:], jnp.full((1,), -1, jnp.int32)])
    next_valid = jnp.concatenate([valid[1:], jnp.zeros((1,), bool)])
    zero_fill = (~valid) | (tile_e != next_e) | (~next_valid)
    return (tile_e.astype(jnp.int32), valid.astype(jnp.int32), zero_fill.astype(jnp.int32),
            base.astype(jnp.int32))


def _positions_kernel(idx_ref, rank_ref, base_ref, pos_ref):
    n_exp = base_ref.shape[0]
    tm = idx_ref.shape[1]
    eiota = lax.broadcasted_iota(jnp.int32, (n_exp, tm), 0)
    base = jnp.broadcast_to(base_ref[...].astype(F32), (n_exp, tm))
    for k in range(MOE_TOPK):
        hit = eiota == idx_ref[k:k + 1, :]
        first = jnp.sum(jnp.where(hit, base, 0.0), axis=0, keepdims=True)
        pos_ref[k:k + 1, :] = first.astype(jnp.int32) + rank_ref[k:k + 1, :]


def pair_positions(idx_t, rank_t, base):
    k, n = idx_t.shape
    tiles, n_exp = base.shape
    tm = n // tiles
    spec = pl.BlockSpec((k, tm), lambda i: (0, i))
    return pl.pallas_call(
        _positions_kernel, grid=(tiles,),
        in_specs=[spec, spec, pl.BlockSpec((None, n_exp, 1), lambda i: (i, 0, 0))],
        out_specs=spec, out_shape=jax.ShapeDtypeStruct((k, n), jnp.int32),
        compiler_params=_params(("parallel",)), name="pair_positions",
    )(idx_t, rank_t, base.reshape(tiles, n_exp, 1))


def _dispatch_kernel(pos_ref, zf_ref, hp_ref, wg_ref, wu_ref, wd_ref, xs_hbm, wgo_ref, wuo_ref, wdo_ref,
                     zeros, sem_z, sem_s, *, n_tok, n_tiles):
    i = pl.program_id(0)
    td = hp_ref.shape[0]
    tm = zeros.shape[0]
    wgo_ref[...] = wg_ref[...].astype(BF16)
    wuo_ref[...] = wu_ref[...].astype(BF16)
    wdo_ref[...] = wd_ref[...].astype(BF16)

    def zero_copy(j):
        return pltpu.make_async_copy(zeros, xs_hbm.at[pl.ds(pl.multiple_of(j * tm, tm), tm)], sem_z)

    @pl.when(i == 0)
    def _():
        zeros[...] = jnp.zeros_like(zeros)

        def start(j, _):
            @pl.when(zf_ref[j] == 1)
            def _():
                zero_copy(j).start()
            return 0

        def wait(j, _):
            @pl.when(zf_ref[j] == 1)
            def _():
                zero_copy(j).wait()
            return 0

        lax.fori_loop(0, n_tiles, start, 0)
        lax.fori_loop(0, n_tiles, wait, 0)

    def body(t, _):
        for k in range(MOE_TOPK):
            row = pos_ref[k * n_tok + i * td + t]
            pltpu.make_async_copy(hp_ref.at[pl.ds(t, 1)], xs_hbm.at[pl.ds(row, 1)], sem_s).start()
        return 0

    lax.fori_loop(0, td, body, 0, unroll=2)
    for k in range(MOE_TOPK):
        pltpu.make_async_copy(hp_ref, xs_hbm.at[pl.ds(0, td)], sem_s).wait()


def dispatch(hp, pos, zero_fill, n_tiles, sh_gate, sh_up, sh_down, layer):
    n_tok, half = hp.shape
    tm = EXPERT_TILE
    steps = n_tok // DISPATCH_TILE
    _, d, f = sh_gate.shape
    bf16_rows = 2 * SUBLANES
    nblk = max(n for n in range(1, steps + 1)
               if d % n == 0 and f % n == 0 and (d // n) % bf16_rows == 0 and (f // n) % bf16_rows == 0)
    wblk = lambda i, p, z: (layer, jnp.minimum(i, nblk - 1), 0)
    oblk = lambda i, p, z: (jnp.minimum(i, nblk - 1), 0)
    grid_spec = pltpu.PrefetchScalarGridSpec(
        num_scalar_prefetch=2, grid=(steps,),
        in_specs=[pl.BlockSpec((DISPATCH_TILE, half), lambda i, p, z: (i, 0)),
                  pl.BlockSpec((None, d // nblk, f), wblk),
                  pl.BlockSpec((None, d // nblk, f), wblk),
                  pl.BlockSpec((None, f // nblk, d), wblk)],
        out_specs=[pl.BlockSpec(memory_space=pl.ANY),
                   pl.BlockSpec((d // nblk, f), oblk),
                   pl.BlockSpec((d // nblk, f), oblk),
                   pl.BlockSpec((f // nblk, d), oblk)],
        scratch_shapes=[pltpu.VMEM((tm, half), jnp.uint32), pltpu.SemaphoreType.DMA(()),
                        pltpu.SemaphoreType.DMA(())])
    return pl.pallas_call(
        functools.partial(_dispatch_kernel, n_tok=n_tok, n_tiles=n_tiles), grid_spec=grid_spec,
        out_shape=[jax.ShapeDtypeStruct((n_tiles * tm, half), jnp.uint32),
                   jax.ShapeDtypeStruct((d, f), BF16), jax.ShapeDtypeStruct((d, f), BF16),
                   jax.ShapeDtypeStruct((f, d), BF16)],
        compiler_params=_params(("arbitrary",)), name="moe_dispatch",
    )(pos, zero_fill, hp, sh_gate, sh_up, sh_down)


def _expert_kernel(te_ref, tv_ref, first_ref, slot_ref, next_ref, x_ref, wg_hbm, wu_hbm, wd_hbm, o_ref,
                   stage_g, stage_u, stage_d, wgb, wub, wdb, sems, *, layer):
    i = pl.program_id(0)

    def weight_copies(e, s):
        return (pltpu.make_async_copy(wg_hbm.at[layer, e], stage_g.at[s], sems.at[s]),
                pltpu.make_async_copy(wu_hbm.at[layer, e], stage_u.at[s], sems.at[s]),
                pltpu.make_async_copy(wd_hbm.at[layer, e], stage_d.at[s], sems.at[s]))

    @pl.when(i == 0)
    def _():
        for cp in weight_copies(te_ref[0], 0):
            cp.start()

    @pl.when(first_ref[i] == 1)
    def _():
        s = slot_ref[i]
        for cp in weight_copies(te_ref[i], s):
            cp.wait()

        @pl.when(next_ref[i] >= 0)
        def _():
            for cp in weight_copies(next_ref[i], 1 - s):
                cp.start()

        wgb[...] = stage_g[s].astype(BF16)
        wub[...] = stage_u[s].astype(BF16)
        wdb[...] = stage_d[s].astype(BF16)

    @pl.when(tv_ref[i] == 1)
    def _():
        lo, hi = _unpack_halves(x_ref[...])
        x = jnp.concatenate([lo.astype(BF16), hi.astype(BF16)], axis=1)
        hid = _silu(_dot(x, wgb[...])) * _dot(x, wub[...])
        o_ref[...] = _pack_halves(_dot(hid.astype(BF16), wdb[...]))

    @pl.when(tv_ref[i] == 0)
    def _():
        o_ref[...] = jnp.zeros_like(o_ref)


def routed_experts(xs, tile_e, tile_valid, w_gate, w_up, w_down, layer):
    tm = EXPERT_TILE
    n_tiles = tile_e.shape[0]
    _, n_exp, d, f = w_gate.shape
    prev_e = jnp.concatenate([jnp.full((1,), -1, jnp.int32), tile_e[:-1]])
    first = (tile_e != prev_e).astype(jnp.int32)
    slot = (jnp.cumsum(first) - 1) % 2
    tiles = jnp.arange(n_tiles, dtype=jnp.int32)
    run_start = jnp.where(first == 1, tiles, n_tiles)
    next_start = jnp.min(jnp.where(run_start[None, :] > tiles[:, None], run_start[None, :], n_tiles), axis=1)
    next_e = jnp.where(next_start < n_tiles, tile_e[jnp.minimum(next_start, n_tiles - 1)], -1)
    idx_map = lambda i, *_: (i, 0)
    anyspace = pl.BlockSpec(memory_space=pl.ANY)
    grid_spec = pltpu.PrefetchScalarGridSpec(
        num_scalar_prefetch=5, grid=(n_tiles,),
        in_specs=[pl.BlockSpec((tm, d // 2), idx_map), anyspace, anyspace, anyspace],
        out_specs=pl.BlockSpec((tm, d // 2), idx_map),
        scratch_shapes=[pltpu.VMEM((2, d, f), F32), pltpu.VMEM((2, d, f), F32), pltpu.VMEM((2, f, d), F32),
                        pltpu.VMEM((d, f), BF16), pltpu.VMEM((d, f), BF16), pltpu.VMEM((f, d), BF16),
                        pltpu.SemaphoreType.DMA((2,))])
    return pl.pallas_call(
        functools.partial(_expert_kernel, layer=layer), grid_spec=grid_spec,
        out_shape=jax.ShapeDtypeStruct((n_tiles * tm, d // 2), jnp.uint32),
        compiler_params=_params(("arbitrary",)), name="routed_experts",
    )(tile_e, tile_valid, first, slot.astype(jnp.int32), next_e.astype(jnp.int32), xs, w_gate, w_up, w_down)


def _combine_kernel(pos_ref, ys_hbm, sh_ref, w_ref, x_ref, mod_a_ref, wpost_ref, *refs, n_tok, has_prenorm,
                    gate_idx, shift_idx, scale_idx):
    if has_prenorm:
        mod_b_ref, wpre_ref, xo_ref, h_ref, buf_a, buf_b, sem_a, sem_b = refs
    else:
        xo_ref, buf_a, buf_b, sem_a, sem_b = refs
    i = pl.program_id(0)
    n = pl.num_programs(0)
    tc = buf_a.shape[1]
    half = sh_ref.shape[-1] // 2
    last_tile = 2 * n - 1

    def issue(tile, buf, sem):
        for t in range(tc):
            for k in range(MOE_TOPK):
                p = pos_ref[k * n_tok + tile * tc + t]
                pltpu.make_async_copy(ys_hbm.at[pl.ds(p, 1)], buf.at[k, pl.ds(t, 1)], sem).start()

    def wait(buf, sem):
        for k in range(MOE_TOPK):
            pltpu.make_async_copy(ys_hbm.at[pl.ds(0, tc)], buf.at[k], sem).wait()

    def reduce(buf, rows):
        sh = sh_ref[rows, :].astype(F32)
        w = w_ref[rows, :]
        lo_acc, hi_acc = sh[:, :half], sh[:, half:]
        for k in range(MOE_TOPK):
            lo, hi = _unpack_halves(buf[k])
            lo_acc = lo_acc + w[:, k:k + 1] * lo
            hi_acc = hi_acc + w[:, k:k + 1] * hi
        ffn = jnp.concatenate([lo_acc, hi_acc], axis=1)
        x = x_ref[rows, :] + mod_a_ref[gate_idx:gate_idx + 1, :] * (_rms(ffn) * wpost_ref[...])
        xo_ref[rows, :] = x
        if has_prenorm:
            h = (_rms(x) * wpre_ref[...]) * (1.0 + mod_b_ref[scale_idx:scale_idx + 1, :]) \
                + mod_b_ref[shift_idx:shift_idx + 1, :]
            h_ref[rows, :] = h.astype(BF16)

    @pl.when(i == 0)
    def _():
        issue(0, buf_a, sem_a)

    wait(buf_a, sem_a)
    issue(2 * i + 1, buf_b, sem_b)
    reduce(buf_a, slice(0, tc))
    wait(buf_b, sem_b)
    issue(jnp.minimum(2 * i + 2, last_tile), buf_a, sem_a)
    reduce(buf_b, slice(tc, 2 * tc))

    @pl.when(i == n - 1)
    def _():
        wait(buf_a, sem_a)


def combine(ys, pos, shared, wgt, x, mod_row_map, *, mod_a, w_post, gate_idx, mod_b=None, w_pre=None,
            shift_idx=0, scale_idx=0):
    n_tok, d = shared.shape
    tc = COMBINE_TILE
    step = 2 * tc
    has_prenorm = mod_b is not None
    buf = pltpu.VMEM((MOE_TOPK, tc, d // 2), jnp.uint32)
    row = lambda i, p: (i, 0)
    const = lambda i, p: (0, 0)
    mod_spec = pl.BlockSpec((None, N_ADA, d), lambda i, p: (mod_row_map(i * step), 0, 0))
    vec_spec = pl.BlockSpec((1, d), const)
    args = [pos, ys, shared, wgt, x, mod_a, w_post.reshape(1, d)]
    in_specs = [pl.BlockSpec(memory_space=pl.ANY), pl.BlockSpec((step, d), row),
                pl.BlockSpec((step, MOE_TOPK), row), pl.BlockSpec((step, d), row), mod_spec, vec_spec]
    out_shape = [jax.ShapeDtypeStruct((n_tok, d), F32)]
    out_specs = [pl.BlockSpec((step, d), row)]
    if has_prenorm:
        args += [mod_b, w_pre.reshape(1, d)]
        in_specs += [mod_spec, vec_spec]
        out_shape.append(jax.ShapeDtypeStruct((n_tok, d), BF16))
        out_specs.append(pl.BlockSpec((step, d), row))
    grid_spec = pltpu.PrefetchScalarGridSpec(
        num_scalar_prefetch=1, grid=(n_tok // step,), in_specs=in_specs, out_specs=out_specs,
        scratch_shapes=[buf, buf, pltpu.SemaphoreType.DMA(()), pltpu.SemaphoreType.DMA(())])
    kern = functools.partial(_combine_kernel, n_tok=n_tok, has_prenorm=has_prenorm, gate_idx=gate_idx,
                             shift_idx=shift_idx, scale_idx=scale_idx)
    return pl.pallas_call(
        kern, grid_spec=grid_spec, out_shape=out_shape,
        compiler_params=_params(("arbitrary",)), name="moe_combine",
    )(*args)


def moe_ffn(h, hp, route, w_gate, w_up, w_down, sh_gate, sh_up, sh_down, layer, x, mod_row_map, **epilogue):
    idx_t, wgt_t, rank_t, cnt = route
    pairs = idx_t.shape[0] * idx_t.shape[1]
    tile_e, tile_valid, zero_fill, base = _dispatch_plan(cnt[:, :, 0], pairs, EXPERT_TILE)
    pos = pair_positions(idx_t, rank_t, base).reshape(pairs)
    xs, sg, su, sd = dispatch(hp, pos, zero_fill, tile_e.shape[0], sh_gate, sh_up, sh_down, layer)
    ys = routed_experts(xs, tile_e, tile_valid, w_gate, w_up, w_down, layer)
    shared = shared_expert(h, sg, su, sd)
    return combine(ys, pos, shared, wgt_t.T, x, mod_row_map, **epilogue)


def kernel(x, c, ctx, c_ctx, ada_w, ada_b, norm_pre_mix, norm_post_mix, norm_pre_ffn, norm_post_ffn, ret_w_in, ret_w_out, ret_decay_fwd, ret_decay_bwd, diff_w_in, diff_w_out, diff_lam_q1, diff_lam_k1, diff_lam_q2, diff_lam_k2, diff_subln_w, moe_router_w, moe_router_b, moe_w_gate, moe_w_up, moe_w_down, moe_shared_gate, moe_shared_up, moe_shared_down):
    batch, t_len, d = x.shape
    c_len = ctx.shape[1]
    s_len = t_len + c_len
    depth = ada_w.shape[0]
    assert depth == 2 and batch + 1 <= MOD_ROWS
    assert t_len % ROW_TILE == 0 and c_len % ROW_TILE == 0 and t_len % GRID_W == 0
    ret_heads = ret_decay_fwd.shape[-1]
    diff_heads = d // (2 * DIFF_HEAD_DIM)
    lat_tiles, all_tiles = t_len // ROW_TILE, s_len // ROW_TILE

    cc = jnp.concatenate([c, c_ctx[None], jnp.zeros((MOD_ROWS - batch - 1, d), F32)], axis=0)
    mods = ada_modulation(cc, ada_w, ada_b)
    rope_ret = _rope_tables(t_len, d // ret_heads)
    rope_diff = _rope_tables(t_len, DIFF_HEAD_DIM)

    ident = lambda i: i
    uni_mod = lambda i: jnp.where(i % all_tiles < lat_tiles, i // all_tiles, batch)
    lat_mod = lambda i: i // lat_tiles
    uni_row_mod = lambda r: jnp.where(r % s_len < t_len, r // s_len, batch)
    lat_row_mod = lambda r: r // t_len
    lat_of_uni = lambda i: (i // lat_tiles) * all_tiles + i % lat_tiles

    xs, h = ingest(x, ctx, uni_mod, mods[0], norm_pre_mix[0], 0, 1)
    qkvg = matmul(h, ret_w_in, 0)
    lg_f = jax.nn.log_sigmoid(ret_decay_fwd[0].astype(F32))
    lg_b = jax.nn.log_sigmoid(ret_decay_bwd[0].astype(F32))
    r = retention(qkvg, lg_f, lg_b, *rope_ret, batch=batch, t_len=t_len, c_len=c_len, heads=ret_heads)
    y = matmul(r, ret_w_out, 0)
    xs, h, hp, *route = mixer_norm(
        xs, ident, batch * all_tiles, uni_mod, y, mods[0], norm_post_mix[0], norm_pre_ffn[0],
        moe_router_w[0].T, moe_router_b[0], gate_idx=2, shift_idx=3, scale_idx=4)
    xs, h = moe_ffn(h, hp, route, moe_w_gate, moe_w_up, moe_w_down,
                    moe_shared_gate, moe_shared_up, moe_shared_down, 0, xs, uni_row_mod,
                    mod_a=mods[0], w_post=norm_post_ffn[0], gate_idx=5,
                    mod_b=mods[1], w_pre=norm_pre_mix[1], shift_idx=0, scale_idx=1)

    qkv = matmul(h, diff_w_in, 0)
    lam_vecs = jnp.stack([diff_lam_q1[0], diff_lam_k1[0], diff_lam_q2[0], diff_lam_k2[0]]).astype(F32)
    lambda_init = 0.8 - 0.6 * math.exp(-0.3 * 1)
    a = diff_attention(qkv, lam_vecs, diff_subln_w[0], *rope_diff, batch=batch, t_len=t_len, c_len=c_len,
                       heads=diff_heads, lambda_init=lambda_init)
    y = matmul(a, diff_w_out, 0)
    xl, h, hp, *route = mixer_norm(
        xs, lat_of_uni, batch * lat_tiles, lat_mod, y, mods[1], norm_post_mix[1], norm_pre_ffn[1],
        moe_router_w[1].T, moe_router_b[1], gate_idx=2, shift_idx=3, scale_idx=4)
    (out,) = moe_ffn(h, hp, route, moe_w_gate, moe_w_up, moe_w_down,
                     moe_shared_gate, moe_shared_up, moe_shared_down, 1, xl, lat_row_mod,
                     mod_a=mods[1], w_post=norm_post_ffn[1], gate_idx=5)
    return out.reshape(batch, t_len, d)
```

```python
import functools
import math

import jax
import jax.numpy as jnp
from jax import lax
from jax.experimental import pallas as pl
from jax.experimental.pallas import tpu as pltpu

GRID_W = 64
N_ADA = 6
NORM_EPS = 1e-6
ROPE_BASE = 10000.0
RET_BLOCK = 256
DIFF_HEAD_DIM = 128
MOE_TOPK = 8
MOE_GROUPS = 8
MOE_TOPK_GROUPS = 4
ROUTED_SCALE = 2.5

LANES = 128
SUBLANES = 8
MOD_ROWS = SUBLANES
ROW_TILE = 256
EXPERT_TILE = 256
COMBINE_TILE = 64
DISPATCH_TILE = 256
ATTN_Q_TILES = 8
VMEM_LIMIT = 56 * 1024 * 1024
MATMUL_VMEM_BUDGET = 53 * 1024 * 1024
MATMUL_VMEM_LIMIT = 60 * 1024 * 1024

F32 = jnp.float32
BF16 = jnp.bfloat16


def _pick(dim, candidates):
    for c in candidates:
        if dim % c == 0:
            return c
    raise ValueError(f"no tile in {candidates} divides {dim}")


def _params(sem, vmem=VMEM_LIMIT):
    return pltpu.CompilerParams(dimension_semantics=sem, vmem_limit_bytes=vmem)


def _dot(a, b):
    return jnp.dot(a, b, preferred_element_type=F32)


def _dot_nt(a, b):
    return lax.dot_general(a, b, (((1,), (1,)), ((), ())), preferred_element_type=F32)


def _dot_tn(a, b):
    return lax.dot_general(a, b, (((0,), (0,)), ((), ())), preferred_element_type=F32)


def _silu(x):
    return x * jax.nn.sigmoid(x)


def _pack_halves(y):
    w = y.shape[-1] // 2
    lo = lax.bitcast_convert_type(y[:, :w].astype(BF16).astype(F32), jnp.uint32)
    hi = lax.bitcast_convert_type(y[:, w:].astype(BF16).astype(F32), jnp.uint32)
    return (hi & jnp.uint32(0xFFFF0000)) | (lo >> 16)


def _unpack_halves(p):
    lo = lax.bitcast_convert_type(p << 16, F32)
    hi = lax.bitcast_convert_type(p & jnp.uint32(0xFFFF0000), F32)
    return lo, hi


def _ada_kernel(c_ref, w_ref, b_ref, o_ref):
    a = _silu(c_ref[...]).astype(BF16)
    o_ref[...] = _dot(a, w_ref[...].astype(BF16)) + b_ref[...]


def ada_modulation(cc, ada_w, ada_b):
    depth, d, n = ada_w.shape
    tn = _pick(n, (512, 256, 128))
    out = pl.pallas_call(
        _ada_kernel,
        grid=(depth, n // tn),
        in_specs=[pl.BlockSpec((MOD_ROWS, d), lambda l, j: (0, 0)),
                  pl.BlockSpec((None, d, tn), lambda l, j: (l, 0, j)),
                  pl.BlockSpec((None, 1, tn), lambda l, j: (l, 0, j))],
        out_specs=pl.BlockSpec((None, MOD_ROWS, tn), lambda l, j: (l, 0, j)),
        out_shape=jax.ShapeDtypeStruct((depth, MOD_ROWS, n), F32),
        compiler_params=_params(("parallel", "parallel")),
        name="ada_modulation",
    )(cc, ada_w, ada_b.reshape(depth, 1, n))
    return out.reshape(depth, MOD_ROWS, N_ADA, d)


def _rms(x):
    return x * lax.rsqrt(jnp.mean(x * x, axis=-1, keepdims=True) + NORM_EPS)


def _route(h, rw_ref, rb_ref, idx_ref, wgt_ref, rank_ref, cnt_ref):
    n_exp = rw_ref.shape[0]
    tm = h.shape[0]
    per_group = n_exp // MOE_GROUPS
    w = rw_ref[...]
    w_hi = w.astype(BF16)
    w_lo = (w - w_hi.astype(F32)).astype(BF16)
    h_hi = h.astype(BF16)
    h_lo = (h - h_hi.astype(F32)).astype(BF16)
    logits = _dot_nt(w_hi, h_hi) + (_dot_nt(w_hi, h_lo) + _dot_nt(w_lo, h_hi))
    scores = jax.nn.sigmoid(logits)
    biased = scores + rb_ref[...]
    neg = jnp.float32(-jnp.inf)
    sub = lax.broadcasted_iota(jnp.int32, (per_group, tm), 0)
    giota = lax.broadcasted_iota(jnp.int32, (MOE_GROUPS, tm), 0)
    gs = jnp.zeros((MOE_GROUPS, tm), F32)
    for g in range(MOE_GROUPS):
        blk = biased[g * per_group:(g + 1) * per_group]
        m1 = jnp.max(blk, axis=0, keepdims=True)
        i1 = jnp.min(jnp.where(blk == m1, sub, per_group), axis=0, keepdims=True)
        m2 = jnp.max(jnp.where(sub == i1, neg, blk), axis=0, keepdims=True)
        gs = jnp.where(giota == g, m1 + m2, gs)
    rank = jnp.zeros((MOE_GROUPS, tm), jnp.int32)
    for j in range(MOE_GROUPS):
        gj = gs[j:j + 1]
        beats = (gj > gs) | ((gj == gs) & (giota > j))
        rank = rank + beats.astype(jnp.int32)
    keep = (rank < MOE_TOPK_GROUPS).astype(F32)
    keep_e = jnp.concatenate(
        [jnp.broadcast_to(keep[g:g + 1], (per_group, tm)) for g in range(MOE_GROUPS)], axis=0)
    masked = jnp.where(keep_e > 0.5, biased, neg)
    eiota = lax.broadcasted_iota(jnp.int32, (n_exp, tm), 0)
    sel_w, hits = [], []
    for k in range(MOE_TOPK):
        m = jnp.max(masked, axis=0, keepdims=True)
        idx = jnp.min(jnp.where(masked == m, eiota, n_exp), axis=0, keepdims=True)
        hit = eiota == idx
        hits.append(hit)
        sel_w.append(jnp.sum(jnp.where(hit, scores, 0.0), axis=0, keepdims=True))
        masked = jnp.where(hit, neg, masked)
        idx_ref[k:k + 1, :] = idx
    total = sel_w[0]
    for k in range(1, MOE_TOPK):
        total = total + sel_w[k]
    for k in range(MOE_TOPK):
        wgt_ref[k:k + 1, :] = sel_w[k] / total * ROUTED_SCALE
    chosen = jnp.zeros((n_exp, tm), F32)
    for k in range(MOE_TOPK):
        chosen = jnp.where(hits[k], 1.0, chosen)
    before = (lax.broadcasted_iota(jnp.int32, (tm, tm), 0)
              < lax.broadcasted_iota(jnp.int32, (tm, tm), 1)).astype(BF16)
    prefix = _dot(chosen.astype(BF16), before)
    for k in range(MOE_TOPK):
        rank_ref[k:k + 1, :] = jnp.sum(jnp.where(hits[k], prefix, 0.0), axis=0,
                                       keepdims=True).astype(jnp.int32)
    cnt_ref[...] = jnp.broadcast_to(jnp.sum(chosen, axis=1, keepdims=True),
                                    cnt_ref.shape).astype(jnp.int32)


def _mixer_norm_kernel(x_ref, y_ref, mod_ref, wpost_ref, wpre_ref, rw_ref, rb_ref,
                       xo_ref, h_ref, idx_ref, wgt_ref, rank_ref, cnt_ref, *, gate_idx, shift_idx, scale_idx):
    x = x_ref[...] + mod_ref[gate_idx:gate_idx + 1, :] * (_rms(y_ref[...].astype(F32)) * wpost_ref[...])
    xo_ref[...] = x
    h = (_rms(x) * wpre_ref[...]) * (1.0 + mod_ref[scale_idx:scale_idx + 1, :]) \
        + mod_ref[shift_idx:shift_idx + 1, :]
    h_ref[...] = h.astype(BF16)
    _route(h, rw_ref, rb_ref, idx_ref, wgt_ref, rank_ref, cnt_ref)


def _ingest_kernel(x_ref, c_ref, mod_ref, wpre_ref, xs_ref, h_ref, *, lat_tiles, all_tiles, shift_idx,
                   scale_idx):
    is_latent = pl.program_id(0) % all_tiles < lat_tiles

    def emit(x):
        xs_ref[...] = x
        h = (_rms(x) * wpre_ref[...]) * (1.0 + mod_ref[scale_idx:scale_idx + 1, :]) \
            + mod_ref[shift_idx:shift_idx + 1, :]
        h_ref[...] = h.astype(BF16)

    @pl.when(is_latent)
    def _():
        emit(x_ref[...])

    @pl.when(jnp.logical_not(is_latent))
    def _():
        emit(c_ref[...])


def ingest(x, ctx, mod_row_map, mod, w_pre, shift_idx, scale_idx):
    batch, t_len, d = x.shape
    c_len = ctx.shape[1]
    tm = ROW_TILE
    lat_tiles, ctx_tiles = t_len // tm, c_len // tm
    all_tiles = lat_tiles + ctx_tiles
    x_map = lambda i: ((i // all_tiles) * lat_tiles + jnp.minimum(i % all_tiles, lat_tiles - 1), 0)
    c_map = lambda i: ((i // all_tiles) * ctx_tiles + jnp.maximum(i % all_tiles - lat_tiles, 0), 0)
    row = lambda i: (i, 0)
    kern = functools.partial(_ingest_kernel, lat_tiles=lat_tiles, all_tiles=all_tiles, shift_idx=shift_idx,
                             scale_idx=scale_idx)
    n_rows = batch * all_tiles * tm
    return pl.pallas_call(
        kern, grid=(batch * all_tiles,),
        in_specs=[pl.BlockSpec((tm, d), x_map), pl.BlockSpec((tm, d), c_map),
                  pl.BlockSpec((None, N_ADA, d), lambda i: (mod_row_map(i), 0, 0)),
                  pl.BlockSpec((1, d), lambda i: (0, 0))],
        out_specs=[pl.BlockSpec((tm, d), row), pl.BlockSpec((tm, d), row)],
        out_shape=[jax.ShapeDtypeStruct((n_rows, d), F32), jax.ShapeDtypeStruct((n_rows, d), BF16)],
        compiler_params=_params(("arbitrary",)), name="ingest",
    )(x.reshape(batch * t_len, d), ctx.reshape(batch * c_len, d), mod, w_pre.reshape(1, d))


def mixer_norm(x, x_tile_map, n_out_tiles, mod_row_map, y, mod, w_post, w_pre, router_wt, router_b, *,
               gate_idx, shift_idx, scale_idx):
    d = x.shape[-1]
    tm = ROW_TILE
    n_exp = router_wt.shape[0]
    n_rows = n_out_tiles * tm
    row = lambda i: (i, 0)
    const = lambda i: (0, 0)
    per_token = pl.BlockSpec((MOE_TOPK, tm), lambda i: (0, i))
    kern = functools.partial(_mixer_norm_kernel, gate_idx=gate_idx, shift_idx=shift_idx, scale_idx=scale_idx)
    return pl.pallas_call(
        kern, grid=(n_out_tiles,),
        in_specs=[pl.BlockSpec((tm, d), lambda i: (x_tile_map(i), 0)), pl.BlockSpec((tm, d), row),
                  pl.BlockSpec((None, N_ADA, d), lambda i: (mod_row_map(i), 0, 0)),
                  pl.BlockSpec((1, d), const), pl.BlockSpec((1, d), const),
                  pl.BlockSpec((n_exp, d), const), pl.BlockSpec((n_exp, 1), const)],
        out_specs=[pl.BlockSpec((tm, d), row), pl.BlockSpec((tm, d), row),
                   per_token, per_token, per_token,
                   pl.BlockSpec((None, n_exp, LANES), lambda i: (i, 0, 0))],
        out_shape=[jax.ShapeDtypeStruct((n_rows, d), F32), jax.ShapeDtypeStruct((n_rows, d), BF16),
                   jax.ShapeDtypeStruct((MOE_TOPK, n_rows), jnp.int32),
                   jax.ShapeDtypeStruct((MOE_TOPK, n_rows), F32),
                   jax.ShapeDtypeStruct((MOE_TOPK, n_rows), jnp.int32),
                   jax.ShapeDtypeStruct((n_out_tiles, n_exp, LANES), jnp.int32)],
        compiler_params=_params(("parallel",)), name="mixer_norm",
    )(x, y, mod, w_post.reshape(1, d), w_pre.reshape(1, d), router_wt, router_b.reshape(n_exp, 1))


def _mm_kernel(a_ref, w_ref, o_ref):
    o_ref[...] = _dot(a_ref[...], w_ref[...].astype(BF16)).astype(o_ref.dtype)


def _mm_acc_kernel(a_ref, w_ref, o_ref, acc_ref):
    k = pl.program_id(2)

    @pl.when(k == 0)
    def _():
        acc_ref[...] = jnp.zeros_like(acc_ref)

    acc_ref[...] += _dot(a_ref[...], w_ref[...].astype(BF16))

    @pl.when(k == pl.num_programs(2) - 1)
    def _():
        o_ref[...] = acc_ref[...].astype(o_ref.dtype)


def matmul(a, w, layer, out_dtype=BF16):
    m, k = a.shape
    n = w.shape[2]
    tm = _pick(m, (1024, 768, 512, 256))

    def vmem_bytes(tn, tk):
        return (2 * tm * tk * 2 + 2 * tk * tn * 4 + tk * tn * 2 + 2 * tm * tn * 2
                + (tm * tn * 4 if tk < k else 0))

    tn, tk = next(((tn, tk) for tk in (k, k // 2, k // 4) for tn in (512, 256)
                   if n % tn == 0 and tk % LANES == 0 and vmem_bytes(tn, tk) <= MATMUL_VMEM_BUDGET),
                  (_pick(n, (256, 128)), _pick(k, (1024, 512, 256, 128))))
    if tk == k:
        return pl.pallas_call(
            _mm_kernel, grid=(m // tm, n // tn),
            in_specs=[pl.BlockSpec((tm, k), lambda i, j: (i, 0)),
                      pl.BlockSpec((None, k, tn), lambda i, j: (layer, 0, j))],
            out_specs=pl.BlockSpec((tm, tn), lambda i, j: (i, j)),
            out_shape=jax.ShapeDtypeStruct((m, n), out_dtype),
            compiler_params=_params(("parallel", "parallel"), MATMUL_VMEM_LIMIT), name="matmul",
        )(a, w)
    return pl.pallas_call(
        _mm_acc_kernel, grid=(m // tm, n // tn, k // tk),
        in_specs=[pl.BlockSpec((tm, tk), lambda i, j, l: (i, l)),
                  pl.BlockSpec((None, tk, tn), lambda i, j, l: (layer, l, j))],
        out_specs=pl.BlockSpec((tm, tn), lambda i, j, l: (i, j)),
        out_shape=jax.ShapeDtypeStruct((m, n), out_dtype),
        scratch_shapes=[pltpu.VMEM((tm, tn), F32)],
        compiler_params=_params(("parallel", "parallel", "arbitrary")), name="matmul_acc",
    )(a, w)


def _rope_tables(t_len, head_dim):
    rows = t_len // GRID_W
    n_freq = head_dim // 4
    row, col = jnp.meshgrid(jnp.arange(rows, dtype=F32), jnp.arange(GRID_W, dtype=F32), indexing="ij")
    inv_freq = ROPE_BASE ** (-jnp.arange(n_freq, dtype=F32) / n_freq)
    ang_r = row.reshape(-1, 1) * inv_freq
    ang_c = col.reshape(-1, 1) * inv_freq
    cr, sr, cc, sc = jnp.cos(ang_r), jnp.sin(ang_r), jnp.cos(ang_c), jnp.sin(ang_c)
    return (jnp.concatenate([cr, cr, cc, cc], axis=-1), jnp.concatenate([-sr, sr, -sc, sc], axis=-1))


def _swap_quarters(x, quarter):
    lane = lax.broadcasted_iota(jnp.int32, x.shape, 1)
    first = (lane % (2 * quarter)) < quarter
    return jnp.where(first, pltpu.roll(x, LANES - quarter, axis=1), pltpu.roll(x, quarter, axis=1))


def _retention_kernel(lgf_ref, lgb_ref, q_ref, k_ref, v_ref, g_ref, cos_ref, sin_ref, o_ref,
                      qr, kr, oacc_f, oacc_b, state_f, state_b, *, t_len, c_len):
    head = pl.program_id(1)
    L = RET_BLOCK
    dk = q_ref.shape[-1]
    k_scale = dk ** -0.5

    def rope(x, rows):
        sw = jnp.concatenate([pltpu.roll(x[:, :LANES], LANES // 2, axis=1),
                              pltpu.roll(x[:, LANES:], LANES // 2, axis=1)], axis=1)
        return x * cos_ref[rows, :] + sw * sin_ref[rows, :]

    def prepare(row0):
        rows = slice(row0, row0 + L)
        if row0 < t_len:
            qr[rows, :] = rope(q_ref[rows, :].astype(F32), rows).astype(BF16)
            kr[rows, :] = (rope(k_ref[rows, :].astype(F32), rows) * k_scale).astype(BF16)
        else:
            qr[rows, :] = q_ref[rows, :]
            kr[rows, :] = (k_ref[rows, :].astype(F32) * k_scale).astype(BF16)

    ii = lax.broadcasted_iota(jnp.int32, (L, L), 0)
    jj = lax.broadcasted_iota(jnp.int32, (L, L), 1)
    rel = (ii - jj).astype(F32)
    idx = lax.broadcasted_iota(jnp.int32, (L, 1), 0).astype(F32)

    lg_f, lg_b = lgf_ref[head], lgb_ref[head]
    fwd = (jnp.where(rel >= 0, jnp.exp(lg_f * jnp.maximum(rel, 0.0)), 0.0),
           jnp.exp(lg_f * (idx + 1.0)),
           jnp.exp(lg_f * (L - 1.0 - idx)),
           jnp.exp(lg_f * L), state_f, oacc_f)
    bwd = (jnp.where(rel <= 0, jnp.exp(lg_b * jnp.maximum(-rel, 0.0)), 0.0),
           jnp.exp(lg_b * (L - idx)),
           jnp.exp(lg_b * idx),
           jnp.exp(lg_b * L), state_b, oacc_b)

    def chunk(row0, direction):
        dmat, q_decay, k_decay, chunk_decay, state, oacc = direction
        rows = slice(row0, row0 + L)
        qb, kb, vb = qr[rows, :], kr[rows, :], v_ref[rows, :]
        scores = _dot_nt(qb, kb) * dmat
        inner = _dot(scores.astype(BF16), vb)
        st = state[...]
        cross = _dot(qb, st.astype(BF16)) * q_decay
        oacc[rows, :] = inner + cross
        kd = (kb.astype(F32) * k_decay).astype(BF16)
        state[...] = st * chunk_decay + _dot_tn(kd, vb)

    def finish(row0):
        rows = slice(row0, row0 + L)
        o = _rms(oacc_f[rows, :] + oacc_b[rows, :])
        o_ref[rows, :] = (_silu(g_ref[rows, :].astype(F32)) * o).astype(BF16)

    def scan(r0, n_chunks):
        ready, done_f, done_b = set(), set(), set()
        for ci in range(n_chunks):
            cf, cb = ci, n_chunks - 1 - ci
            for c in (cf, cb):
                if c not in ready:
                    prepare(r0 + c * L)
                    ready.add(c)
            chunk(r0 + cf * L, fwd)
            chunk(r0 + cb * L, bwd)
            done_f.add(cf)
            done_b.add(cb)
            for c in sorted({cf, cb}):
                if c in done_f and c in done_b:
                    finish(r0 + c * L)

    state_f[...] = jnp.zeros_like(state_f)
    state_b[...] = jnp.zeros_like(state_b)
    scan(t_len, c_len // L)
    scan(0, t_len // L)


def retention(qkvg, lg_f, lg_b, cos, sin, *, batch, t_len, c_len, heads):
    s_len = t_len + c_len
    dk = cos.shape[-1]
    dv = 2 * dk
    assert dk == 2 * LANES
    kern = functools.partial(_retention_kernel, t_len=t_len, c_len=c_len)
    grid_spec = pltpu.PrefetchScalarGridSpec(
        num_scalar_prefetch=2, grid=(batch, heads),
        in_specs=[pl.BlockSpec((s_len, dk), lambda b, h, *_: (b, h)),
                  pl.BlockSpec((s_len, dk), lambda b, h, *_: (b, heads + h)),
                  pl.BlockSpec((s_len, dv), lambda b, h, *_: (b, heads + h)),
                  pl.BlockSpec((s_len, dv), lambda b, h, *_: (b, 2 * heads + h)),
                  pl.BlockSpec((t_len, dk), lambda b, h, *_: (0, 0)),
                  pl.BlockSpec((t_len, dk), lambda b, h, *_: (0, 0))],
        out_specs=pl.BlockSpec((s_len, dv), lambda b, h, *_: (b, h)),
        scratch_shapes=[pltpu.VMEM((s_len, dk), BF16), pltpu.VMEM((s_len, dk), BF16),
                        pltpu.VMEM((s_len, dv), F32), pltpu.VMEM((s_len, dv), F32),
                        pltpu.VMEM((dk, dv), F32), pltpu.VMEM((dk, dv), F32)])
    return pl.pallas_call(
        kern, grid_spec=grid_spec,
        out_shape=jax.ShapeDtypeStruct((batch * s_len, heads * dv), BF16),
        compiler_params=_params(("parallel", "parallel")), name="retention",
    )(lg_f, lg_b, qkvg, qkvg, qkvg, qkvg, cos, sin)


def _diff_attn_kernel(lam_ref, *refs, n_q, t_len, c_len, lambda_init):
    q_refs = refs[:n_q]
    k_ref, v_ref, cosq_ref, sinq_ref, cosk_ref, sink_ref, subln_ref, o_ref, kr = refs[n_q:]
    hd = DIFF_HEAD_DIM
    quarter = hd // 4
    rt = ROW_TILE

    def rope(x, cos, sin):
        return x * cos + _swap_quarters(x, quarter) * sin

    @pl.when(pl.program_id(2) == 0)
    def _():
        def body(i, _):
            rows = pl.ds(pl.multiple_of(i * rt, rt), rt)
            for c in range(2):
                cols = slice(c * hd, (c + 1) * hd)
                kr[rows, cols] = rope(k_ref[rows, cols].astype(F32), cosk_ref[rows, :],
                                      sink_ref[rows, :]).astype(BF16)
            return 0
        lax.fori_loop(0, t_len // rt, body, 0)
        ctx_rows = pl.ds(t_len, c_len)
        kr[ctx_rows, :] = k_ref[ctx_rows, :]

    lam_v = lam_ref[...]
    lam = (jnp.exp(jnp.sum(lam_v[0:1] * lam_v[1:2], axis=-1, keepdims=True))
           - jnp.exp(jnp.sum(lam_v[2:3] * lam_v[3:4], axis=-1, keepdims=True)) + lambda_init)
    q_scale = (hd ** -0.5) * math.log2(math.e)
    tq = q_refs[0].shape[0]
    for part, q_ref in enumerate(q_refs):
        rows = slice(part * tq, (part + 1) * tq)
        outs = []
        for c in range(2):
            cols = slice(c * hd, (c + 1) * hd)
            qc = (rope(q_ref[:, cols].astype(F32), cosq_ref[rows, :], sinq_ref[rows, :])
                  * q_scale).astype(BF16)
            s = _dot_nt(qc, kr[:, cols])
            e = jnp.exp2(s - jnp.max(s, axis=-1, keepdims=True))
            denom = jnp.sum(e, axis=-1, keepdims=True)
            outs.append(_dot(e.astype(BF16), v_ref[...]) / denom)
        o = outs[0] - lam * outs[1]
        o_ref[rows, :] = ((_rms(o) * subln_ref[...]) * (1.0 - lambda_init)).astype(BF16)


def diff_attention(qkv, lam_vecs, subln_w, cos, sin, *, batch, t_len, c_len, heads, lambda_init):
    s_len = t_len + c_len
    hd = DIFF_HEAD_DIM
    tq = ROW_TILE
    n_q = _pick(t_len // tq, (ATTN_Q_TILES, 2, 1))
    steps, s_tiles = t_len // (n_q * tq), s_len // tq
    kern = functools.partial(_diff_attn_kernel, n_q=n_q, t_len=t_len, c_len=c_len, lambda_init=lambda_init)
    q_specs = [pl.BlockSpec((tq, 2 * hd), lambda b, h, i, j=j: (b * s_tiles + n_q * i + j, h))
               for j in range(n_q)]
    return pl.pallas_call(
        kern, grid=(batch, heads, steps),
        in_specs=[pl.BlockSpec((4, hd), lambda b, h, i: (0, 0)), *q_specs,
                  pl.BlockSpec((s_len, 2 * hd), lambda b, h, i: (b, heads + h)),
                  pl.BlockSpec((s_len, 2 * hd), lambda b, h, i: (b, 2 * heads + h)),
                  pl.BlockSpec((n_q * tq, hd), lambda b, h, i: (i, 0)),
                  pl.BlockSpec((n_q * tq, hd), lambda b, h, i: (i, 0)),
                  pl.BlockSpec((t_len, hd), lambda b, h, i: (0, 0)),
                  pl.BlockSpec((t_len, hd), lambda b, h, i: (0, 0)),
                  pl.BlockSpec((1, 2 * hd), lambda b, h, i: (0, 0))],
        out_specs=pl.BlockSpec((n_q * tq, 2 * hd), lambda b, h, i: (b * steps + i, h)),
        out_shape=jax.ShapeDtypeStruct((batch * t_len, heads * 2 * hd), BF16),
        scratch_shapes=[pltpu.VMEM((s_len, 2 * hd), BF16)],
        compiler_params=_params(("parallel", "parallel", "arbitrary")), name="diff_attention",
    )(lam_vecs, *([qkv] * (n_q + 2)), cos, sin, cos, sin, subln_w.reshape(1, 2 * hd))


def _shared_kernel(x_ref, wg_ref, wu_ref, wd_ref, o_ref, acc_ref):
    f = pl.program_id(1)

    @pl.when(f == 0)
    def _():
        acc_ref[...] = jnp.zeros_like(acc_ref)

    x = x_ref[...]
    hid = _silu(_dot(x, wg_ref[...])) * _dot(x, wu_ref[...])
    acc_ref[...] += _dot(hid.astype(BF16), wd_ref[...])

    @pl.when(f == pl.num_programs(1) - 1)
    def _():
        o_ref[...] = acc_ref[...].astype(o_ref.dtype)


def shared_expert(h, w_gate, w_up, w_down):
    m, d = h.shape
    f_dim = w_gate.shape[1]
    tm = _pick(m, (512, 256))
    tf = _pick(f_dim, (512, 256, 128))
    return pl.pallas_call(
        _shared_kernel, grid=(m // tm, f_dim // tf),
        in_specs=[pl.BlockSpec((tm, d), lambda i, f: (i, 0)),
                  pl.BlockSpec((d, tf), lambda i, f: (0, f)),
                  pl.BlockSpec((d, tf), lambda i, f: (0, f)),
                  pl.BlockSpec((tf, d), lambda i, f: (f, 0))],
        out_specs=pl.BlockSpec((tm, d), lambda i, f: (i, 0)),
        out_shape=jax.ShapeDtypeStruct((m, d), BF16),
        scratch_shapes=[pltpu.VMEM((tm, d), F32)],
        compiler_params=_params(("parallel", "arbitrary")), name="shared_expert",
    )(h, w_gate, w_up, w_down)


def _dispatch_plan(cnt, pairs, tm):
    n_exp = cnt.shape[1]
    total = jnp.sum(cnt, axis=0)
    padded = (total + tm - 1) // tm * tm
    pend = jnp.cumsum(padded)
    base = (pend - padded)[None, :] + jnp.cumsum(cnt, axis=0) - cnt
    n_tiles = pairs // tm + n_exp
    tile_start = jnp.arange(n_tiles, dtype=jnp.int32) * tm
    valid = tile_start < pend[-1]
    tile_e = jnp.minimum(jnp.sum((tile_start[:, None] >= pend[None, :]).astype(jnp.int32), axis=1), n_exp - 1)
    tile_e = jnp.where(valid, tile_e, jnp.max(jnp.where(valid, tile_e, 0)))
    next_e = jnp.concatenate([tile_e[1:], jnp.full((1,), -1, jnp.int32)])
    next_valid = jnp.concatenate([valid[1:], jnp.zeros((1,), bool)])
    zero_fill = (~valid) | (tile_e != next_e) | (~next_valid)
    return (tile_e.astype(jnp.int32), valid.astype(jnp.int32), zero_fill.astype(jnp.int32),
            base.astype(jnp.int32))


def _positions_kernel(idx_ref, rank_ref, base_ref, pos_ref):
    n_exp = base_ref.shape[0]
    tm = idx_ref.shape[1]
    eiota = lax.broadcasted_iota(jnp.int32, (n_exp, tm), 0)
    base = jnp.broadcast_to(base_ref[...].astype(F32), (n_exp, tm))
    for k in range(MOE_TOPK):
        hit = eiota == idx_ref[k:k + 1, :]
        first = jnp.sum(jnp.where(hit, base, 0.0), axis=0, keepdims=True)
        pos_ref[k:k + 1, :] = first.astype(jnp.int32) + rank_ref[k:k + 1, :]


def pair_positions(idx_t, rank_t, base):
    k, n = idx_t.shape
    tiles, n_exp = base.shape
    tm = n // tiles
    spec = pl.BlockSpec((k, tm), lambda i: (0, i))
    return pl.pallas_call(
        _positions_kernel, grid=(tiles,),
        in_specs=[spec, spec, pl.BlockSpec((None, n_exp, 1), lambda i: (i, 0, 0))],
        out_specs=spec, out_shape=jax.ShapeDtypeStruct((k, n), jnp.int32),
        compiler_params=_params(("parallel",)), name="pair_positions",
    )(idx_t, rank_t, base.reshape(tiles, n_exp, 1))


def _dispatch_kernel(pos_ref, zf_ref, h_ref, wg_ref, wu_ref, wd_ref, xs_hbm, wgo_ref, wuo_ref, wdo_ref,
                     hp_ref, zeros, sem_z, sem_s, *, n_tok, n_tiles):
    i = pl.program_id(0)
    td = hp_ref.shape[0]
    tm = zeros.shape[0]
    hp_ref[...] = _pack_halves(h_ref[...].astype(F32))
    wgo_ref[...] = wg_ref[...].astype(BF16)
    wuo_ref[...] = wu_ref[...].astype(BF16)
    wdo_ref[...] = wd_ref[...].astype(BF16)

    def zero_copy(j):
        return pltpu.make_async_copy(zeros, xs_hbm.at[pl.ds(pl.multiple_of(j * tm, tm), tm)], sem_z)

    @pl.when(i == 0)
    def _():
        zeros[...] = jnp.zeros_like(zeros)

        def start(j, _):
            @pl.when(zf_ref[j] == 1)
            def _():
                zero_copy(j).start()
            return 0

        def wait(j, _):
            @pl.when(zf_ref[j] == 1)
            def _():
                zero_copy(j).wait()
            return 0

        lax.fori_loop(0, n_tiles, start, 0)
        lax.fori_loop(0, n_tiles, wait, 0)

    def body(t, _):
        for k in range(MOE_TOPK):
            row = pos_ref[k * n_tok + i * td + t]
            pltpu.make_async_copy(hp_ref.at[pl.ds(t, 1)], xs_hbm.at[pl.ds(row, 1)], sem_s).start()
        return 0

    lax.fori_loop(0, td, body, 0, unroll=2)
    for k in range(MOE_TOPK):
        pltpu.make_async_copy(hp_ref, xs_hbm.at[pl.ds(0, td)], sem_s).wait()


def dispatch(h, pos, zero_fill, n_tiles, sh_gate, sh_up, sh_down, layer):
    n_tok, half = h.shape[0], h.shape[1] // 2
    tm = EXPERT_TILE
    steps = n_tok // DISPATCH_TILE
    _, d, f = sh_gate.shape
    bf16_rows = 2 * SUBLANES
    nblk = max(n for n in range(1, steps + 1)
               if d % n == 0 and f % n == 0 and (d // n) % bf16_rows == 0 and (f // n) % bf16_rows == 0)
    wblk = lambda i, p, z: (layer, jnp.minimum(i, nblk - 1), 0)
    oblk = lambda i, p, z: (jnp.minimum(i, nblk - 1), 0)
    grid_spec = pltpu.PrefetchScalarGridSpec(
        num_scalar_prefetch=2, grid=(steps,),
        in_specs=[pl.BlockSpec((DISPATCH_TILE, 2 * half), lambda i, p, z: (i, 0)),
                  pl.BlockSpec((None, d // nblk, f), wblk),
                  pl.BlockSpec((None, d // nblk, f), wblk),
                  pl.BlockSpec((None, f // nblk, d), wblk)],
        out_specs=[pl.BlockSpec(memory_space=pl.ANY),
                   pl.BlockSpec((d // nblk, f), oblk),
                   pl.BlockSpec((d // nblk, f), oblk),
                   pl.BlockSpec((f // nblk, d), oblk)],
        scratch_shapes=[pltpu.VMEM((DISPATCH_TILE, half), jnp.uint32), pltpu.VMEM((tm, half), jnp.uint32),
                        pltpu.SemaphoreType.DMA(()), pltpu.SemaphoreType.DMA(())])
    return pl.pallas_call(
        functools.partial(_dispatch_kernel, n_tok=n_tok, n_tiles=n_tiles), grid_spec=grid_spec,
        out_shape=[jax.ShapeDtypeStruct((n_tiles * tm, half), jnp.uint32),
                   jax.ShapeDtypeStruct((d, f), BF16), jax.ShapeDtypeStruct((d, f), BF16),
                   jax.ShapeDtypeStruct((f, d), BF16)],
        compiler_params=_params(("arbitrary",)), name="moe_dispatch",
    )(pos, zero_fill, h, sh_gate, sh_up, sh_down)


def _expert_kernel(te_ref, tv_ref, first_ref, slot_ref, next_ref, src_ref, x_ref, wg_hbm, wu_hbm, wd_hbm, o_ref,
                   stage_g, stage_u, stage_d, wgb, wub, wdb, sems, *, layer):
    i = pl.program_id(0)

    def weight_copies(e, s):
        return (pltpu.make_async_copy(wg_hbm.at[layer, e], stage_g.at[s], sems.at[s]),
                pltpu.make_async_copy(wu_hbm.at[layer, e], stage_u.at[s], sems.at[s]),
                pltpu.make_async_copy(wd_hbm.at[layer, e], stage_d.at[s], sems.at[s]))

    @pl.when(i == 0)
    def _():
        for cp in weight_copies(te_ref[0], 0):
            cp.start()

    @pl.when(first_ref[i] == 1)
    def _():
        s = slot_ref[i]
        for cp in weight_copies(te_ref[i], s):
            cp.wait()

        @pl.when(next_ref[i] >= 0)
        def _():
            for cp in weight_copies(next_ref[i], 1 - s):
                cp.start()

        wgb[...] = stage_g[s].astype(BF16)
        wub[...] = stage_u[s].astype(BF16)
        wdb[...] = stage_d[s].astype(BF16)

    @pl.when(tv_ref[i] == 1)
    def _():
        lo, hi = _unpack_halves(x_ref[...])
        x = jnp.concatenate([lo.astype(BF16), hi.astype(BF16)], axis=1)
        hid = _silu(_dot(x, wgb[...])) * _dot(x, wub[...])
        o_ref[...] = _pack_halves(_dot(hid.astype(BF16), wdb[...]))

    @pl.when(tv_ref[i] == 0)
    def _():
        o_ref[...] = jnp.zeros_like(o_ref)


def routed_experts(xs, tile_e, tile_valid, w_gate, w_up, w_down, layer):
    tm = EXPERT_TILE
    n_tiles = tile_e.shape[0]
    _, n_exp, d, f = w_gate.shape
    prev_e = jnp.concatenate([jnp.full((1,), -1, jnp.int32), tile_e[:-1]])
    first = (tile_e != prev_e).astype(jnp.int32)
    slot = (jnp.cumsum(first) - 1) % 2
    tiles = jnp.arange(n_tiles, dtype=jnp.int32)
    run_start = jnp.where(first == 1, tiles, n_tiles)
    next_start = jnp.min(jnp.where(run_start[None, :] > tiles[:, None], run_start[None, :], n_tiles), axis=1)
    next_e = jnp.where(next_start < n_tiles, tile_e[jnp.minimum(next_start, n_tiles - 1)], -1)
    idx_map = lambda i, *_: (i, 0)
    src_tile = jnp.where(tile_valid == 1, tiles, jnp.sum(tile_valid) - 1).astype(jnp.int32)
    src_map = lambda i, te, tv, fi, sl, nx, st: (st[i], 0)
    anyspace = pl.BlockSpec(memory_space=pl.ANY)
    grid_spec = pltpu.PrefetchScalarGridSpec(
        num_scalar_prefetch=6, grid=(n_tiles,),
        in_specs=[pl.BlockSpec((tm, d // 2), src_map), anyspace, anyspace, anyspace],
        out_specs=pl.BlockSpec((tm, d // 2), idx_map),
        scratch_shapes=[pltpu.VMEM((2, d, f), F32), pltpu.VMEM((2, d, f), F32), pltpu.VMEM((2, f, d), F32),
                        pltpu.VMEM((d, f), BF16), pltpu.VMEM((d, f), BF16), pltpu.VMEM((f, d), BF16),
                        pltpu.SemaphoreType.DMA((2,))])
    return pl.pallas_call(
        functools.partial(_expert_kernel, layer=layer), grid_spec=grid_spec,
        out_shape=jax.ShapeDtypeStruct((n_tiles * tm, d // 2), jnp.uint32),
        compiler_params=_params(("arbitrary",)), name="routed_experts",
    )(tile_e, tile_valid, first, slot.astype(jnp.int32), next_e.astype(jnp.int32), src_tile,
      xs, w_gate, w_up, w_down)


def _combine_kernel(pos_ref, ys_hbm, sh_ref, w_ref, x_ref, mod_a_ref, wpost_ref, *refs, n_tok, has_prenorm,
                    gate_idx, shift_idx, scale_idx):
    if has_prenorm:
        mod_b_ref, wpre_ref, xo_ref, h_ref, buf_a, buf_b, sem_a, sem_b = refs
    else:
        xo_ref, buf_a, buf_b, sem_a, sem_b = refs
    i = pl.program_id(0)
    n = pl.num_programs(0)
    tc = buf_a.shape[1]
    half = sh_ref.shape[-1] // 2
    last_tile = 2 * n - 1

    def issue(tile, buf, sem):
        for t in range(tc):
            for k in range(MOE_TOPK):
                p = pos_ref[k * n_tok + tile * tc + t]
                pltpu.make_async_copy(ys_hbm.at[pl.ds(p, 1)], buf.at[k, pl.ds(t, 1)], sem).start()

    def wait(buf, sem):
        for k in range(MOE_TOPK):
            pltpu.make_async_copy(ys_hbm.at[pl.ds(0, tc)], buf.at[k], sem).wait()

    def reduce(buf, rows):
        sh = sh_ref[rows, :].astype(F32)
        w = w_ref[rows, :]
        lo_acc, hi_acc = sh[:, :half], sh[:, half:]
        for k in range(MOE_TOPK):
            lo, hi = _unpack_halves(buf[k])
            lo_acc = lo_acc + w[:, k:k + 1] * lo
            hi_acc = hi_acc + w[:, k:k + 1] * hi
        ffn = jnp.concatenate([lo_acc, hi_acc], axis=1)
        x = x_ref[rows, :] + mod_a_ref[gate_idx:gate_idx + 1, :] * (_rms(ffn) * wpost_ref[...])
        xo_ref[rows, :] = x
        if has_prenorm:
            h = (_rms(x) * wpre_ref[...]) * (1.0 + mod_b_ref[scale_idx:scale_idx + 1, :]) \
                + mod_b_ref[shift_idx:shift_idx + 1, :]
            h_ref[rows, :] = h.astype(BF16)

    @pl.when(i == 0)
    def _():
        issue(0, buf_a, sem_a)

    wait(buf_a, sem_a)
    issue(2 * i + 1, buf_b, sem_b)
    reduce(buf_a, slice(0, tc))
    wait(buf_b, sem_b)
    issue(jnp.minimum(2 * i + 2, last_tile), buf_a, sem_a)
    reduce(buf_b, slice(tc, 2 * tc))

    @pl.when(i == n - 1)
    def _():
        wait(buf_a, sem_a)


def combine(ys, pos, shared, wgt, x, mod_row_map, *, mod_a, w_post, gate_idx, mod_b=None, w_pre=None,
            shift_idx=0, scale_idx=0):
    n_tok, d = shared.shape
    tc = COMBINE_TILE
    step = 2 * tc
    has_prenorm = mod_b is not None
    buf = pltpu.VMEM((MOE_TOPK, tc, d // 2), jnp.uint32)
    row = lambda i, p: (i, 0)
    const = lambda i, p: (0, 0)
    mod_spec = pl.BlockSpec((None, N_ADA, d), lambda i, p: (mod_row_map(i * step), 0, 0))
    vec_spec = pl.BlockSpec((1, d), const)
    args = [pos, ys, shared, wgt, x, mod_a, w_post.reshape(1, d)]
    in_specs = [pl.BlockSpec(memory_space=pl.ANY), pl.BlockSpec((step, d), row),
                pl.BlockSpec((step, MOE_TOPK), row), pl.BlockSpec((step, d), row), mod_spec, vec_spec]
    out_shape = [jax.ShapeDtypeStruct((n_tok, d), F32)]
    out_specs = [pl.BlockSpec((step, d), row)]
    if has_prenorm:
        args += [mod_b, w_pre.reshape(1, d)]
        in_specs += [mod_spec, vec_spec]
        out_shape.append(jax.ShapeDtypeStruct((n_tok, d), BF16))
        out_specs.append(pl.BlockSpec((step, d), row))
    grid_spec = pltpu.PrefetchScalarGridSpec(
        num_scalar_prefetch=1, grid=(n_tok // step,), in_specs=in_specs, out_specs=out_specs,
        scratch_shapes=[buf, buf, pltpu.SemaphoreType.DMA(()), pltpu.SemaphoreType.DMA(())])
    kern = functools.partial(_combine_kernel, n_tok=n_tok, has_prenorm=has_prenorm, gate_idx=gate_idx,
                             shift_idx=shift_idx, scale_idx=scale_idx)
    return pl.pallas_call(
        kern, grid_spec=grid_spec, out_shape=out_shape,
        compiler_params=_params(("arbitrary",)), name="moe_combine",
    )(*args)


def moe_ffn(h, route, w_gate, w_up, w_down, sh_gate, sh_up, sh_down, layer, x, mod_row_map, **epilogue):
    idx_t, wgt_t, rank_t, cnt = route
    pairs = idx_t.shape[0] * idx_t.shape[1]
    tile_e, tile_valid, zero_fill, base = _dispatch_plan(cnt[:, :, 0], pairs, EXPERT_TILE)
    pos = pair_positions(idx_t, rank_t, base).reshape(pairs)
    xs, sg, su, sd = dispatch(h, pos, zero_fill, tile_e.shape[0], sh_gate, sh_up, sh_down, layer)
    ys = routed_experts(xs, tile_e, tile_valid, w_gate, w_up, w_down, layer)
    shared = shared_expert(h, sg, su, sd)
    return combine(ys, pos, shared, wgt_t.T, x, mod_row_map, **epilogue)


def kernel(x, c, ctx, c_ctx, ada_w, ada_b, norm_pre_mix, norm_post_mix, norm_pre_ffn, norm_post_ffn, ret_w_in, ret_w_out, ret_decay_fwd, ret_decay_bwd, diff_w_in, diff_w_out, diff_lam_q1, diff_lam_k1, diff_lam_q2, diff_lam_k2, diff_subln_w, moe_router_w, moe_router_b, moe_w_gate, moe_w_up, moe_w_down, moe_shared_gate, moe_shared_up, moe_shared_down):
    batch, t_len, d = x.shape
    c_len = ctx.shape[1]
    s_len = t_len + c_len
    depth = ada_w.shape[0]
    assert depth == 2 and batch + 1 <= MOD_ROWS
    assert t_len % ROW_TILE == 0 and c_len % ROW_TILE == 0 and t_len % GRID_W == 0
    ret_heads = ret_decay_fwd.shape[-1]
    diff_heads = d // (2 * DIFF_HEAD_DIM)
    lat_tiles, all_tiles = t_len // ROW_TILE, s_len // ROW_TILE

    cc = jnp.concatenate([c, c_ctx[None], jnp.zeros((MOD_ROWS - batch - 1, d), F32)], axis=0)
    mods = ada_modulation(cc, ada_w, ada_b)
    rope_ret = _rope_tables(t_len, d // ret_heads)
    rope_diff = _rope_tables(t_len, DIFF_HEAD_DIM)

    ident = lambda i: i
    uni_mod = lambda i: jnp.where(i % all_tiles < lat_tiles, i // all_tiles, batch)
    lat_mod = lambda i: i // lat_tiles
    uni_row_mod = lambda r: jnp.where(r % s_len < t_len, r // s_len, batch)
    lat_row_mod = lambda r: r // t_len
    lat_of_uni = lambda i: (i // lat_tiles) * all_tiles + i % lat_tiles

    xs, h = ingest(x, ctx, uni_mod, mods[0], norm_pre_mix[0], 0, 1)
    qkvg = matmul(h, ret_w_in, 0)
    lg_f = jax.nn.log_sigmoid(ret_decay_fwd[0].astype(F32))
    lg_b = jax.nn.log_sigmoid(ret_decay_bwd[0].astype(F32))
    r = retention(qkvg, lg_f, lg_b, *rope_ret, batch=batch, t_len=t_len, c_len=c_len, heads=ret_heads)
    y = matmul(r, ret_w_out, 0)
    xs, h, *route = mixer_norm(
        xs, ident, batch * all_tiles, uni_mod, y, mods[0], norm_post_mix[0], norm_pre_ffn[0],
        moe_router_w[0].T, moe_router_b[0], gate_idx=2, shift_idx=3, scale_idx=4)
    xs, h = moe_ffn(h, route, moe_w_gate, moe_w_up, moe_w_down,
                    moe_shared_gate, moe_shared_up, moe_shared_down, 0, xs, uni_row_mod,
                    mod_a=mods[0], w_post=norm_post_ffn[0], gate_idx=5,
                    mod_b=mods[1], w_pre=norm_pre_mix[1], shift_idx=0, scale_idx=1)

    qkv = matmul(h, diff_w_in, 0)
    lam_vecs = jnp.stack([diff_lam_q1[0], diff_lam_k1[0], diff_lam_q2[0], diff_lam_k2[0]]).astype(F32)
    lambda_init = 0.8 - 0.6 * math.exp(-0.3 * 1)
    a = diff_attention(qkv, lam_vecs, diff_subln_w[0], *rope_diff, batch=batch, t_len=t_len, c_len=c_len,
                       heads=diff_heads, lambda_init=lambda_init)
    y = matmul(a, diff_w_out, 0)
    xl, h, *route = mixer_norm(
        xs, lat_of_uni, batch * lat_tiles, lat_mod, y, mods[1], norm_post_mix[1], norm_pre_ffn[1],
        moe_router_w[1].T, moe_router_b[1], gate_idx=2, shift_idx=3, scale_idx=4)
    (out,) = moe_ffn(h, route, moe_w_gate, moe_w_up, moe_w_down,
                     moe_shared_gate, moe_shared_up, moe_shared_down, 1, xl, lat_row_mod,
                     mod_a=mods[1], w_post=norm_post_ffn[1], gate_idx=5)
    return out.reshape(batch, t_len, d)
```

```python
import functools
import math

import jax
import jax.numpy as jnp
from jax import lax
from jax.experimental import pallas as pl
from jax.experimental.pallas import tpu as pltpu

GRID_W = 64
N_ADA = 6
NORM_EPS = 1e-6
ROPE_BASE = 10000.0
RET_BLOCK = 256
DIFF_HEAD_DIM = 128
MOE_TOPK = 8
MOE_GROUPS = 8
MOE_TOPK_GROUPS = 4
ROUTED_SCALE = 2.5

LANES = 128
SUBLANES = 8
MOD_ROWS = SUBLANES
ROW_TILE = 256
EXPERT_TILE = 256
COMBINE_TILE = 64
DISPATCH_TILE = 256
ATTN_Q_TILES = 8
MIB = 1024 * 1024
V7X_VMEM_BYTES = 64 * MIB
VMEM_LIMIT = V7X_VMEM_BYTES - 8 * MIB
MATMUL_VMEM_LIMIT = V7X_VMEM_BYTES - 4 * MIB
MATMUL_VMEM_BUDGET = MATMUL_VMEM_LIMIT - 7 * MIB

F32 = jnp.float32
BF16 = jnp.bfloat16


def _pick(dim, candidates):
    for c in candidates:
        if dim % c == 0:
            return c
    raise ValueError(f"no tile in {candidates} divides {dim}")


def _params(sem, vmem=VMEM_LIMIT):
    return pltpu.CompilerParams(dimension_semantics=sem, vmem_limit_bytes=vmem)


def _dot(a, b):
    return jnp.dot(a, b, preferred_element_type=F32)


def _dot_nt(a, b):
    return lax.dot_general(a, b, (((1,), (1,)), ((), ())), preferred_element_type=F32)


def _dot_tn(a, b):
    return lax.dot_general(a, b, (((0,), (0,)), ((), ())), preferred_element_type=F32)


def _silu(x):
    return x * jax.nn.sigmoid(x)


def _pack_halves(y):
    w = y.shape[-1] // 2
    lo = lax.bitcast_convert_type(y[:, :w].astype(BF16).astype(F32), jnp.uint32)
    hi = lax.bitcast_convert_type(y[:, w:].astype(BF16).astype(F32), jnp.uint32)
    return (hi & jnp.uint32(0xFFFF0000)) | (lo >> 16)


def _unpack_halves(p):
    lo = lax.bitcast_convert_type(p << 16, F32)
    hi = lax.bitcast_convert_type(p & jnp.uint32(0xFFFF0000), F32)
    return lo, hi


def _ada_kernel(c_ref, w_ref, b_ref, o_ref):
    a = _silu(c_ref[...]).astype(BF16)
    o_ref[...] = _dot(a, w_ref[...].astype(BF16)) + b_ref[...]


def ada_modulation(cc, ada_w, ada_b):
    depth, d, n = ada_w.shape
    tn = _pick(n, (512, 256, 128))
    out = pl.pallas_call(
        _ada_kernel,
        grid=(depth, n // tn),
        in_specs=[pl.BlockSpec((MOD_ROWS, d), lambda l, j: (0, 0)),
                  pl.BlockSpec((None, d, tn), lambda l, j: (l, 0, j)),
                  pl.BlockSpec((None, 1, tn), lambda l, j: (l, 0, j))],
        out_specs=pl.BlockSpec((None, MOD_ROWS, tn), lambda l, j: (l, 0, j)),
        out_shape=jax.ShapeDtypeStruct((depth, MOD_ROWS, n), F32),
        compiler_params=_params(("parallel", "parallel")),
        name="ada_modulation",
    )(cc, ada_w, ada_b.reshape(depth, 1, n))
    return out.reshape(depth, MOD_ROWS, N_ADA, d)


def _rms(x):
    return x * lax.rsqrt(jnp.mean(x * x, axis=-1, keepdims=True) + NORM_EPS)


def _residual(x, y, mod_ref, gate_idx, wpost_ref):
    return x + _rms(y) * (mod_ref[gate_idx:gate_idx + 1, :] * wpost_ref[...])


def _prenorm(x, mod_ref, shift_idx, scale_idx, wpre_ref):
    return (_rms(x) * (wpre_ref[...] * (1.0 + mod_ref[scale_idx:scale_idx + 1, :]))
            + mod_ref[shift_idx:shift_idx + 1, :])


def _route(h, rw_ref, rb_ref, idx_ref, wgt_ref, rank_ref, cnt_ref):
    n_exp = rw_ref.shape[0]
    tm = h.shape[0]
    per_group = n_exp // MOE_GROUPS
    w = rw_ref[...]
    w_hi = w.astype(BF16)
    w_lo = (w - w_hi.astype(F32)).astype(BF16)
    h_hi = h.astype(BF16)
    h_lo = (h - h_hi.astype(F32)).astype(BF16)
    logits = _dot_nt(w_hi, h_hi) + (_dot_nt(w_hi, h_lo) + _dot_nt(w_lo, h_hi))
    scores = jax.nn.sigmoid(logits)
    biased = scores + rb_ref[...]
    neg = jnp.float32(-jnp.inf)
    sub = lax.broadcasted_iota(jnp.int32, (per_group, tm), 0)
    giota = lax.broadcasted_iota(jnp.int32, (MOE_GROUPS, tm), 0)
    gs = jnp.zeros((MOE_GROUPS, tm), F32)
    for g in range(MOE_GROUPS):
        blk = biased[g * per_group:(g + 1) * per_group]
        m1 = jnp.max(blk, axis=0, keepdims=True)
        i1 = jnp.min(jnp.where(blk == m1, sub, per_group), axis=0, keepdims=True)
        m2 = jnp.max(jnp.where(sub == i1, neg, blk), axis=0, keepdims=True)
        gs = jnp.where(giota == g, m1 + m2, gs)
    rank = jnp.zeros((MOE_GROUPS, tm), jnp.int32)
    for j in range(MOE_GROUPS):
        gj = gs[j:j + 1]
        beats = (gj > gs) | ((gj == gs) & (giota > j))
        rank = rank + beats.astype(jnp.int32)
    keep = (rank < MOE_TOPK_GROUPS).astype(F32)
    keep_e = jnp.concatenate(
        [jnp.broadcast_to(keep[g:g + 1], (per_group, tm)) for g in range(MOE_GROUPS)], axis=0)
    masked = jnp.where(keep_e > 0.5, biased, neg)
    eiota = lax.broadcasted_iota(jnp.int32, (n_exp, tm), 0)
    sel_w, hits = [], []
    for k in range(MOE_TOPK):
        m = jnp.max(masked, axis=0, keepdims=True)
        idx = jnp.min(jnp.where(masked == m, eiota, n_exp), axis=0, keepdims=True)
        hit = eiota == idx
        hits.append(hit)
        sel_w.append(jnp.sum(jnp.where(hit, scores, 0.0), axis=0, keepdims=True))
        masked = jnp.where(hit, neg, masked)
        idx_ref[k:k + 1, :] = idx
    total = sel_w[0]
    for k in range(1, MOE_TOPK):
        total = total + sel_w[k]
    for k in range(MOE_TOPK):
        wgt_ref[k:k + 1, :] = sel_w[k] / total * ROUTED_SCALE
    chosen = jnp.zeros((n_exp, tm), F32)
    for k in range(MOE_TOPK):
        chosen = jnp.where(hits[k], 1.0, chosen)
    before = (lax.broadcasted_iota(jnp.int32, (tm, tm), 0)
              < lax.broadcasted_iota(jnp.int32, (tm, tm), 1)).astype(BF16)
    prefix = _dot(chosen.astype(BF16), before)
    for k in range(MOE_TOPK):
        rank_ref[k:k + 1, :] = jnp.sum(jnp.where(hits[k], prefix, 0.0), axis=0,
                                       keepdims=True).astype(jnp.int32)
    cnt_ref[...] = jnp.broadcast_to(jnp.sum(chosen, axis=1, keepdims=True),
                                    cnt_ref.shape).astype(jnp.int32)


def _mixer_norm_kernel(x_ref, y_ref, mod_ref, wpost_ref, wpre_ref, rw_ref, rb_ref,
                       xo_ref, h_ref, idx_ref, wgt_ref, rank_ref, cnt_ref, *, gate_idx, shift_idx, scale_idx):
    x = _residual(x_ref[...], y_ref[...].astype(F32), mod_ref, gate_idx, wpost_ref)
    xo_ref[...] = x
    h = _prenorm(x, mod_ref, shift_idx, scale_idx, wpre_ref)
    h_ref[...] = h.astype(BF16)
    _route(h, rw_ref, rb_ref, idx_ref, wgt_ref, rank_ref, cnt_ref)


def _ingest_kernel(x_ref, c_ref, mod_ref, wpre_ref, xs_ref, h_ref, *, lat_tiles, all_tiles, shift_idx,
                   scale_idx):
    is_latent = pl.program_id(0) % all_tiles < lat_tiles

    def emit(x):
        xs_ref[...] = x
        h_ref[...] = _prenorm(x, mod_ref, shift_idx, scale_idx, wpre_ref).astype(BF16)

    @pl.when(is_latent)
    def _():
        emit(x_ref[...])

    @pl.when(jnp.logical_not(is_latent))
    def _():
        emit(c_ref[...])


def ingest(x, ctx, mod_row_map, mod, w_pre, shift_idx, scale_idx):
    batch, t_len, d = x.shape
    c_len = ctx.shape[1]
    tm = ROW_TILE
    lat_tiles, ctx_tiles = t_len // tm, c_len // tm
    all_tiles = lat_tiles + ctx_tiles
    x_map = lambda i: ((i // all_tiles) * lat_tiles + jnp.minimum(i % all_tiles, lat_tiles - 1), 0)
    c_map = lambda i: ((i // all_tiles) * ctx_tiles + jnp.maximum(i % all_tiles - lat_tiles, 0), 0)
    row = lambda i: (i, 0)
    kern = functools.partial(_ingest_kernel, lat_tiles=lat_tiles, all_tiles=all_tiles, shift_idx=shift_idx,
                             scale_idx=scale_idx)
    n_rows = batch * all_tiles * tm
    return pl.pallas_call(
        kern, grid=(batch * all_tiles,),
        in_specs=[pl.BlockSpec((tm, d), x_map), pl.BlockSpec((tm, d), c_map),
                  pl.BlockSpec((None, N_ADA, d), lambda i: (mod_row_map(i), 0, 0)),
                  pl.BlockSpec((1, d), lambda i: (0, 0))],
        out_specs=[pl.BlockSpec((tm, d), row), pl.BlockSpec((tm, d), row)],
        out_shape=[jax.ShapeDtypeStruct((n_rows, d), F32), jax.ShapeDtypeStruct((n_rows, d), BF16)],
        compiler_params=_params(("arbitrary",)), name="ingest",
    )(x.reshape(batch * t_len, d), ctx.reshape(batch * c_len, d), mod, w_pre.reshape(1, d))


def mixer_norm(x, x_tile_map, n_out_tiles, mod_row_map, y, mod, w_post, w_pre, router_wt, router_b, *,
               gate_idx, shift_idx, scale_idx):
    d = x.shape[-1]
    tm = ROW_TILE
    n_exp = router_wt.shape[0]
    n_rows = n_out_tiles * tm
    row = lambda i: (i, 0)
    const = lambda i: (0, 0)
    per_token = pl.BlockSpec((MOE_TOPK, tm), lambda i: (0, i))
    kern = functools.partial(_mixer_norm_kernel, gate_idx=gate_idx, shift_idx=shift_idx, scale_idx=scale_idx)
    return pl.pallas_call(
        kern, grid=(n_out_tiles,),
        in_specs=[pl.BlockSpec((tm, d), lambda i: (x_tile_map(i), 0)), pl.BlockSpec((tm, d), row),
                  pl.BlockSpec((None, N_ADA, d), lambda i: (mod_row_map(i), 0, 0)),
                  pl.BlockSpec((1, d), const), pl.BlockSpec((1, d), const),
                  pl.BlockSpec((n_exp, d), const), pl.BlockSpec((n_exp, 1), const)],
        out_specs=[pl.BlockSpec((tm, d), row), pl.BlockSpec((tm, d), row),
                   per_token, per_token, per_token,
                   pl.BlockSpec((None, n_exp, LANES), lambda i: (i, 0, 0))],
        out_shape=[jax.ShapeDtypeStruct((n_rows, d), F32), jax.ShapeDtypeStruct((n_rows, d), BF16),
                   jax.ShapeDtypeStruct((MOE_TOPK, n_rows), jnp.int32),
                   jax.ShapeDtypeStruct((MOE_TOPK, n_rows), F32),
                   jax.ShapeDtypeStruct((MOE_TOPK, n_rows), jnp.int32),
                   jax.ShapeDtypeStruct((n_out_tiles, n_exp, LANES), jnp.int32)],
        compiler_params=_params(("parallel",)), name="mixer_norm",
    )(x, y, mod, w_post.reshape(1, d), w_pre.reshape(1, d), router_wt, router_b.reshape(n_exp, 1))


def _mm_kernel(a_ref, w_ref, o_ref):
    o_ref[...] = _dot(a_ref[...], w_ref[...].astype(BF16)).astype(o_ref.dtype)


def _mm_acc_kernel(a_ref, w_ref, o_ref, acc_ref):
    k = pl.program_id(2)

    @pl.when(k == 0)
    def _():
        acc_ref[...] = jnp.zeros_like(acc_ref)

    acc_ref[...] += _dot(a_ref[...], w_ref[...].astype(BF16))

    @pl.when(k == pl.num_programs(2) - 1)
    def _():
        o_ref[...] = acc_ref[...].astype(o_ref.dtype)


def matmul(a, w, layer, out_dtype=BF16):
    m, k = a.shape
    n = w.shape[2]
    tm = _pick(m, (1024, 768, 512, 256))

    def vmem_bytes(tn, tk):
        return (2 * tm * tk * 2 + 2 * tk * tn * 4 + tk * tn * 2 + 2 * tm * tn * 2
                + (tm * tn * 4 if tk < k else 0))

    tn, tk = next(((tn, tk) for tk in (k, k // 2, k // 4) for tn in (512, 256)
                   if n % tn == 0 and tk % LANES == 0 and vmem_bytes(tn, tk) <= MATMUL_VMEM_BUDGET),
                  (_pick(n, (256, 128)), _pick(k, (1024, 512, 256, 128))))
    if tk == k:
        return pl.pallas_call(
            _mm_kernel, grid=(m // tm, n // tn),
            in_specs=[pl.BlockSpec((tm, k), lambda i, j: (i, 0)),
                      pl.BlockSpec((None, k, tn), lambda i, j: (layer, 0, j))],
            out_specs=pl.BlockSpec((tm, tn), lambda i, j: (i, j)),
            out_shape=jax.ShapeDtypeStruct((m, n), out_dtype),
            compiler_params=_params(("parallel", "parallel"), MATMUL_VMEM_LIMIT), name="matmul",
        )(a, w)
    return pl.pallas_call(
        _mm_acc_kernel, grid=(m // tm, n // tn, k // tk),
        in_specs=[pl.BlockSpec((tm, tk), lambda i, j, l: (i, l)),
                  pl.BlockSpec((None, tk, tn), lambda i, j, l: (layer, l, j))],
        out_specs=pl.BlockSpec((tm, tn), lambda i, j, l: (i, j)),
        out_shape=jax.ShapeDtypeStruct((m, n), out_dtype),
        scratch_shapes=[pltpu.VMEM((tm, tn), F32)],
        compiler_params=_params(("parallel", "parallel", "arbitrary")), name="matmul_acc",
    )(a, w)


def _rope_tables(t_len, head_dim):
    rows = t_len // GRID_W
    n_freq = head_dim // 4
    row, col = jnp.meshgrid(jnp.arange(rows, dtype=F32), jnp.arange(GRID_W, dtype=F32), indexing="ij")
    inv_freq = ROPE_BASE ** (-jnp.arange(n_freq, dtype=F32) / n_freq)
    ang_r = row.reshape(-1, 1) * inv_freq
    ang_c = col.reshape(-1, 1) * inv_freq
    cr, sr, cc, sc = jnp.cos(ang_r), jnp.sin(ang_r), jnp.cos(ang_c), jnp.sin(ang_c)
    return (jnp.concatenate([cr, cr, cc, cc], axis=-1), jnp.concatenate([-sr, sr, -sc, sc], axis=-1))


def _swap_quarters(x, quarter):
    lane = lax.broadcasted_iota(jnp.int32, x.shape, 1)
    first = (lane % (2 * quarter)) < quarter
    return jnp.where(first, pltpu.roll(x, LANES - quarter, axis=1), pltpu.roll(x, quarter, axis=1))


def _retention_kernel(lgf_ref, lgb_ref, q_ref, k_ref, v_ref, g_ref, cos_ref, sin_ref, o_ref,
                      qr, kr, oacc_f, oacc_b, state_f, state_b, *, t_len, c_len):
    head = pl.program_id(1)
    L = RET_BLOCK
    dk = q_ref.shape[-1]
    k_scale = dk ** -0.5

    def rope(x, rows):
        sw = jnp.concatenate([pltpu.roll(x[:, :LANES], LANES // 2, axis=1),
                              pltpu.roll(x[:, LANES:], LANES // 2, axis=1)], axis=1)
        return x * cos_ref[rows, :] + sw * sin_ref[rows, :]

    def prepare(row0):
        rows = slice(row0, row0 + L)
        if row0 < t_len:
            qr[rows, :] = rope(q_ref[rows, :].astype(F32), rows).astype(BF16)
            kr[rows, :] = (rope(k_ref[rows, :].astype(F32), rows) * k_scale).astype(BF16)
        else:
            qr[rows, :] = q_ref[rows, :]
            kr[rows, :] = (k_ref[rows, :].astype(F32) * k_scale).astype(BF16)

    ii = lax.broadcasted_iota(jnp.int32, (L, L), 0)
    jj = lax.broadcasted_iota(jnp.int32, (L, L), 1)
    rel = (ii - jj).astype(F32)
    idx = lax.broadcasted_iota(jnp.int32, (L, 1), 0).astype(F32)

    lg_f, lg_b = lgf_ref[head], lgb_ref[head]
    fwd = (jnp.where(rel >= 0, jnp.exp(lg_f * jnp.maximum(rel, 0.0)), 0.0),
           jnp.exp(lg_f * (idx + 1.0)),
           jnp.exp(lg_f * (L - 1.0 - idx)),
           jnp.exp(lg_f * L), state_f, oacc_f)
    bwd = (jnp.where(rel <= 0, jnp.exp(lg_b * jnp.maximum(-rel, 0.0)), 0.0),
           jnp.exp(lg_b * (L - idx)),
           jnp.exp(lg_b * idx),
           jnp.exp(lg_b * L), state_b, oacc_b)

    def chunk(row0, direction):
        dmat, q_decay, k_decay, chunk_decay, state, oacc = direction
        rows = slice(row0, row0 + L)
        qb, kb, vb = qr[rows, :], kr[rows, :], v_ref[rows, :]
        scores = _dot_nt(qb, kb) * dmat
        inner = _dot(scores.astype(BF16), vb)
        st = state[...]
        cross = _dot(qb, st.astype(BF16)) * q_decay
        oacc[rows, :] = inner + cross
        kd = (kb.astype(F32) * k_decay).astype(BF16)
        state[...] = st * chunk_decay + _dot_tn(kd, vb)

    def finish(row0):
        rows = slice(row0, row0 + L)
        o = _rms(oacc_f[rows, :] + oacc_b[rows, :])
        o_ref[rows, :] = (_silu(g_ref[rows, :].astype(F32)) * o).astype(BF16)

    def scan(r0, n_chunks):
        ready, done_f, done_b = set(), set(), set()
        for ci in range(n_chunks):
            cf, cb = ci, n_chunks - 1 - ci
            for c in (cf, cb):
                if c not in ready:
                    prepare(r0 + c * L)
                    ready.add(c)
            chunk(r0 + cf * L, fwd)
            chunk(r0 + cb * L, bwd)
            done_f.add(cf)
            done_b.add(cb)
            for c in sorted({cf, cb}):
                if c in done_f and c in done_b:
                    finish(r0 + c * L)

    state_f[...] = jnp.zeros_like(state_f)
    state_b[...] = jnp.zeros_like(state_b)
    scan(t_len, c_len // L)
    scan(0, t_len // L)


def retention(qkvg, lg_f, lg_b, cos, sin, *, batch, t_len, c_len, heads):
    s_len = t_len + c_len
    dk = cos.shape[-1]
    dv = 2 * dk
    assert dk == 2 * LANES
    kern = functools.partial(_retention_kernel, t_len=t_len, c_len=c_len)
    grid_spec = pltpu.PrefetchScalarGridSpec(
        num_scalar_prefetch=2, grid=(batch, heads),
        in_specs=[pl.BlockSpec((s_len, dk), lambda b, h, *_: (b, h)),
                  pl.BlockSpec((s_len, dk), lambda b, h, *_: (b, heads + h)),
                  pl.BlockSpec((s_len, dv), lambda b, h, *_: (b, heads + h)),
                  pl.BlockSpec((s_len, dv), lambda b, h, *_: (b, 2 * heads + h)),
                  pl.BlockSpec((t_len, dk), lambda b, h, *_: (0, 0)),
                  pl.BlockSpec((t_len, dk), lambda b, h, *_: (0, 0))],
        out_specs=pl.BlockSpec((s_len, dv), lambda b, h, *_: (b, h)),
        scratch_shapes=[pltpu.VMEM((s_len, dk), BF16), pltpu.VMEM((s_len, dk), BF16),
                        pltpu.VMEM((s_len, dv), F32), pltpu.VMEM((s_len, dv), F32),
                        pltpu.VMEM((dk, dv), F32), pltpu.VMEM((dk, dv), F32)])
    return pl.pallas_call(
        kern, grid_spec=grid_spec,
        out_shape=jax.ShapeDtypeStruct((batch * s_len, heads * dv), BF16),
        compiler_params=_params(("parallel", "parallel")), name="retention",
    )(lg_f, lg_b, qkvg, qkvg, qkvg, qkvg, cos, sin)


def _diff_attn_kernel(lam_ref, *refs, n_q, t_len, c_len, lambda_init):
    q_refs = refs[:n_q]
    k_ref, v_ref, cosq_ref, sinq_ref, cosk_ref, sink_ref, subln_ref, o_ref, kr = refs[n_q:]
    hd = DIFF_HEAD_DIM
    quarter = hd // 4
    rt = ROW_TILE

    def rope(x, cos, sin):
        return x * cos + _swap_quarters(x, quarter) * sin

    @pl.when(pl.program_id(2) == 0)
    def _():
        def body(i, _):
            rows = pl.ds(pl.multiple_of(i * rt, rt), rt)
            for c in range(2):
                cols = slice(c * hd, (c + 1) * hd)
                kr[rows, cols] = rope(k_ref[rows, cols].astype(F32), cosk_ref[rows, :],
                                      sink_ref[rows, :]).astype(BF16)
            return 0
        lax.fori_loop(0, t_len // rt, body, 0)
        ctx_rows = pl.ds(t_len, c_len)
        kr[ctx_rows, :] = k_ref[ctx_rows, :]

    lam_v = lam_ref[...]
    lam = (jnp.exp(jnp.sum(lam_v[0:1] * lam_v[1:2], axis=-1, keepdims=True))
           - jnp.exp(jnp.sum(lam_v[2:3] * lam_v[3:4], axis=-1, keepdims=True)) + lambda_init)
    q_scale = (hd ** -0.5) * math.log2(math.e)
    tq = q_refs[0].shape[0]
    for part, q_ref in enumerate(q_refs):
        rows = slice(part * tq, (part + 1) * tq)
        outs = []
        for c in range(2):
            cols = slice(c * hd, (c + 1) * hd)
            qc = (rope(q_ref[:, cols].astype(F32), cosq_ref[rows, :], sinq_ref[rows, :])
                  * q_scale).astype(BF16)
            s = _dot_nt(qc, kr[:, cols])
            e = jnp.exp2(s - jnp.max(s, axis=-1, keepdims=True))
            denom = jnp.sum(e, axis=-1, keepdims=True)
            outs.append(_dot(e.astype(BF16), v_ref[...]) / denom)
        o = outs[0] - lam * outs[1]
        o_ref[rows, :] = ((_rms(o) * subln_ref[...]) * (1.0 - lambda_init)).astype(BF16)


def diff_attention(qkv, lam_vecs, subln_w, cos, sin, *, batch, t_len, c_len, heads, lambda_init):
    s_len = t_len + c_len
    hd = DIFF_HEAD_DIM
    tq = ROW_TILE
    n_q = _pick(t_len // tq, (ATTN_Q_TILES, 2, 1))
    steps, s_tiles = t_len // (n_q * tq), s_len // tq
    kern = functools.partial(_diff_attn_kernel, n_q=n_q, t_len=t_len, c_len=c_len, lambda_init=lambda_init)
    q_specs = [pl.BlockSpec((tq, 2 * hd), lambda b, h, i, j=j: (b * s_tiles + n_q * i + j, h))
               for j in range(n_q)]
    return pl.pallas_call(
        kern, grid=(batch, heads, steps),
        in_specs=[pl.BlockSpec((4, hd), lambda b, h, i: (0, 0)), *q_specs,
                  pl.BlockSpec((s_len, 2 * hd), lambda b, h, i: (b, heads + h)),
                  pl.BlockSpec((s_len, 2 * hd), lambda b, h, i: (b, 2 * heads + h)),
                  pl.BlockSpec((n_q * tq, hd), lambda b, h, i: (i, 0)),
                  pl.BlockSpec((n_q * tq, hd), lambda b, h, i: (i, 0)),
                  pl.BlockSpec((t_len, hd), lambda b, h, i: (0, 0)),
                  pl.BlockSpec((t_len, hd), lambda b, h, i: (0, 0)),
                  pl.BlockSpec((1, 2 * hd), lambda b, h, i: (0, 0))],
        out_specs=pl.BlockSpec((n_q * tq, 2 * hd), lambda b, h, i: (b * steps + i, h)),
        out_shape=jax.ShapeDtypeStruct((batch * t_len, heads * 2 * hd), BF16),
        scratch_shapes=[pltpu.VMEM((s_len, 2 * hd), BF16)],
        compiler_params=_params(("parallel", "parallel", "arbitrary")), name="diff_attention",
    )(lam_vecs, *([qkv] * (n_q + 2)), cos, sin, cos, sin, subln_w.reshape(1, 2 * hd))


def _shared_kernel(x_ref, wg_ref, wu_ref, wd_ref, o_ref, acc_ref):
    f = pl.program_id(1)

    @pl.when(f == 0)
    def _():
        acc_ref[...] = jnp.zeros_like(acc_ref)

    x = x_ref[...]
    hid = _silu(_dot(x, wg_ref[...])) * _dot(x, wu_ref[...])
    acc_ref[...] += _dot(hid.astype(BF16), wd_ref[...])

    @pl.when(f == pl.num_programs(1) - 1)
    def _():
        o_ref[...] = acc_ref[...].astype(o_ref.dtype)


def shared_expert(h, w_gate, w_up, w_down):
    m, d = h.shape
    f_dim = w_gate.shape[1]
    tm = _pick(m, (512, 256))
    tf = _pick(f_dim, (512, 256, 128))
    return pl.pallas_call(
        _shared_kernel, grid=(m // tm, f_dim // tf),
        in_specs=[pl.BlockSpec((tm, d), lambda i, f: (i, 0)),
                  pl.BlockSpec((d, tf), lambda i, f: (0, f)),
                  pl.BlockSpec((d, tf), lambda i, f: (0, f)),
                  pl.BlockSpec((tf, d), lambda i, f: (f, 0))],
        out_specs=pl.BlockSpec((tm, d), lambda i, f: (i, 0)),
        out_shape=jax.ShapeDtypeStruct((m, d), BF16),
        scratch_shapes=[pltpu.VMEM((tm, d), F32)],
        compiler_params=_params(("parallel", "arbitrary")), name="shared_expert",
    )(h, w_gate, w_up, w_down)


def _dispatch_plan(cnt, pairs, tm):
    n_exp = cnt.shape[1]
    total = jnp.sum(cnt, axis=0)
    padded = (total + tm - 1) // tm * tm
    pend = jnp.cumsum(padded)
    base = (pend - padded)[None, :] + jnp.cumsum(cnt, axis=0) - cnt
    n_tiles = pairs // tm + n_exp
    tile_start = jnp.arange(n_tiles, dtype=jnp.int32) * tm
    valid = tile_start < pend[-1]
    tile_e = jnp.minimum(jnp.sum((tile_start[:, None] >= pend[None, :]).astype(jnp.int32), axis=1), n_exp - 1)
    tile_e = jnp.where(valid, tile_e, jnp.max(jnp.where(valid, tile_e, 0)))
    next_e = jnp.concatenate([tile_e[1:], jnp.full((1,), -1, jnp.int32)])
    next_valid = jnp.concatenate([valid[1:], jnp.zeros((1,), bool)])
    zero_fill = (~valid) | (tile_e != next_e) | (~next_valid)
    return (tile_e.astype(jnp.int32), valid.astype(jnp.int32), zero_fill.astype(jnp.int32),
            base.astype(jnp.int32))


def _positions_kernel(idx_ref, rank_ref, base_ref, pos_ref):
    n_exp = base_ref.shape[0]
    tm = idx_ref.shape[1]
    eiota = lax.broadcasted_iota(jnp.int32, (n_exp, tm), 0)
    base = jnp.broadcast_to(base_ref[...].astype(F32), (n_exp, tm))
    for k in range(MOE_TOPK):
        hit = eiota == idx_ref[k:k + 1, :]
        first = jnp.sum(jnp.where(hit, base, 0.0), axis=0, keepdims=True)
        pos_ref[k:k + 1, :] = first.astype(jnp.int32) + rank_ref[k:k + 1, :]


def pair_positions(idx_t, rank_t, base):
    k, n = idx_t.shape
    tiles, n_exp = base.shape
    tm = n // tiles
    spec = pl.BlockSpec((k, tm), lambda i: (0, i))
    return pl.pallas_call(
        _positions_kernel, grid=(tiles,),
        in_specs=[spec, spec, pl.BlockSpec((None, n_exp, 1), lambda i: (i, 0, 0))],
        out_specs=spec, out_shape=jax.ShapeDtypeStruct((k, n), jnp.int32),
        compiler_params=_params(("parallel",)), name="pair_positions",
    )(idx_t, rank_t, base.reshape(tiles, n_exp, 1))


def _dispatch_kernel(pos_ref, zf_ref, h_ref, wg_ref, wu_ref, wd_ref, xs_hbm, wgo_ref, wuo_ref, wdo_ref,
                     hp_ref, zeros, sem_z, sem_s, *, n_tok, n_tiles):
    i = pl.program_id(0)
    td = hp_ref.shape[0]
    tm = zeros.shape[0]
    hp_ref[...] = _pack_halves(h_ref[...].astype(F32))
    wgo_ref[...] = wg_ref[...].astype(BF16)
    wuo_ref[...] = wu_ref[...].astype(BF16)
    wdo_ref[...] = wd_ref[...].astype(BF16)

    def zero_copy(j):
        return pltpu.make_async_copy(zeros, xs_hbm.at[pl.ds(pl.multiple_of(j * tm, tm), tm)], sem_z)

    @pl.when(i == 0)
    def _():
        zeros[...] = jnp.zeros_like(zeros)

        def start(j, _):
            @pl.when(zf_ref[j] == 1)
            def _():
                zero_copy(j).start()
            return 0

        def wait(j, _):
            @pl.when(zf_ref[j] == 1)
            def _():
                zero_copy(j).wait()
            return 0

        lax.fori_loop(0, n_tiles, start, 0)
        lax.fori_loop(0, n_tiles, wait, 0)

    def body(t, _):
        for k in range(MOE_TOPK):
            row = pos_ref[k * n_tok + i * td + t]
            pltpu.make_async_copy(hp_ref.at[pl.ds(t, 1)], xs_hbm.at[pl.ds(row, 1)], sem_s).start()
        return 0

    lax.fori_loop(0, td, body, 0, unroll=2)
    for k in range(MOE_TOPK):
        pltpu.make_async_copy(hp_ref, xs_hbm.at[pl.ds(0, td)], sem_s).wait()


def dispatch(h, pos, zero_fill, n_tiles, sh_gate, sh_up, sh_down, layer):
    n_tok, half = h.shape[0], h.shape[1] // 2
    tm = EXPERT_TILE
    steps = n_tok // DISPATCH_TILE
    _, d, f = sh_gate.shape
    bf16_rows = 2 * SUBLANES
    nblk = max(n for n in range(1, steps + 1)
               if d % n == 0 and f % n == 0 and (d // n) % bf16_rows == 0 and (f // n) % bf16_rows == 0)
    wblk = lambda i, p, z: (layer, jnp.minimum(i, nblk - 1), 0)
    oblk = lambda i, p, z: (jnp.minimum(i, nblk - 1), 0)
    grid_spec = pltpu.PrefetchScalarGridSpec(
        num_scalar_prefetch=2, grid=(steps,),
        in_specs=[pl.BlockSpec((DISPATCH_TILE, 2 * half), lambda i, p, z: (i, 0)),
                  pl.BlockSpec((None, d // nblk, f), wblk),
                  pl.BlockSpec((None, d // nblk, f), wblk),
                  pl.BlockSpec((None, f // nblk, d), wblk)],
        out_specs=[pl.BlockSpec(memory_space=pl.ANY),
                   pl.BlockSpec((d // nblk, f), oblk),
                   pl.BlockSpec((d // nblk, f), oblk),
                   pl.BlockSpec((f // nblk, d), oblk)],
        scratch_shapes=[pltpu.VMEM((DISPATCH_TILE, half), jnp.uint32), pltpu.VMEM((tm, half), jnp.uint32),
                        pltpu.SemaphoreType.DMA(()), pltpu.SemaphoreType.DMA(())])
    return pl.pallas_call(
        functools.partial(_dispatch_kernel, n_tok=n_tok, n_tiles=n_tiles), grid_spec=grid_spec,
        out_shape=[jax.ShapeDtypeStruct((n_tiles * tm, half), jnp.uint32),
                   jax.ShapeDtypeStruct((d, f), BF16), jax.ShapeDtypeStruct((d, f), BF16),
                   jax.ShapeDtypeStruct((f, d), BF16)],
        compiler_params=_params(("arbitrary",)), name="moe_dispatch",
    )(pos, zero_fill, h, sh_gate, sh_up, sh_down)


def _expert_kernel(te_ref, tv_ref, first_ref, slot_ref, next_ref, src_ref, x_ref, wg_hbm, wu_hbm, wd_hbm, o_ref,
                   stage_g, stage_u, stage_d, wgb, wub, wdb, sems, *, layer):
    i = pl.program_id(0)

    def weight_copies(e, s):
        return (pltpu.make_async_copy(wg_hbm.at[layer, e], stage_g.at[s], sems.at[s]),
                pltpu.make_async_copy(wu_hbm.at[layer, e], stage_u.at[s], sems.at[s]),
                pltpu.make_async_copy(wd_hbm.at[layer, e], stage_d.at[s], sems.at[s]))

    @pl.when(i == 0)
    def _():
        for cp in weight_copies(te_ref[0], 0):
            cp.start()

    @pl.when(first_ref[i] == 1)
    def _():
        s = slot_ref[i]
        for cp in weight_copies(te_ref[i], s):
            cp.wait()

        @pl.when(next_ref[i] >= 0)
        def _():
            for cp in weight_copies(next_ref[i], 1 - s):
                cp.start()

        wgb[...] = stage_g[s].astype(BF16)
        wub[...] = stage_u[s].astype(BF16)
        wdb[...] = stage_d[s].astype(BF16)

    @pl.when(tv_ref[i] == 1)
    def _():
        lo, hi = _unpack_halves(x_ref[...])
        x = jnp.concatenate([lo.astype(BF16), hi.astype(BF16)], axis=1)
        hid = _silu(_dot(x, wgb[...])) * _dot(x, wub[...])
        o_ref[...] = _pack_halves(_dot(hid.astype(BF16), wdb[...]))

    @pl.when(tv_ref[i] == 0)
    def _():
        o_ref[...] = jnp.zeros_like(o_ref)


def routed_experts(xs, tile_e, tile_valid, w_gate, w_up, w_down, layer):
    tm = EXPERT_TILE
    n_tiles = tile_e.shape[0]
    _, n_exp, d, f = w_gate.shape
    prev_e = jnp.concatenate([jnp.full((1,), -1, jnp.int32), tile_e[:-1]])
    first = (tile_e != prev_e).astype(jnp.int32)
    slot = (jnp.cumsum(first) - 1) % 2
    tiles = jnp.arange(n_tiles, dtype=jnp.int32)
    run_start = jnp.where(first == 1, tiles, n_tiles)
    next_start = jnp.min(jnp.where(run_start[None, :] > tiles[:, None], run_start[None, :], n_tiles), axis=1)
    next_e = jnp.where(next_start < n_tiles, tile_e[jnp.minimum(next_start, n_tiles - 1)], -1)
    idx_map = lambda i, *_: (i, 0)
    src_tile = jnp.where(tile_valid == 1, tiles, jnp.sum(tile_valid) - 1).astype(jnp.int32)
    src_map = lambda i, te, tv, fi, sl, nx, st: (st[i], 0)
    anyspace = pl.BlockSpec(memory_space=pl.ANY)
    grid_spec = pltpu.PrefetchScalarGridSpec(
        num_scalar_prefetch=6, grid=(n_tiles,),
        in_specs=[pl.BlockSpec((tm, d // 2), src_map), anyspace, anyspace, anyspace],
        out_specs=pl.BlockSpec((tm, d // 2), idx_map),
        scratch_shapes=[pltpu.VMEM((2, d, f), F32), pltpu.VMEM((2, d, f), F32), pltpu.VMEM((2, f, d), F32),
                        pltpu.VMEM((d, f), BF16), pltpu.VMEM((d, f), BF16), pltpu.VMEM((f, d), BF16),
                        pltpu.SemaphoreType.DMA((2,))])
    return pl.pallas_call(
        functools.partial(_expert_kernel, layer=layer), grid_spec=grid_spec,
        out_shape=jax.ShapeDtypeStruct((n_tiles * tm, d // 2), jnp.uint32),
        compiler_params=_params(("arbitrary",)), name="routed_experts",
    )(tile_e, tile_valid, first, slot.astype(jnp.int32), next_e.astype(jnp.int32), src_tile,
      xs, w_gate, w_up, w_down)


def _combine_kernel(pos_ref, ys_hbm, sh_ref, w_ref, x_ref, mod_a_ref, wpost_ref, *refs, n_tok, has_prenorm,
                    gate_idx, shift_idx, scale_idx):
    if has_prenorm:
        mod_b_ref, wpre_ref, xo_ref, h_ref, buf_a, buf_b, sem_a, sem_b = refs
    else:
        xo_ref, buf_a, buf_b, sem_a, sem_b = refs
    i = pl.program_id(0)
    n = pl.num_programs(0)
    tc = buf_a.shape[1]
    half = sh_ref.shape[-1] // 2
    last_tile = 2 * n - 1

    def issue(tile, buf, sem):
        for t in range(tc):
            for k in range(MOE_TOPK):
                p = pos_ref[k * n_tok + tile * tc + t]
                pltpu.make_async_copy(ys_hbm.at[pl.ds(p, 1)], buf.at[k, pl.ds(t, 1)], sem).start()

    def wait(buf, sem):
        for k in range(MOE_TOPK):
            pltpu.make_async_copy(ys_hbm.at[pl.ds(0, tc)], buf.at[k], sem).wait()

    def reduce(buf, rows):
        sh = sh_ref[rows, :].astype(F32)
        w = w_ref[rows, :]
        lo_acc, hi_acc = sh[:, :half], sh[:, half:]
        for k in range(MOE_TOPK):
            lo, hi = _unpack_halves(buf[k])
            lo_acc = lo_acc + w[:, k:k + 1] * lo
            hi_acc = hi_acc + w[:, k:k + 1] * hi
        ffn = jnp.concatenate([lo_acc, hi_acc], axis=1)
        x = _residual(x_ref[rows, :], ffn, mod_a_ref, gate_idx, wpost_ref)
        xo_ref[rows, :] = x
        if has_prenorm:
            h_ref[rows, :] = _prenorm(x, mod_b_ref, shift_idx, scale_idx, wpre_ref).astype(BF16)

    @pl.when(i == 0)
    def _():
        issue(0, buf_a, sem_a)

    wait(buf_a, sem_a)
    issue(2 * i + 1, buf_b, sem_b)
    reduce(buf_a, slice(0, tc))
    wait(buf_b, sem_b)
    issue(jnp.minimum(2 * i + 2, last_tile), buf_a, sem_a)
    reduce(buf_b, slice(tc, 2 * tc))

    @pl.when(i == n - 1)
    def _():
        wait(buf_a, sem_a)


def combine(ys, pos, shared, wgt, x, mod_row_map, *, mod_a, w_post, gate_idx, mod_b=None, w_pre=None,
            shift_idx=0, scale_idx=0):
    n_tok, d = shared.shape
    tc = COMBINE_TILE
    step = 2 * tc
    has_prenorm = mod_b is not None
    buf = pltpu.VMEM((MOE_TOPK, tc, d // 2), jnp.uint32)
    row = lambda i, p: (i, 0)
    const = lambda i, p: (0, 0)
    mod_spec = pl.BlockSpec((None, N_ADA, d), lambda i, p: (mod_row_map(i * step), 0, 0))
    vec_spec = pl.BlockSpec((1, d), const)
    args = [pos, ys, shared, wgt, x, mod_a, w_post.reshape(1, d)]
    in_specs = [pl.BlockSpec(memory_space=pl.ANY), pl.BlockSpec((step, d), row),
                pl.BlockSpec((step, MOE_TOPK), row), pl.BlockSpec((step, d), row), mod_spec, vec_spec]
    out_shape = [jax.ShapeDtypeStruct((n_tok, d), F32)]
    out_specs = [pl.BlockSpec((step, d), row)]
    if has_prenorm:
        args += [mod_b, w_pre.reshape(1, d)]
        in_specs += [mod_spec, vec_spec]
        out_shape.append(jax.ShapeDtypeStruct((n_tok, d), BF16))
        out_specs.append(pl.BlockSpec((step, d), row))
    grid_spec = pltpu.PrefetchScalarGridSpec(
        num_scalar_prefetch=1, grid=(n_tok // step,), in_specs=in_specs, out_specs=out_specs,
        scratch_shapes=[buf, buf, pltpu.SemaphoreType.DMA(()), pltpu.SemaphoreType.DMA(())])
    kern = functools.partial(_combine_kernel, n_tok=n_tok, has_prenorm=has_prenorm, gate_idx=gate_idx,
                             shift_idx=shift_idx, scale_idx=scale_idx)
    return pl.pallas_call(
        kern, grid_spec=grid_spec, out_shape=out_shape,
        compiler_params=_params(("arbitrary",)), name="moe_combine",
    )(*args)


def moe_ffn(h, route, w_gate, w_up, w_down, sh_gate, sh_up, sh_down, layer, x, mod_row_map, **epilogue):
    idx_t, wgt_t, rank_t, cnt = route
    pairs = idx_t.shape[0] * idx_t.shape[1]
    tile_e, tile_valid, zero_fill, base = _dispatch_plan(cnt[:, :, 0], pairs, EXPERT_TILE)
    pos = pair_positions(idx_t, rank_t, base).reshape(pairs)
    xs, sg, su, sd = dispatch(h, pos, zero_fill, tile_e.shape[0], sh_gate, sh_up, sh_down, layer)
    ys = routed_experts(xs, tile_e, tile_valid, w_gate, w_up, w_down, layer)
    shared = shared_expert(h, sg, su, sd)
    return combine(ys, pos, shared, wgt_t.T, x, mod_row_map, **epilogue)


def kernel(x, c, ctx, c_ctx, ada_w, ada_b, norm_pre_mix, norm_post_mix, norm_pre_ffn, norm_post_ffn, ret_w_in, ret_w_out, ret_decay_fwd, ret_decay_bwd, diff_w_in, diff_w_out, diff_lam_q1, diff_lam_k1, diff_lam_q2, diff_lam_k2, diff_subln_w, moe_router_w, moe_router_b, moe_w_gate, moe_w_up, moe_w_down, moe_shared_gate, moe_shared_up, moe_shared_down):
    batch, t_len, d = x.shape
    c_len = ctx.shape[1]
    s_len = t_len + c_len
    depth = ada_w.shape[0]
    assert depth == 2 and batch + 1 <= MOD_ROWS
    assert t_len % ROW_TILE == 0 and c_len % ROW_TILE == 0 and t_len % GRID_W == 0
    ret_heads = ret_decay_fwd.shape[-1]
    diff_heads = d // (2 * DIFF_HEAD_DIM)
    lat_tiles, all_tiles = t_len // ROW_TILE, s_len // ROW_TILE

    cc = jnp.concatenate([c, c_ctx[None], jnp.zeros((MOD_ROWS - batch - 1, d), F32)], axis=0)
    mods = ada_modulation(cc, ada_w, ada_b)
    rope_ret = _rope_tables(t_len, d // ret_heads)
    rope_diff = _rope_tables(t_len, DIFF_HEAD_DIM)

    ident = lambda i: i
    uni_mod = lambda i: jnp.where(i % all_tiles < lat_tiles, i // all_tiles, batch)
    lat_mod = lambda i: i // lat_tiles
    uni_row_mod = lambda r: jnp.where(r % s_len < t_len, r // s_len, batch)
    lat_row_mod = lambda r: r // t_len
    lat_of_uni = lambda i: (i // lat_tiles) * all_tiles + i % lat_tiles

    xs, h = ingest(x, ctx, uni_mod, mods[0], norm_pre_mix[0], 0, 1)
    qkvg = matmul(h, ret_w_in, 0)
    lg_f = jax.nn.log_sigmoid(ret_decay_fwd[0].astype(F32))
    lg_b = jax.nn.log_sigmoid(ret_decay_bwd[0].astype(F32))
    r = retention(qkvg, lg_f, lg_b, *rope_ret, batch=batch, t_len=t_len, c_len=c_len, heads=ret_heads)
    y = matmul(r, ret_w_out, 0)
    xs, h, *route = mixer_norm(
        xs, ident, batch * all_tiles, uni_mod, y, mods[0], norm_post_mix[0], norm_pre_ffn[0],
        moe_router_w[0].T, moe_router_b[0], gate_idx=2, shift_idx=3, scale_idx=4)
    xs, h = moe_ffn(h, route, moe_w_gate, moe_w_up, moe_w_down,
                    moe_shared_gate, moe_shared_up, moe_shared_down, 0, xs, uni_row_mod,
                    mod_a=mods[0], w_post=norm_post_ffn[0], gate_idx=5,
                    mod_b=mods[1], w_pre=norm_pre_mix[1], shift_idx=0, scale_idx=1)

    qkv = matmul(h, diff_w_in, 0)
    lam_vecs = jnp.stack([diff_lam_q1[0], diff_lam_k1[0], diff_lam_q2[0], diff_lam_k2[0]]).astype(F32)
    lambda_init = 0.8 - 0.6 * math.exp(-0.3 * 1)
    a = diff_attention(qkv, lam_vecs, diff_subln_w[0], *rope_diff, batch=batch, t_len=t_len, c_len=c_len,
                       heads=diff_heads, lambda_init=lambda_init)
    y = matmul(a, diff_w_out, 0)
    xl, h, *route = mixer_norm(
        xs, lat_of_uni, batch * lat_tiles, lat_mod, y, mods[1], norm_post_mix[1], norm_pre_ffn[1],
        moe_router_w[1].T, moe_router_b[1], gate_idx=2, shift_idx=3, scale_idx=4)
    (out,) = moe_ffn(h, route, moe_w_gate, moe_w_up, moe_w_down,
                     moe_shared_gate, moe_shared_up, moe_shared_down, 1, xl, lat_row_mod,
                     mod_a=mods[1], w_post=norm_post_ffn[1], gate_idx=5)
    return out.reshape(batch, t_len, d)
```

```python
import functools
import math

import jax
import jax.numpy as jnp
from jax import lax
from jax.experimental import pallas as pl
from jax.experimental.pallas import tpu as pltpu

GRID_W = 64
N_ADA = 6
NORM_EPS = 1e-6
ROPE_BASE = 10000.0
RET_BLOCK = 256
DIFF_HEAD_DIM = 128
MOE_TOPK = 8
MOE_GROUPS = 8
MOE_TOPK_GROUPS = 4
ROUTED_SCALE = 2.5

LANES = 128
SUBLANES = 8
MOD_ROWS = SUBLANES
ROW_TILE = 256
EXPERT_TILE = 256
COMBINE_TILE = 64
DISPATCH_TILE = 256
ATTN_Q_TILES = 8
MIB = 1024 * 1024
V7X_VMEM_BYTES = 64 * MIB
VMEM_LIMIT = V7X_VMEM_BYTES - 8 * MIB
MATMUL_VMEM_LIMIT = V7X_VMEM_BYTES - 4 * MIB
MATMUL_VMEM_BUDGET = MATMUL_VMEM_LIMIT - 7 * MIB

F32 = jnp.float32
BF16 = jnp.bfloat16


def _pick(dim, candidates):
    for c in candidates:
        if dim % c == 0:
            return c
    raise ValueError(f"no tile in {candidates} divides {dim}")


def _params(sem, vmem=VMEM_LIMIT):
    return pltpu.CompilerParams(dimension_semantics=sem, vmem_limit_bytes=vmem)


def _dot(a, b):
    return jnp.dot(a, b, preferred_element_type=F32)


def _dot_nt(a, b):
    return lax.dot_general(a, b, (((1,), (1,)), ((), ())), preferred_element_type=F32)


def _dot_tn(a, b):
    return lax.dot_general(a, b, (((0,), (0,)), ((), ())), preferred_element_type=F32)


def _silu(x):
    return x * jax.nn.sigmoid(x)


def _pack_halves(y):
    w = y.shape[-1] // 2
    lo = lax.bitcast_convert_type(y[:, :w].astype(BF16).astype(F32), jnp.uint32)
    hi = lax.bitcast_convert_type(y[:, w:].astype(BF16).astype(F32), jnp.uint32)
    return (hi & jnp.uint32(0xFFFF0000)) | (lo >> 16)


def _unpack_halves(p):
    lo = lax.bitcast_convert_type(p << 16, F32)
    hi = lax.bitcast_convert_type(p & jnp.uint32(0xFFFF0000), F32)
    return lo, hi


def _ada_kernel(c_ref, w_ref, b_ref, o_ref):
    a = _silu(c_ref[...]).astype(BF16)
    o_ref[...] = _dot(a, w_ref[...].astype(BF16)) + b_ref[...]


def ada_modulation(cc, ada_w, ada_b):
    depth, d, n = ada_w.shape
    tn = _pick(n, (512, 256, 128))
    out = pl.pallas_call(
        _ada_kernel,
        grid=(depth, n // tn),
        in_specs=[pl.BlockSpec((MOD_ROWS, d), lambda l, j: (0, 0)),
                  pl.BlockSpec((None, d, tn), lambda l, j: (l, 0, j)),
                  pl.BlockSpec((None, 1, tn), lambda l, j: (l, 0, j))],
        out_specs=pl.BlockSpec((None, MOD_ROWS, tn), lambda l, j: (l, 0, j)),
        out_shape=jax.ShapeDtypeStruct((depth, MOD_ROWS, n), F32),
        compiler_params=_params(("parallel", "parallel")),
        name="ada_modulation",
    )(cc, ada_w, ada_b.reshape(depth, 1, n))
    return out.reshape(depth, MOD_ROWS, N_ADA, d)


def _rms(x):
    return x * lax.rsqrt(jnp.mean(x * x, axis=-1, keepdims=True) + NORM_EPS)


def _residual(x, y, mod_ref, gate_idx, wpost_ref):
    return x + _rms(y) * (mod_ref[gate_idx:gate_idx + 1, :] * wpost_ref[...])


def _prenorm(x, mod_ref, shift_idx, scale_idx, wpre_ref):
    return (_rms(x) * (wpre_ref[...] * (1.0 + mod_ref[scale_idx:scale_idx + 1, :]))
            + mod_ref[shift_idx:shift_idx + 1, :])


def _route(h, rw_ref, rb_ref, idx_ref, wgt_ref, rank_ref, cnt_ref):
    n_exp = rw_ref.shape[0]
    tm = h.shape[0]
    per_group = n_exp // MOE_GROUPS
    w = rw_ref[...]
    w_hi = w.astype(BF16)
    w_lo = (w - w_hi.astype(F32)).astype(BF16)
    h_hi = h.astype(BF16)
    h_lo = (h - h_hi.astype(F32)).astype(BF16)
    logits = _dot_nt(w_hi, h_hi) + (_dot_nt(w_hi, h_lo) + _dot_nt(w_lo, h_hi))
    scores = jax.nn.sigmoid(logits)
    biased = scores + rb_ref[...]
    neg = jnp.float32(-jnp.inf)
    sub = lax.broadcasted_iota(jnp.int32, (per_group, tm), 0)
    giota = lax.broadcasted_iota(jnp.int32, (MOE_GROUPS, tm), 0)
    gs = jnp.zeros((MOE_GROUPS, tm), F32)
    for g in range(MOE_GROUPS):
        blk = biased[g * per_group:(g + 1) * per_group]
        m1 = jnp.max(blk, axis=0, keepdims=True)
        i1 = jnp.min(jnp.where(blk == m1, sub, per_group), axis=0, keepdims=True)
        m2 = jnp.max(jnp.where(sub == i1, neg, blk), axis=0, keepdims=True)
        gs = jnp.where(giota == g, m1 + m2, gs)
    rank = jnp.zeros((MOE_GROUPS, tm), jnp.int32)
    for j in range(MOE_GROUPS):
        gj = gs[j:j + 1]
        beats = (gj > gs) | ((gj == gs) & (giota > j))
        rank = rank + beats.astype(jnp.int32)
    keep = (rank < MOE_TOPK_GROUPS).astype(F32)
    keep_e = jnp.concatenate(
        [jnp.broadcast_to(keep[g:g + 1], (per_group, tm)) for g in range(MOE_GROUPS)], axis=0)
    masked = jnp.where(keep_e > 0.5, biased, neg)
    eiota = lax.broadcasted_iota(jnp.int32, (n_exp, tm), 0)
    sel_w, hits = [], []
    for k in range(MOE_TOPK):
        m = jnp.max(masked, axis=0, keepdims=True)
        idx = jnp.min(jnp.where(masked == m, eiota, n_exp), axis=0, keepdims=True)
        hit = eiota == idx
        hits.append(hit)
        sel_w.append(jnp.sum(jnp.where(hit, scores, 0.0), axis=0, keepdims=True))
        masked = jnp.where(hit, neg, masked)
        idx_ref[k:k + 1, :] = idx
    total = sel_w[0]
    for k in range(1, MOE_TOPK):
        total = total + sel_w[k]
    for k in range(MOE_TOPK):
        wgt_ref[k:k + 1, :] = sel_w[k] / total * ROUTED_SCALE
    chosen = jnp.zeros((n_exp, tm), F32)
    for k in range(MOE_TOPK):
        chosen = jnp.where(hits[k], 1.0, chosen)
    before = (lax.broadcasted_iota(jnp.int32, (tm, tm), 0)
              < lax.broadcasted_iota(jnp.int32, (tm, tm), 1)).astype(BF16)
    prefix = _dot(chosen.astype(BF16), before)
    for k in range(MOE_TOPK):
        rank_ref[k:k + 1, :] = jnp.sum(jnp.where(hits[k], prefix, 0.0), axis=0,
                                       keepdims=True).astype(jnp.int32)
    cnt_ref[...] = jnp.broadcast_to(jnp.sum(chosen, axis=1, keepdims=True),
                                    cnt_ref.shape).astype(jnp.int32)


def _mixer_norm_kernel(x_ref, y_ref, mod_ref, wpost_ref, wpre_ref, rw_ref, rb_ref,
                       xo_ref, h_ref, idx_ref, wgt_ref, rank_ref, cnt_ref, *, gate_idx, shift_idx, scale_idx):
    x = _residual(x_ref[...], y_ref[...].astype(F32), mod_ref, gate_idx, wpost_ref)
    xo_ref[...] = x
    h = _prenorm(x, mod_ref, shift_idx, scale_idx, wpre_ref)
    h_ref[...] = h.astype(BF16)
    _route(h, rw_ref, rb_ref, idx_ref, wgt_ref, rank_ref, cnt_ref)


def _ingest_kernel(x_ref, c_ref, mod_ref, wpre_ref, xs_ref, h_ref, *, lat_tiles, all_tiles, shift_idx,
                   scale_idx):
    is_latent = pl.program_id(0) % all_tiles < lat_tiles

    def emit(x):
        xs_ref[...] = x
        h_ref[...] = _prenorm(x, mod_ref, shift_idx, scale_idx, wpre_ref).astype(BF16)

    @pl.when(is_latent)
    def _():
        emit(x_ref[...])

    @pl.when(jnp.logical_not(is_latent))
    def _():
        emit(c_ref[...])


def ingest(x, ctx, mod_row_map, mod, w_pre, shift_idx, scale_idx):
    batch, t_len, d = x.shape
    c_len = ctx.shape[1]
    tm = ROW_TILE
    lat_tiles, ctx_tiles = t_len // tm, c_len // tm
    all_tiles = lat_tiles + ctx_tiles
    x_map = lambda i: ((i // all_tiles) * lat_tiles + jnp.minimum(i % all_tiles, lat_tiles - 1), 0)
    c_map = lambda i: ((i // all_tiles) * ctx_tiles + jnp.maximum(i % all_tiles - lat_tiles, 0), 0)
    row = lambda i: (i, 0)
    kern = functools.partial(_ingest_kernel, lat_tiles=lat_tiles, all_tiles=all_tiles, shift_idx=shift_idx,
                             scale_idx=scale_idx)
    n_rows = batch * all_tiles * tm
    return pl.pallas_call(
        kern, grid=(batch * all_tiles,),
        in_specs=[pl.BlockSpec((tm, d), x_map), pl.BlockSpec((tm, d), c_map),
                  pl.BlockSpec((None, N_ADA, d), lambda i: (mod_row_map(i), 0, 0)),
                  pl.BlockSpec((1, d), lambda i: (0, 0))],
        out_specs=[pl.BlockSpec((tm, d), row), pl.BlockSpec((tm, d), row)],
        out_shape=[jax.ShapeDtypeStruct((n_rows, d), F32), jax.ShapeDtypeStruct((n_rows, d), BF16)],
        compiler_params=_params(("arbitrary",)), name="ingest",
    )(x.reshape(batch * t_len, d), ctx.reshape(batch * c_len, d), mod, w_pre.reshape(1, d))


def mixer_norm(x, x_tile_map, n_out_tiles, mod_row_map, y, mod, w_post, w_pre, router_wt, router_b, *,
               gate_idx, shift_idx, scale_idx):
    d = x.shape[-1]
    tm = ROW_TILE
    n_exp = router_wt.shape[0]
    n_rows = n_out_tiles * tm
    row = lambda i: (i, 0)
    const = lambda i: (0, 0)
    per_token = pl.BlockSpec((MOE_TOPK, tm), lambda i: (0, i))
    kern = functools.partial(_mixer_norm_kernel, gate_idx=gate_idx, shift_idx=shift_idx, scale_idx=scale_idx)
    return pl.pallas_call(
        kern, grid=(n_out_tiles,),
        in_specs=[pl.BlockSpec((tm, d), lambda i: (x_tile_map(i), 0)), pl.BlockSpec((tm, d), row),
                  pl.BlockSpec((None, N_ADA, d), lambda i: (mod_row_map(i), 0, 0)),
                  pl.BlockSpec((1, d), const), pl.BlockSpec((1, d), const),
                  pl.BlockSpec((n_exp, d), const), pl.BlockSpec((n_exp, 1), const)],
        out_specs=[pl.BlockSpec((tm, d), row), pl.BlockSpec((tm, d), row),
                   per_token, per_token, per_token,
                   pl.BlockSpec((None, n_exp, LANES), lambda i: (i, 0, 0))],
        out_shape=[jax.ShapeDtypeStruct((n_rows, d), F32), jax.ShapeDtypeStruct((n_rows, d), BF16),
                   jax.ShapeDtypeStruct((MOE_TOPK, n_rows), jnp.int32),
                   jax.ShapeDtypeStruct((MOE_TOPK, n_rows), F32),
                   jax.ShapeDtypeStruct((MOE_TOPK, n_rows), jnp.int32),
                   jax.ShapeDtypeStruct((n_out_tiles, n_exp, LANES), jnp.int32)],
        compiler_params=_params(("parallel",)), name="mixer_norm",
    )(x, y, mod, w_post.reshape(1, d), w_pre.reshape(1, d), router_wt, router_b.reshape(n_exp, 1))


def _mm_kernel(a_ref, w_ref, o_ref):
    o_ref[...] = _dot(a_ref[...], w_ref[...].astype(BF16)).astype(o_ref.dtype)


def _mm_acc_kernel(a_ref, w_ref, o_ref, acc_ref):
    k = pl.program_id(2)

    @pl.when(k == 0)
    def _():
        acc_ref[...] = jnp.zeros_like(acc_ref)

    acc_ref[...] += _dot(a_ref[...], w_ref[...].astype(BF16))

    @pl.when(k == pl.num_programs(2) - 1)
    def _():
        o_ref[...] = acc_ref[...].astype(o_ref.dtype)


def matmul(a, w, layer, out_dtype=BF16):
    m, k = a.shape
    n = w.shape[2]
    tm = _pick(m, (1024, 768, 512, 256))

    def vmem_bytes(tn, tk):
        return (2 * tm * tk * 2 + 2 * tk * tn * 4 + tk * tn * 2 + 2 * tm * tn * 2
                + (tm * tn * 4 if tk < k else 0))

    tn, tk = next(((tn, tk) for tk in (k, k // 2, k // 4) for tn in (512, 256)
                   if n % tn == 0 and tk % LANES == 0 and vmem_bytes(tn, tk) <= MATMUL_VMEM_BUDGET),
                  (_pick(n, (256, 128)), _pick(k, (1024, 512, 256, 128))))
    if tk == k:
        return pl.pallas_call(
            _mm_kernel, grid=(m // tm, n // tn),
            in_specs=[pl.BlockSpec((tm, k), lambda i, j: (i, 0)),
                      pl.BlockSpec((None, k, tn), lambda i, j: (layer, 0, j))],
            out_specs=pl.BlockSpec((tm, tn), lambda i, j: (i, j)),
            out_shape=jax.ShapeDtypeStruct((m, n), out_dtype),
            compiler_params=_params(("parallel", "parallel"), MATMUL_VMEM_LIMIT), name="matmul",
        )(a, w)
    return pl.pallas_call(
        _mm_acc_kernel, grid=(m // tm, n // tn, k // tk),
        in_specs=[pl.BlockSpec((tm, tk), lambda i, j, l: (i, l)),
                  pl.BlockSpec((None, tk, tn), lambda i, j, l: (layer, l, j))],
        out_specs=pl.BlockSpec((tm, tn), lambda i, j, l: (i, j)),
        out_shape=jax.ShapeDtypeStruct((m, n), out_dtype),
        scratch_shapes=[pltpu.VMEM((tm, tn), F32)],
        compiler_params=_params(("parallel", "parallel", "arbitrary")), name="matmul_acc",
    )(a, w)


def _rope_tables(t_len, head_dim):
    rows = t_len // GRID_W
    n_freq = head_dim // 4
    row, col = jnp.meshgrid(jnp.arange(rows, dtype=F32), jnp.arange(GRID_W, dtype=F32), indexing="ij")
    inv_freq = ROPE_BASE ** (-jnp.arange(n_freq, dtype=F32) / n_freq)
    ang_r = row.reshape(-1, 1) * inv_freq
    ang_c = col.reshape(-1, 1) * inv_freq
    cr, sr, cc, sc = jnp.cos(ang_r), jnp.sin(ang_r), jnp.cos(ang_c), jnp.sin(ang_c)
    return (jnp.concatenate([cr, cr, cc, cc], axis=-1), jnp.concatenate([-sr, sr, -sc, sc], axis=-1))


def _swap_quarters(x, quarter):
    lane = lax.broadcasted_iota(jnp.int32, x.shape, 1)
    first = (lane % (2 * quarter)) < quarter
    return jnp.where(first, pltpu.roll(x, LANES - quarter, axis=1), pltpu.roll(x, quarter, axis=1))


def _retention_kernel(lgf_ref, lgb_ref, q_ref, k_ref, v_ref, g_ref, cos_ref, sin_ref, o_ref,
                      qr, kr, oacc_f, oacc_b, state_f, state_b, *, t_len, c_len):
    head = pl.program_id(1)
    L = RET_BLOCK
    dk = q_ref.shape[-1]
    k_scale = dk ** -0.5

    def rope(x, rows):
        sw = jnp.concatenate([pltpu.roll(x[:, :LANES], LANES // 2, axis=1),
                              pltpu.roll(x[:, LANES:], LANES // 2, axis=1)], axis=1)
        return x * cos_ref[rows, :] + sw * sin_ref[rows, :]

    def prepare(row0):
        rows = slice(row0, row0 + L)
        if row0 < t_len:
            qr[rows, :] = rope(q_ref[rows, :].astype(F32), rows).astype(BF16)
            kr[rows, :] = (rope(k_ref[rows, :].astype(F32), rows) * k_scale).astype(BF16)
        else:
            qr[rows, :] = q_ref[rows, :]
            kr[rows, :] = (k_ref[rows, :].astype(F32) * k_scale).astype(BF16)

    ii = lax.broadcasted_iota(jnp.int32, (L, L), 0)
    jj = lax.broadcasted_iota(jnp.int32, (L, L), 1)
    rel = (ii - jj).astype(F32)
    idx = lax.broadcasted_iota(jnp.int32, (L, 1), 0).astype(F32)

    lg_f, lg_b = lgf_ref[head], lgb_ref[head]
    fwd = (jnp.where(rel >= 0, jnp.exp(lg_f * jnp.maximum(rel, 0.0)), 0.0),
           jnp.exp(lg_f * (idx + 1.0)),
           jnp.exp(lg_f * (L - 1.0 - idx)),
           jnp.exp(lg_f * L), state_f, oacc_f)
    bwd = (jnp.where(rel <= 0, jnp.exp(lg_b * jnp.maximum(-rel, 0.0)), 0.0),
           jnp.exp(lg_b * (L - idx)),
           jnp.exp(lg_b * idx),
           jnp.exp(lg_b * L), state_b, oacc_b)

    def chunk(row0, direction):
        dmat, q_decay, k_decay, chunk_decay, state, oacc = direction
        rows = slice(row0, row0 + L)
        qb, kb, vb = qr[rows, :], kr[rows, :], v_ref[rows, :]
        scores = _dot_nt(qb, kb) * dmat
        inner = _dot(scores.astype(BF16), vb)
        st = state[...]
        cross = _dot(qb, st.astype(BF16)) * q_decay
        oacc[rows, :] = inner + cross
        kd = (kb.astype(F32) * k_decay).astype(BF16)
        state[...] = st * chunk_decay + _dot_tn(kd, vb)

    def finish(row0):
        rows = slice(row0, row0 + L)
        o = _rms(oacc_f[rows, :] + oacc_b[rows, :])
        o_ref[rows, :] = (_silu(g_ref[rows, :].astype(F32)) * o).astype(BF16)

    def scan(r0, n_chunks):
        ready, done_f, done_b = set(), set(), set()
        for ci in range(n_chunks):
            cf, cb = ci, n_chunks - 1 - ci
            for c in (cf, cb):
                if c not in ready:
                    prepare(r0 + c * L)
                    ready.add(c)
            chunk(r0 + cf * L, fwd)
            chunk(r0 + cb * L, bwd)
            done_f.add(cf)
            done_b.add(cb)
            for c in sorted({cf, cb}):
                if c in done_f and c in done_b:
                    finish(r0 + c * L)

    state_f[...] = jnp.zeros_like(state_f)
    state_b[...] = jnp.zeros_like(state_b)
    scan(t_len, c_len // L)
    scan(0, t_len // L)


def retention(qkvg, lg_f, lg_b, cos, sin, *, batch, t_len, c_len, heads):
    s_len = t_len + c_len
    dk = cos.shape[-1]
    dv = 2 * dk
    assert dk == 2 * LANES
    kern = functools.partial(_retention_kernel, t_len=t_len, c_len=c_len)
    grid_spec = pltpu.PrefetchScalarGridSpec(
        num_scalar_prefetch=2, grid=(batch, heads),
        in_specs=[pl.BlockSpec((s_len, dk), lambda b, h, *_: (b, h)),
                  pl.BlockSpec((s_len, dk), lambda b, h, *_: (b, heads + h)),
                  pl.BlockSpec((s_len, dv), lambda b, h, *_: (b, heads + h)),
                  pl.BlockSpec((s_len, dv), lambda b, h, *_: (b, 2 * heads + h)),
                  pl.BlockSpec((t_len, dk), lambda b, h, *_: (0, 0)),
                  pl.BlockSpec((t_len, dk), lambda b, h, *_: (0, 0))],
        out_specs=pl.BlockSpec((s_len, dv), lambda b, h, *_: (b, h)),
        scratch_shapes=[pltpu.VMEM((s_len, dk), BF16), pltpu.VMEM((s_len, dk), BF16),
                        pltpu.VMEM((s_len, dv), F32), pltpu.VMEM((s_len, dv), F32),
                        pltpu.VMEM((dk, dv), F32), pltpu.VMEM((dk, dv), F32)])
    return pl.pallas_call(
        kern, grid_spec=grid_spec,
        out_shape=jax.ShapeDtypeStruct((batch * s_len, heads * dv), BF16),
        compiler_params=_params(("parallel", "parallel")), name="retention",
    )(lg_f, lg_b, qkvg, qkvg, qkvg, qkvg, cos, sin)


def _diff_attn_kernel(lam_ref, *refs, n_q, t_len, c_len, lambda_init):
    q_refs = refs[:n_q]
    k_ref, v_ref, cosq_ref, sinq_ref, cosk_ref, sink_ref, subln_ref, o_ref, kr = refs[n_q:]
    hd = DIFF_HEAD_DIM
    quarter = hd // 4
    rt = ROW_TILE

    def rope(x, cos, sin):
        return x * cos + _swap_quarters(x, quarter) * sin

    @pl.when(pl.program_id(2) == 0)
    def _():
        def body(i, _):
            rows = pl.ds(pl.multiple_of(i * rt, rt), rt)
            for c in range(2):
                cols = slice(c * hd, (c + 1) * hd)
                kr[rows, cols] = rope(k_ref[rows, cols].astype(F32), cosk_ref[rows, :],
                                      sink_ref[rows, :]).astype(BF16)
            return 0
        lax.fori_loop(0, t_len // rt, body, 0)
        ctx_rows = pl.ds(t_len, c_len)
        kr[ctx_rows, :] = k_ref[ctx_rows, :]

    lam_v = lam_ref[...]
    lam = (jnp.exp(jnp.sum(lam_v[0:1] * lam_v[1:2], axis=-1, keepdims=True))
           - jnp.exp(jnp.sum(lam_v[2:3] * lam_v[3:4], axis=-1, keepdims=True)) + lambda_init)
    q_scale = (hd ** -0.5) * math.log2(math.e)
    tq = q_refs[0].shape[0]
    for part, q_ref in enumerate(q_refs):
        rows = slice(part * tq, (part + 1) * tq)
        outs = []
        for c in range(2):
            cols = slice(c * hd, (c + 1) * hd)
            qc = (rope(q_ref[:, cols].astype(F32), cosq_ref[rows, :], sinq_ref[rows, :])
                  * q_scale).astype(BF16)
            s = _dot_nt(qc, kr[:, cols])
            e = jnp.exp2(s - jnp.max(s, axis=-1, keepdims=True))
            denom = jnp.sum(e, axis=-1, keepdims=True)
            outs.append(_dot(e.astype(BF16), v_ref[...]) / denom)
        o = outs[0] - lam * outs[1]
        o_ref[rows, :] = ((_rms(o) * subln_ref[...]) * (1.0 - lambda_init)).astype(BF16)


def diff_attention(qkv, lam_vecs, subln_w, cos, sin, *, batch, t_len, c_len, heads, lambda_init):
    s_len = t_len + c_len
    hd = DIFF_HEAD_DIM
    tq = ROW_TILE
    n_q = _pick(t_len // tq, (ATTN_Q_TILES, 2, 1))
    steps, s_tiles = t_len // (n_q * tq), s_len // tq
    kern = functools.partial(_diff_attn_kernel, n_q=n_q, t_len=t_len, c_len=c_len, lambda_init=lambda_init)
    q_specs = [pl.BlockSpec((tq, 2 * hd), lambda b, h, i, j=j: (b * s_tiles + n_q * i + j, h))
               for j in range(n_q)]
    return pl.pallas_call(
        kern, grid=(batch, heads, steps),
        in_specs=[pl.BlockSpec((4, hd), lambda b, h, i: (0, 0)), *q_specs,
                  pl.BlockSpec((s_len, 2 * hd), lambda b, h, i: (b, heads + h)),
                  pl.BlockSpec((s_len, 2 * hd), lambda b, h, i: (b, 2 * heads + h)),
                  pl.BlockSpec((n_q * tq, hd), lambda b, h, i: (i, 0)),
                  pl.BlockSpec((n_q * tq, hd), lambda b, h, i: (i, 0)),
                  pl.BlockSpec((t_len, hd), lambda b, h, i: (0, 0)),
                  pl.BlockSpec((t_len, hd), lambda b, h, i: (0, 0)),
                  pl.BlockSpec((1, 2 * hd), lambda b, h, i: (0, 0))],
        out_specs=pl.BlockSpec((n_q * tq, 2 * hd), lambda b, h, i: (b * steps + i, h)),
        out_shape=jax.ShapeDtypeStruct((batch * t_len, heads * 2 * hd), BF16),
        scratch_shapes=[pltpu.VMEM((s_len, 2 * hd), BF16)],
        compiler_params=_params(("parallel", "parallel", "arbitrary")), name="diff_attention",
    )(lam_vecs, *([qkv] * (n_q + 2)), cos, sin, cos, sin, subln_w.reshape(1, 2 * hd))


def _shared_kernel(x_ref, wg_ref, wu_ref, wd_ref, o_ref, acc_ref):
    f = pl.program_id(1)

    @pl.when(f == 0)
    def _():
        acc_ref[...] = jnp.zeros_like(acc_ref)

    x = x_ref[...]
    hid = _silu(_dot(x, wg_ref[...])) * _dot(x, wu_ref[...])
    acc_ref[...] += _dot(hid.astype(BF16), wd_ref[...])

    @pl.when(f == pl.num_programs(1) - 1)
    def _():
        o_ref[...] = acc_ref[...].astype(o_ref.dtype)


def shared_expert(h, w_gate, w_up, w_down):
    m, d = h.shape
    f_dim = w_gate.shape[1]
    tm = _pick(m, (512, 256))
    tf = _pick(f_dim, (512, 256, 128))
    return pl.pallas_call(
        _shared_kernel, grid=(m // tm, f_dim // tf),
        in_specs=[pl.BlockSpec((tm, d), lambda i, f: (i, 0)),
                  pl.BlockSpec((d, tf), lambda i, f: (0, f)),
                  pl.BlockSpec((d, tf), lambda i, f: (0, f)),
                  pl.BlockSpec((tf, d), lambda i, f: (f, 0))],
        out_specs=pl.BlockSpec((tm, d), lambda i, f: (i, 0)),
        out_shape=jax.ShapeDtypeStruct((m, d), BF16),
        scratch_shapes=[pltpu.VMEM((tm, d), F32)],
        compiler_params=_params(("parallel", "arbitrary")), name="shared_expert",
    )(h, w_gate, w_up, w_down)


def _dispatch_plan(cnt, pairs, tm):
    n_exp = cnt.shape[1]
    total = jnp.sum(cnt, axis=0)
    padded = (total + tm - 1) // tm * tm
    pend = jnp.cumsum(padded)
    base = (pend - padded)[None, :] + jnp.cumsum(cnt, axis=0) - cnt
    n_tiles = pairs // tm + n_exp
    tile_start = jnp.arange(n_tiles, dtype=jnp.int32) * tm
    valid = tile_start < pend[-1]
    tile_e = jnp.minimum(jnp.sum((tile_start[:, None] >= pend[None, :]).astype(jnp.int32), axis=1), n_exp - 1)
    tile_e = jnp.where(valid, tile_e, jnp.max(jnp.where(valid, tile_e, 0)))
    next_e = jnp.concatenate([tile_e[1:], jnp.full((1,), -1, jnp.int32)])
    next_valid = jnp.concatenate([valid[1:], jnp.zeros((1,), bool)])
    zero_fill = (~valid) | (tile_e != next_e) | (~next_valid)
    return (tile_e.astype(jnp.int32), valid.astype(jnp.int32), zero_fill.astype(jnp.int32),
            base.astype(jnp.int32))


def _positions_kernel(idx_ref, rank_ref, base_ref, pos_ref):
    n_exp = base_ref.shape[0]
    tm = idx_ref.shape[1]
    eiota = lax.broadcasted_iota(jnp.int32, (n_exp, tm), 0)
    base = jnp.broadcast_to(base_ref[...].astype(F32), (n_exp, tm))
    for k in range(MOE_TOPK):
        hit = eiota == idx_ref[k:k + 1, :]
        first = jnp.sum(jnp.where(hit, base, 0.0), axis=0, keepdims=True)
        pos_ref[k:k + 1, :] = first.astype(jnp.int32) + rank_ref[k:k + 1, :]


def pair_positions(idx_t, rank_t, base):
    k, n = idx_t.shape
    tiles, n_exp = base.shape
    tm = n // tiles
    spec = pl.BlockSpec((k, tm), lambda i: (0, i))
    return pl.pallas_call(
        _positions_kernel, grid=(tiles,),
        in_specs=[spec, spec, pl.BlockSpec((None, n_exp, 1), lambda i: (i, 0, 0))],
        out_specs=spec, out_shape=jax.ShapeDtypeStruct((k, n), jnp.int32),
        compiler_params=_params(("parallel",)), name="pair_positions",
    )(idx_t, rank_t, base.reshape(tiles, n_exp, 1))


def _dispatch_kernel(pos_ref, zf_ref, h_ref, wg_ref, wu_ref, wd_ref, xs_hbm, wgo_ref, wuo_ref, wdo_ref,
                     hp_ref, zeros, sem_z, sem_s, *, n_tok, n_tiles):
    i = pl.program_id(0)
    td = hp_ref.shape[0]
    tm = zeros.shape[0]
    hp_ref[...] = _pack_halves(h_ref[...].astype(F32))
    wgo_ref[...] = wg_ref[...].astype(BF16)
    wuo_ref[...] = wu_ref[...].astype(BF16)
    wdo_ref[...] = wd_ref[...].astype(BF16)

    def zero_copy(j):
        return pltpu.make_async_copy(zeros, xs_hbm.at[pl.ds(pl.multiple_of(j * tm, tm), tm)], sem_z)

    @pl.when(i == 0)
    def _():
        zeros[...] = jnp.zeros_like(zeros)

        def start(j, _):
            @pl.when(zf_ref[j] == 1)
            def _():
                zero_copy(j).start()
            return 0

        def wait(j, _):
            @pl.when(zf_ref[j] == 1)
            def _():
                zero_copy(j).wait()
            return 0

        lax.fori_loop(0, n_tiles, start, 0)
        lax.fori_loop(0, n_tiles, wait, 0)

    def body(t, _):
        for k in range(MOE_TOPK):
            row = pos_ref[k * n_tok + i * td + t]
            pltpu.make_async_copy(hp_ref.at[pl.ds(t, 1)], xs_hbm.at[pl.ds(row, 1)], sem_s).start(priority=k % 2)
        return 0

    lax.fori_loop(0, td, body, 0, unroll=2)
    for k in range(MOE_TOPK):
        pltpu.make_async_copy(hp_ref, xs_hbm.at[pl.ds(0, td)], sem_s).wait()


def dispatch(h, pos, zero_fill, n_tiles, sh_gate, sh_up, sh_down, layer):
    n_tok, half = h.shape[0], h.shape[1] // 2
    tm = EXPERT_TILE
    steps = n_tok // DISPATCH_TILE
    _, d, f = sh_gate.shape
    bf16_rows = 2 * SUBLANES
    nblk = max(n for n in range(1, steps + 1)
               if d % n == 0 and f % n == 0 and (d // n) % bf16_rows == 0 and (f // n) % bf16_rows == 0)
    wblk = lambda i, p, z: (layer, jnp.minimum(i, nblk - 1), 0)
    oblk = lambda i, p, z: (jnp.minimum(i, nblk - 1), 0)
    grid_spec = pltpu.PrefetchScalarGridSpec(
        num_scalar_prefetch=2, grid=(steps,),
        in_specs=[pl.BlockSpec((DISPATCH_TILE, 2 * half), lambda i, p, z: (i, 0)),
                  pl.BlockSpec((None, d // nblk, f), wblk),
                  pl.BlockSpec((None, d // nblk, f), wblk),
                  pl.BlockSpec((None, f // nblk, d), wblk)],
        out_specs=[pl.BlockSpec(memory_space=pl.ANY),
                   pl.BlockSpec((d // nblk, f), oblk),
                   pl.BlockSpec((d // nblk, f), oblk),
                   pl.BlockSpec((f // nblk, d), oblk)],
        scratch_shapes=[pltpu.VMEM((DISPATCH_TILE, half), jnp.uint32), pltpu.VMEM((tm, half), jnp.uint32),
                        pltpu.SemaphoreType.DMA(()), pltpu.SemaphoreType.DMA(())])
    return pl.pallas_call(
        functools.partial(_dispatch_kernel, n_tok=n_tok, n_tiles=n_tiles), grid_spec=grid_spec,
        out_shape=[jax.ShapeDtypeStruct((n_tiles * tm, half), jnp.uint32),
                   jax.ShapeDtypeStruct((d, f), BF16), jax.ShapeDtypeStruct((d, f), BF16),
                   jax.ShapeDtypeStruct((f, d), BF16)],
        compiler_params=_params(("arbitrary",)), name="moe_dispatch",
    )(pos, zero_fill, h, sh_gate, sh_up, sh_down)


def _expert_kernel(te_ref, tv_ref, first_ref, slot_ref, next_ref, src_ref, x_ref, wg_hbm, wu_hbm, wd_hbm, o_ref,
                   stage_g, stage_u, stage_d, wgb, wub, wdb, sems, *, layer):
    i = pl.program_id(0)

    def weight_copies(e, s):
        return (pltpu.make_async_copy(wg_hbm.at[layer, e], stage_g.at[s], sems.at[s]),
                pltpu.make_async_copy(wu_hbm.at[layer, e], stage_u.at[s], sems.at[s]),
                pltpu.make_async_copy(wd_hbm.at[layer, e], stage_d.at[s], sems.at[s]))

    @pl.when(i == 0)
    def _():
        for cp in weight_copies(te_ref[0], 0):
            cp.start()

    @pl.when(first_ref[i] == 1)
    def _():
        s = slot_ref[i]
        for cp in weight_copies(te_ref[i], s):
            cp.wait()

        @pl.when(next_ref[i] >= 0)
        def _():
            for cp in weight_copies(next_ref[i], 1 - s):
                cp.start()

        wgb[...] = stage_g[s].astype(BF16)
        wub[...] = stage_u[s].astype(BF16)
        wdb[...] = stage_d[s].astype(BF16)

    @pl.when(tv_ref[i] == 1)
    def _():
        lo, hi = _unpack_halves(x_ref[...])
        x = jnp.concatenate([lo.astype(BF16), hi.astype(BF16)], axis=1)
        hid = _silu(_dot(x, wgb[...])) * _dot(x, wub[...])
        o_ref[...] = _pack_halves(_dot(hid.astype(BF16), wdb[...]))

    @pl.when(tv_ref[i] == 0)
    def _():
        o_ref[...] = jnp.zeros_like(o_ref)


def routed_experts(xs, tile_e, tile_valid, w_gate, w_up, w_down, layer):
    tm = EXPERT_TILE
    n_tiles = tile_e.shape[0]
    _, n_exp, d, f = w_gate.shape
    prev_e = jnp.concatenate([jnp.full((1,), -1, jnp.int32), tile_e[:-1]])
    first = (tile_e != prev_e).astype(jnp.int32)
    slot = (jnp.cumsum(first) - 1) % 2
    tiles = jnp.arange(n_tiles, dtype=jnp.int32)
    run_start = jnp.where(first == 1, tiles, n_tiles)
    next_start = jnp.min(jnp.where(run_start[None, :] > tiles[:, None], run_start[None, :], n_tiles), axis=1)
    next_e = jnp.where(next_start < n_tiles, tile_e[jnp.minimum(next_start, n_tiles - 1)], -1)
    idx_map = lambda i, *_: (i, 0)
    src_tile = jnp.where(tile_valid == 1, tiles, jnp.sum(tile_valid) - 1).astype(jnp.int32)
    src_map = lambda i, te, tv, fi, sl, nx, st: (st[i], 0)
    anyspace = pl.BlockSpec(memory_space=pl.ANY)
    grid_spec = pltpu.PrefetchScalarGridSpec(
        num_scalar_prefetch=6, grid=(n_tiles,),
        in_specs=[pl.BlockSpec((tm, d // 2), src_map), anyspace, anyspace, anyspace],
        out_specs=pl.BlockSpec((tm, d // 2), idx_map),
        scratch_shapes=[pltpu.VMEM((2, d, f), F32), pltpu.VMEM((2, d, f), F32), pltpu.VMEM((2, f, d), F32),
                        pltpu.VMEM((d, f), BF16), pltpu.VMEM((d, f), BF16), pltpu.VMEM((f, d), BF16),
                        pltpu.SemaphoreType.DMA((2,))])
    return pl.pallas_call(
        functools.partial(_expert_kernel, layer=layer), grid_spec=grid_spec,
        out_shape=jax.ShapeDtypeStruct((n_tiles * tm, d // 2), jnp.uint32),
        compiler_params=_params(("arbitrary",)), name="routed_experts",
    )(tile_e, tile_valid, first, slot.astype(jnp.int32), next_e.astype(jnp.int32), src_tile,
      xs, w_gate, w_up, w_down)


def _combine_kernel(pos_ref, ys_hbm, sh_ref, w_ref, x_ref, mod_a_ref, wpost_ref, *refs, n_tok, has_prenorm,
                    gate_idx, shift_idx, scale_idx):
    if has_prenorm:
        mod_b_ref, wpre_ref, xo_ref, h_ref, buf_a, buf_b, sem_a, sem_b = refs
    else:
        xo_ref, buf_a, buf_b, sem_a, sem_b = refs
    i = pl.program_id(0)
    n = pl.num_programs(0)
    tc = buf_a.shape[1]
    half = sh_ref.shape[-1] // 2
    last_tile = 2 * n - 1

    def issue(tile, buf, sem):
        for t in range(tc):
            for k in range(MOE_TOPK):
                p = pos_ref[k * n_tok + tile * tc + t]
                pltpu.make_async_copy(ys_hbm.at[pl.ds(p, 1)], buf.at[k, pl.ds(t, 1)], sem).start(priority=k % 2)

    def wait(buf, sem):
        for k in range(MOE_TOPK):
            pltpu.make_async_copy(ys_hbm.at[pl.ds(0, tc)], buf.at[k], sem).wait()

    def reduce(buf, rows):
        sh = sh_ref[rows, :].astype(F32)
        w = w_ref[rows, :]
        lo_acc, hi_acc = sh[:, :half], sh[:, half:]
        for k in range(MOE_TOPK):
            lo, hi = _unpack_halves(buf[k])
            lo_acc = lo_acc + w[:, k:k + 1] * lo
            hi_acc = hi_acc + w[:, k:k + 1] * hi
        ffn = jnp.concatenate([lo_acc, hi_acc], axis=1)
        x = _residual(x_ref[rows, :], ffn, mod_a_ref, gate_idx, wpost_ref)
        xo_ref[rows, :] = x
        if has_prenorm:
            h_ref[rows, :] = _prenorm(x, mod_b_ref, shift_idx, scale_idx, wpre_ref).astype(BF16)

    @pl.when(i == 0)
    def _():
        issue(0, buf_a, sem_a)

    wait(buf_a, sem_a)
    issue(2 * i + 1, buf_b, sem_b)
    reduce(buf_a, slice(0, tc))
    wait(buf_b, sem_b)
    issue(jnp.minimum(2 * i + 2, last_tile), buf_a, sem_a)
    reduce(buf_b, slice(tc, 2 * tc))

    @pl.when(i == n - 1)
    def _():
        wait(buf_a, sem_a)


def combine(ys, pos, shared, wgt, x, mod_row_map, *, mod_a, w_post, gate_idx, mod_b=None, w_pre=None,
            shift_idx=0, scale_idx=0):
    n_tok, d = shared.shape
    tc = COMBINE_TILE
    step = 2 * tc
    has_prenorm = mod_b is not None
    buf = pltpu.VMEM((MOE_TOPK, tc, d // 2), jnp.uint32)
    row = lambda i, p: (i, 0)
    const = lambda i, p: (0, 0)
    mod_spec = pl.BlockSpec((None, N_ADA, d), lambda i, p: (mod_row_map(i * step), 0, 0))
    vec_spec = pl.BlockSpec((1, d), const)
    args = [pos, ys, shared, wgt, x, mod_a, w_post.reshape(1, d)]
    in_specs = [pl.BlockSpec(memory_space=pl.ANY), pl.BlockSpec((step, d), row),
                pl.BlockSpec((step, MOE_TOPK), row), pl.BlockSpec((step, d), row), mod_spec, vec_spec]
    out_shape = [jax.ShapeDtypeStruct((n_tok, d), F32)]
    out_specs = [pl.BlockSpec((step, d), row)]
    if has_prenorm:
        args += [mod_b, w_pre.reshape(1, d)]
        in_specs += [mod_spec, vec_spec]
        out_shape.append(jax.ShapeDtypeStruct((n_tok, d), BF16))
        out_specs.append(pl.BlockSpec((step, d), row))
    grid_spec = pltpu.PrefetchScalarGridSpec(
        num_scalar_prefetch=1, grid=(n_tok // step,), in_specs=in_specs, out_specs=out_specs,
        scratch_shapes=[buf, buf, pltpu.SemaphoreType.DMA(()), pltpu.SemaphoreType.DMA(())])
    kern = functools.partial(_combine_kernel, n_tok=n_tok, has_prenorm=has_prenorm, gate_idx=gate_idx,
                             shift_idx=shift_idx, scale_idx=scale_idx)
    return pl.pallas_call(
        kern, grid_spec=grid_spec, out_shape=out_shape,
        compiler_params=_params(("arbitrary",)), name="moe_combine",
    )(*args)


def moe_ffn(h, route, w_gate, w_up, w_down, sh_gate, sh_up, sh_down, layer, x, mod_row_map, **epilogue):
    idx_t, wgt_t, rank_t, cnt = route
    pairs = idx_t.shape[0] * idx_t.shape[1]
    tile_e, tile_valid, zero_fill, base = _dispatch_plan(cnt[:, :, 0], pairs, EXPERT_TILE)
    pos = pair_positions(idx_t, rank_t, base).reshape(pairs)
    xs, sg, su, sd = dispatch(h, pos, zero_fill, tile_e.shape[0], sh_gate, sh_up, sh_down, layer)
    ys = routed_experts(xs, tile_e, tile_valid, w_gate, w_up, w_down, layer)
    shared = shared_expert(h, sg, su, sd)
    return combine(ys, pos, shared, wgt_t.T, x, mod_row_map, **epilogue)


def kernel(x, c, ctx, c_ctx, ada_w, ada_b, norm_pre_mix, norm_post_mix, norm_pre_ffn, norm_post_ffn, ret_w_in, ret_w_out, ret_decay_fwd, ret_decay_bwd, diff_w_in, diff_w_out, diff_lam_q1, diff_lam_k1, diff_lam_q2, diff_lam_k2, diff_subln_w, moe_router_w, moe_router_b, moe_w_gate, moe_w_up, moe_w_down, moe_shared_gate, moe_shared_up, moe_shared_down):
    batch, t_len, d = x.shape
    c_len = ctx.shape[1]
    s_len = t_len + c_len
    depth = ada_w.shape[0]
    assert depth == 2 and batch + 1 <= MOD_ROWS
    assert t_len % ROW_TILE == 0 and c_len % ROW_TILE == 0 and t_len % GRID_W == 0
    ret_heads = ret_decay_fwd.shape[-1]
    diff_heads = d // (2 * DIFF_HEAD_DIM)
    lat_tiles, all_tiles = t_len // ROW_TILE, s_len // ROW_TILE

    cc = jnp.concatenate([c, c_ctx[None], jnp.zeros((MOD_ROWS - batch - 1, d), F32)], axis=0)
    mods = ada_modulation(cc, ada_w, ada_b)
    rope_ret = _rope_tables(t_len, d // ret_heads)
    rope_diff = _rope_tables(t_len, DIFF_HEAD_DIM)

    ident = lambda i: i
    uni_mod = lambda i: jnp.where(i % all_tiles < lat_tiles, i // all_tiles, batch)
    lat_mod = lambda i: i // lat_tiles
    uni_row_mod = lambda r: jnp.where(r % s_len < t_len, r // s_len, batch)
    lat_row_mod = lambda r: r // t_len
    lat_of_uni = lambda i: (i // lat_tiles) * all_tiles + i % lat_tiles

    xs, h = ingest(x, ctx, uni_mod, mods[0], norm_pre_mix[0], 0, 1)
    qkvg = matmul(h, ret_w_in, 0)
    lg_f = jax.nn.log_sigmoid(ret_decay_fwd[0].astype(F32))
    lg_b = jax.nn.log_sigmoid(ret_decay_bwd[0].astype(F32))
    r = retention(qkvg, lg_f, lg_b, *rope_ret, batch=batch, t_len=t_len, c_len=c_len, heads=ret_heads)
    y = matmul(r, ret_w_out, 0)
    xs, h, *route = mixer_norm(
        xs, ident, batch * all_tiles, uni_mod, y, mods[0], norm_post_mix[0], norm_pre_ffn[0],
        moe_router_w[0].T, moe_router_b[0], gate_idx=2, shift_idx=3, scale_idx=4)
    xs, h = moe_ffn(h, route, moe_w_gate, moe_w_up, moe_w_down,
                    moe_shared_gate, moe_shared_up, moe_shared_down, 0, xs, uni_row_mod,
                    mod_a=mods[0], w_post=norm_post_ffn[0], gate_idx=5,
                    mod_b=mods[1], w_pre=norm_pre_mix[1], shift_idx=0, scale_idx=1)

    qkv = matmul(h, diff_w_in, 0)
    lam_vecs = jnp.stack([diff_lam_q1[0], diff_lam_k1[0], diff_lam_q2[0], diff_lam_k2[0]]).astype(F32)
    lambda_init = 0.8 - 0.6 * math.exp(-0.3 * 1)
    a = diff_attention(qkv, lam_vecs, diff_subln_w[0], *rope_diff, batch=batch, t_len=t_len, c_len=c_len,
                       heads=diff_heads, lambda_init=lambda_init)
    y = matmul(a, diff_w_out, 0)
    xl, h, *route = mixer_norm(
        xs, lat_of_uni, batch * lat_tiles, lat_mod, y, mods[1], norm_post_mix[1], norm_pre_ffn[1],
        moe_router_w[1].T, moe_router_b[1], gate_idx=2, shift_idx=3, scale_idx=4)
    (out,) = moe_ffn(h, route, moe_w_gate, moe_w_up, moe_w_down,
                     moe_shared_gate, moe_shared_up, moe_shared_down, 1, xl, lat_row_mod,
                     mod_a=mods[1], w_post=norm_post_ffn[1], gate_idx=5)
    return out.reshape(batch, t_len, d)
```
